```python
import math
import jax, jax.numpy as jnp
from jax import lax
import numpy as np

D_MODEL = 1024
BATCH = 32
SEQ = 2048
DEPTH = 1
DEC_BATCH = 4
DEC_SEQ = 4096
PAST_LEN = 128

HEAD_DIM = 64
N_Q_HEADS = 16
N_KV_HEADS = 4
Q_PER_KV = N_Q_HEADS // N_KV_HEADS
ATTN_WIDTH = N_Q_HEADS * HEAD_DIM
KV_WIDTH = N_KV_HEADS * HEAD_DIM
WINDOW = 128
BAND_BLOCK = 128
NUM_BUCKETS = 32
MAX_DISTANCE = 128
NEG_INF = -1e30

SSD_EXPAND = 2
SSD_INNER = SSD_EXPAND * D_MODEL
SSD_HEAD_DIM = 64
SSD_HEADS = SSD_INNER // SSD_HEAD_DIM
SSD_GROUPS = 4
SSD_HEADS_PER_GROUP = SSD_HEADS // SSD_GROUPS
SSD_STATE = 128
SSD_CHUNK = 128
CONV_WIDTH = 5
CONV_CH = SSD_INNER + 2 * SSD_GROUPS * SSD_STATE

IN_COLS = ATTN_WIDTH + 2 * KV_WIDTH + SSD_INNER + CONV_CH + 2 * SSD_HEADS + 2 * D_MODEL

PEER_HEADS = 8
PEER_KEY_DIM = 256
PEER_HALF = PEER_KEY_DIM // 2
N_KEYS = 128
N_EXPERTS = N_KEYS * N_KEYS
PEER_TOPK = 16
PEER_TOK_BLOCK = 128

N_MOD = 6
EPS = 1e-6

kernel_name = 'hybrid_bidir_encoder_gqa_ssd_peer'


def rms_normalize(x):
    xf = x.astype(jnp.float32)
    return (xf * lax.rsqrt(jnp.mean(xf * xf, axis=-1, keepdims=True) + EPS)).astype(x.dtype)


def rms_norm(x, w):
    return rms_normalize(x) * w


def t5_bucket(rel):
    half = NUM_BUCKETS // 2
    max_exact = half // 2
    bucket = jnp.where(rel > 0, half, 0)
    n = jnp.abs(rel)
    nf = jnp.maximum(n, 1).astype(jnp.float32)
    large = max_exact + (jnp.log(nf / max_exact) / math.log(MAX_DISTANCE / max_exact)
                         * (half - max_exact)).astype(jnp.int32)
    large = jnp.minimum(large, half - 1)
    return bucket + jnp.where(n < max_exact, n, large)


def banded_window_attention(q, k, v, sink, rel_bias):
    b, l = q.shape[0], q.shape[1]
    nb = l // BAND_BLOCK
    qb = q.reshape(b, nb, BAND_BLOCK, N_KV_HEADS, Q_PER_KV, HEAD_DIM).transpose(1, 0, 2, 3, 4, 5)

    def band(t):
        tp = jnp.pad(t, ((0, 0), (BAND_BLOCK, BAND_BLOCK), (0, 0), (0, 0)))
        tp = tp.reshape(b, nb + 2, BAND_BLOCK, N_KV_HEADS, HEAD_DIM)
        tb = jnp.concatenate([tp[:, :-2], tp[:, 1:-1], tp[:, 2:]], axis=2)
        return tb.transpose(1, 0, 2, 3, 4)

    kb, vb = band(k), band(v)
    key_pos = (jnp.arange(nb)[:, None] - 1) * BAND_BLOCK + jnp.arange(3 * BAND_BLOCK)[None, :]
    key_valid = (key_pos >= 0) & (key_pos < l)
    rel = jnp.arange(3 * BAND_BLOCK)[None, :] - BAND_BLOCK - jnp.arange(BAND_BLOCK)[:, None]
    in_window = jnp.abs(rel) <= WINDOW
    bias = rel_bias[t5_bucket(rel)].astype(jnp.float32)
    bias = bias.transpose(2, 0, 1).reshape(N_KV_HEADS, Q_PER_KV, BAND_BLOCK, 3 * BAND_BLOCK)
    sink_f = sink.astype(jnp.float32).reshape(N_KV_HEADS, Q_PER_KV, 1, 1)
    scale = HEAD_DIM ** -0.5

    def one_block(args):
        qi, ki, vi, valid = args
        s = jnp.einsum('bqhgd,bkhd->bhgqk', qi, ki).astype(jnp.float32) * scale + bias
        s = jnp.where(in_window & valid[None, :], s, NEG_INF)
        m = jnp.maximum(jnp.max(s, axis=-1, keepdims=True), sink_f)
        p = jnp.exp(s - m)
        denom = jnp.sum(p, axis=-1, keepdims=True) + jnp.exp(sink_f - m)
        probs = (p / denom).astype(vi.dtype)
        return jnp.einsum('bhgqk,bkhd->bqhgd', probs, vi)

    o = lax.map(one_block, (qb, kb, vb, key_valid))
    return o.transpose(1, 0, 2, 3, 4, 5).reshape(b, l, ATTN_WIDTH)


def centred_depthwise_conv(x, w, bias):
    half = CONV_WIDTH // 2
    l = x.shape[1]
    xp = jnp.pad(x, ((0, 0), (half, half), (0, 0)))
    out = bias
    for t in range(CONV_WIDTH):
        out = out + xp[:, t:t + l] * w[t]
    return out


def ssd_chunked_scan(x, dt, a, bmat, cmat):
    b, l = x.shape[0], x.shape[1]
    nc = l // SSD_CHUNK
    Q, G, R = SSD_CHUNK, SSD_GROUPS, SSD_HEADS_PER_GROUP
    xr = x.reshape(b, nc, Q, G, R, SSD_HEAD_DIM)
    dtr = dt.reshape(b, nc, Q, G, R)
    br = bmat.reshape(b, nc, Q, G, SSD_STATE)
    cr = cmat.reshape(b, nc, Q, G, SSD_STATE)
    da = dtr.astype(jnp.float32) * a.reshape(G, R)
    cum = jnp.cumsum(da, axis=2)
    causal = jnp.arange(Q)[:, None] >= jnp.arange(Q)[None, :]
    seg = cum[:, :, :, None] - cum[:, :, None, :]
    decay_ij = jnp.exp(jnp.where(causal[:, :, None, None], seg, NEG_INF))
    xdt = xr * dtr[..., None].astype(xr.dtype)
    cb = jnp.einsum('bcign,bcjgn->bcgij', cr, br)
    y_diag = jnp.einsum('bcgij,bcijgr,bcjgrp->bcigrp', cb, decay_ij, xdt)
    decay_to_end = jnp.exp(cum[:, :, -1:] - cum)
    chunk_states = jnp.einsum('bcjgn,bcjgr,bcjgrp->bcgrpn', br, decay_to_end, xdt).astype(jnp.float32)
    chunk_decay = jnp.exp(cum[:, :, -1])

    def step(state, inp):
        st, dec = inp
        return state * dec[..., None, None] + st, state

    init = jnp.zeros((b, G, R, SSD_HEAD_DIM, SSD_STATE), jnp.float32)
    _, prev = lax.scan(step, init, (chunk_states.transpose(1, 0, 2, 3, 4, 5),
                                    chunk_decay.transpose(1, 0, 2, 3)))
    prev = prev.transpose(1, 0, 2, 3, 4, 5)
    y_off = jnp.einsum('bcign,bcgrpn,bcigr->bcigrp', cr, prev, jnp.exp(cum))
    return (y_diag + y_off).reshape(b, l, SSD_HEADS, SSD_HEAD_DIM).astype(x.dtype)


def token_mixer(h, rel_bias, w_in, q_norm_w, k_norm_w, attn_sink, conv_w, conv_b,
                a_log, dt_bias, d_skip, ssd_norm_w, w_attn_br, w_ssd_br, w_out):
    b, l = h.shape[0], h.shape[1]
    sizes = (ATTN_WIDTH, KV_WIDTH, KV_WIDTH, SSD_INNER, CONV_CH, 2 * SSD_HEADS, D_MODEL, D_MODEL)
    splits = [sum(sizes[:i]) for i in range(1, len(sizes))]
    proj = h @ w_in
    q, k, v, z, xbc, dt_raw, gate_a, gate_s = jnp.split(proj, splits, axis=-1)

    q = rms_norm(q.reshape(b, l, N_Q_HEADS, HEAD_DIM), q_norm_w)
    k = rms_norm(k.reshape(b, l, N_KV_HEADS, HEAD_DIM), k_norm_w)
    v = v.reshape(b, l, N_KV_HEADS, HEAD_DIM)
    attn = banded_window_attention(q, k, v, attn_sink, rel_bias) @ w_attn_br

    xbc = jax.nn.silu(centred_depthwise_conv(xbc, conv_w, conv_b))
    xs, bm, cm = jnp.split(xbc, [SSD_INNER, SSD_INNER + SSD_GROUPS * SSD_STATE], axis=-1)
    xs = xs.reshape(b, l, SSD_HEADS, SSD_HEAD_DIM)
    bm = bm.reshape(b, l, SSD_GROUPS, SSD_STATE)
    cm = cm.reshape(b, l, SSD_GROUPS, SSD_STATE)
    dt = jax.nn.softplus(dt_raw.reshape(b, l, 2, SSD_HEADS).astype(jnp.float32)
                         + dt_bias.astype(jnp.float32))
    a = -jnp.exp(a_log.astype(jnp.float32))
    flip = lambda t: jnp.flip(t, axis=1)
    y_fwd = ssd_chunked_scan(xs, dt[:, :, 0], a[0], bm, cm)
    y_bwd = flip(ssd_chunked_scan(flip(xs), flip(dt[:, :, 1]), a[1], flip(bm), flip(cm)))
    y = y_fwd + y_bwd + xs * d_skip[:, None]
    y = y.reshape(b, l, SSD_INNER) * jax.nn.silu(z)
    y = rms_normalize(y.reshape(b, l, SSD_GROUPS, SSD_INNER // SSD_GROUPS)).reshape(b, l, SSD_INNER) * ssd_norm_w
    ssd = y @ w_ssd_br

    merged = jax.nn.sigmoid(gate_a) * attn + jax.nn.sigmoid(gate_s) * ssd
    return merged @ w_out


def peer_channel_mixer(h, wq, sub_keys, u_tab, v_tab):
    b, l, d = h.shape
    q = (h @ wq).reshape(b, l, PEER_HEADS, 2, PEER_HALF)
    s = jnp.einsum('blhtd,htkd->blhtk', q, sub_keys).astype(jnp.float32)
    top_s, top_i = lax.top_k(s, PEER_TOPK)
    cand_s = (top_s[..., 0, :, None] + top_s[..., 1, None, :]).reshape(b, l, PEER_HEADS, PEER_TOPK * PEER_TOPK)
    cand_i = (top_i[..., 0, :, None] * N_KEYS + top_i[..., 1, None, :]).reshape(b, l, PEER_HEADS, PEER_TOPK * PEER_TOPK)
    best_s, best_pos = lax.top_k(cand_s, PEER_TOPK)
    expert_idx = jnp.take_along_axis(cand_i, best_pos, axis=-1)
    gates = jax.nn.softmax(best_s, axis=-1)
    n_sel = PEER_HEADS * PEER_TOPK
    nt = (b * l) // PEER_TOK_BLOCK
    tok = h.reshape(nt, PEER_TOK_BLOCK, d)
    idx = expert_idx.reshape(nt, PEER_TOK_BLOCK, n_sel)
    g = gates.astype(h.dtype).reshape(nt, PEER_TOK_BLOCK, n_sel)

    def experts(args):
        xt, it, gt = args
        act = jnp.einsum('td,tkd->tk', xt, u_tab[it])
        w = gt * jax.nn.gelu(act)
        return jnp.einsum('tk,tkd->td', w, v_tab[it])

    out = lax.map(experts, (tok, idx, g))
    return out.reshape(b, l, d)


def encoder_trunk(x, c, rel_bias, ada_w, ada_b, norm1_w, norm2_w, w_in, q_norm_w, k_norm_w,
                  attn_sink, conv_w, conv_b, a_log, dt_bias, d_skip, ssd_norm_w,
                  w_attn_br, w_ssd_br, w_out, peer_wq, peer_keys, peer_u, peer_v):
    nbatch = c.shape[0]
    for layer in range(DEPTH):
        mod = (jax.nn.silu(c) @ ada_w[layer] + ada_b[layer]).reshape(nbatch, N_MOD, 1, D_MODEL)
        shift1, scale1, gate1, shift2, scale2, gate2 = [mod[:, i] for i in range(N_MOD)]
        h = rms_norm(x, norm1_w[layer]) * (1.0 + scale1) + shift1
        x = x + gate1 * token_mixer(h, rel_bias, w_in[layer], q_norm_w[layer], k_norm_w[layer],
                                    attn_sink[layer], conv_w[layer], conv_b[layer], a_log[layer],
                                    dt_bias[layer], d_skip[layer], ssd_norm_w[layer],
                                    w_attn_br[layer], w_ssd_br[layer], w_out[layer])
        h = rms_norm(x, norm2_w[layer]) * (1.0 + scale2) + shift2
        x = x + gate2 * peer_channel_mixer(h, peer_wq[layer], peer_keys[layer], peer_u[layer], peer_v[layer])
    return x


def setup_inputs(seed: int = 0) -> dict:
    key = jax.random.key(seed)
    ks = jax.random.split(key, 32)

    def nrm(k, shape, scale):
        return jax.random.normal(k, shape, jnp.float32) * scale

    dt0 = jnp.exp(jax.random.uniform(ks[16], (DEPTH, 2, SSD_HEADS), jnp.float32,
                                     minval=math.log(1e-3), maxval=math.log(1e-1)))
    return {
        'x_prompt': nrm(ks[0], (BATCH, SEQ, D_MODEL), 1.0),
        'x_sample': nrm(ks[1], (DEC_BATCH, DEC_SEQ, D_MODEL), 1.0),
        'c_prompt': nrm(ks[2], (BATCH, D_MODEL), 1.0),
        'c_sample': nrm(ks[3], (DEC_BATCH, D_MODEL), 1.0),
        'rel_bias': nrm(ks[4], (NUM_BUCKETS, N_Q_HEADS), 0.5),
        'ada_w': nrm(ks[5], (DEPTH, D_MODEL, N_MOD * D_MODEL), 0.5 * D_MODEL ** -0.5),
        'ada_b': nrm(ks[6], (DEPTH, N_MOD * D_MODEL), 0.02),
        'norm1_w': 1.0 + nrm(ks[7], (DEPTH, D_MODEL), 0.01),
        'norm2_w': 1.0 + nrm(ks[8], (DEPTH, D_MODEL), 0.01),
        'w_in': nrm(ks[9], (DEPTH, D_MODEL, IN_COLS), D_MODEL ** -0.5),
        'q_norm_w': 1.0 + nrm(ks[10], (DEPTH, HEAD_DIM), 0.01),
        'k_norm_w': 1.0 + nrm(ks[11], (DEPTH, HEAD_DIM), 0.01),
        'attn_sink': nrm(ks[12], (DEPTH, N_Q_HEADS), 0.5),
        'conv_w': nrm(ks[13], (DEPTH, CONV_WIDTH, CONV_CH), CONV_WIDTH ** -0.5),
        'conv_b': nrm(ks[14], (DEPTH, CONV_CH), 0.02),
        'a_log': jnp.log(jax.random.uniform(ks[15], (DEPTH, 2, SSD_HEADS), jnp.float32, minval=1.0, maxval=16.0)),
        'dt_bias': dt0 + jnp.log(-jnp.expm1(-dt0)),
        'd_skip': 1.0 + nrm(ks[17], (DEPTH, SSD_HEADS), 0.1),
        'ssd_norm_w': 1.0 + nrm(ks[18], (DEPTH, SSD_INNER), 0.01),
        'w_attn_br': nrm(ks[19], (DEPTH, ATTN_WIDTH, D_MODEL), ATTN_WIDTH ** -0.5),
        'w_ssd_br': nrm(ks[20], (DEPTH, SSD_INNER, D_MODEL), SSD_INNER ** -0.5),
        'w_out': nrm(ks[21], (DEPTH, D_MODEL, D_MODEL), D_MODEL ** -0.5),
        'peer_wq': nrm(ks[22], (DEPTH, D_MODEL, PEER_HEADS * PEER_KEY_DIM), D_MODEL ** -0.5),
        'peer_keys': nrm(ks[23], (DEPTH, PEER_HEADS, 2, N_KEYS, PEER_HALF), PEER_HALF ** -0.5),
        'peer_u': nrm(ks[24], (DEPTH, N_EXPERTS, D_MODEL), D_MODEL ** -0.5),
        'peer_v': nrm(ks[25], (DEPTH, N_EXPERTS, D_MODEL), 1.0),
    }


def reference(x_prompt, x_sample, c_prompt, c_sample, rel_bias, ada_w, ada_b, norm1_w, norm2_w,
              w_in, q_norm_w, k_norm_w, attn_sink, conv_w, conv_b, a_log, dt_bias, d_skip,
              ssd_norm_w, w_attn_br, w_ssd_br, w_out, peer_wq, peer_keys, peer_u, peer_v):
    y_prompt = encoder_trunk(x_prompt, c_prompt, rel_bias, ada_w, ada_b, norm1_w, norm2_w, w_in,
                             q_norm_w, k_norm_w, attn_sink, conv_w, conv_b, a_log, dt_bias, d_skip,
                             ssd_norm_w, w_attn_br, w_ssd_br, w_out, peer_wq, peer_keys, peer_u, peer_v)
    y_sample = encoder_trunk(x_sample, c_sample, rel_bias, ada_w, ada_b, norm1_w, norm2_w, w_in,
                             q_norm_w, k_norm_w, attn_sink, conv_w, conv_b, a_log, dt_bias, d_skip,
                             ssd_norm_w, w_attn_br, w_ssd_br, w_out, peer_wq, peer_keys, peer_u, peer_v)
    return (y_prompt, y_sample)
```

```python
import functools
import math

import jax
import jax.numpy as jnp
from jax import lax
from jax.experimental import pallas as pl
from jax.experimental.pallas import tpu as pltpu

F32 = jnp.float32
BF16 = jnp.bfloat16
I32 = jnp.int32

D_MODEL = 1024
HEAD_DIM = 64
N_Q_HEADS = 16
N_KV_HEADS = 4
Q_PER_KV = N_Q_HEADS // N_KV_HEADS
ATTN_WIDTH = N_Q_HEADS * HEAD_DIM
KV_WIDTH = N_KV_HEADS * HEAD_DIM
WINDOW = 128
BAND_BLOCK = 128
NUM_BUCKETS = 32
MAX_DISTANCE = 128
NEG_INF = -1e30

SSD_INNER = 2 * D_MODEL
SSD_HEAD_DIM = 64
SSD_HEADS = SSD_INNER // SSD_HEAD_DIM
SSD_GROUPS = 4
SSD_HEADS_PER_GROUP = SSD_HEADS // SSD_GROUPS
SSD_STATE = 128
SSD_CHUNK = 128
SSD_GROUP_WIDTH = SSD_INNER // SSD_GROUPS
CONV_WIDTH = 5
CONV_CH = SSD_INNER + 2 * SSD_GROUPS * SSD_STATE

PEER_HEADS = 8
PEER_KEY_DIM = 256
PEER_HALF = PEER_KEY_DIM // 2
N_KEYS = 128
N_EXPERTS = N_KEYS * N_KEYS
PEER_TOPK = 16
N_SEL = PEER_HEADS * PEER_TOPK

N_MOD = 6
EPS = 1e-6

COL_Z = 0
COL_Q = 2048
COL_GA = 3072
COL_GS = 4096
COL_XBC = 5120
COL_K = 8192
COL_V = 8448
COL_DT = 8704
PROJ_W = 9216

LANES = 128
SUBLANES = 8
VMEM_LIMIT = 56 * 1024 * 1024
ROW_WORDS = D_MODEL // (2 * LANES)
PEER_TOK_GROUP = 8
PEER_BLOCK = 128


def _cparams(sem):
    return pltpu.CompilerParams(dimension_semantics=sem, vmem_limit_bytes=VMEM_LIMIT)


def _split_bf16(v):
    hi = v.astype(BF16)
    lo = (v - hi.astype(F32)).astype(BF16)
    return hi, lo


def _dot2(v, m_bf16):
    hi, lo = _split_bf16(v)
    return (jnp.dot(hi, m_bf16, preferred_element_type=F32)
            + jnp.dot(lo, m_bf16, preferred_element_type=F32))


def _dot2_left(m_bf16, v):
    hi, lo = _split_bf16(v)
    return (jnp.dot(m_bf16, hi, preferred_element_type=F32)
            + jnp.dot(m_bf16, lo, preferred_element_type=F32))


def _mod_kernel(c_ref, w_ref, b_ref, o_ref):
    c = c_ref[...]
    sc = c * jax.nn.sigmoid(c)
    o_ref[...] = jnp.dot(sc, w_ref[...], preferred_element_type=F32,
                         precision=lax.Precision.HIGHEST) + b_ref[...]


def _modulation(c, ada_w, ada_b):
    nb = c.shape[0]
    n = ada_w.shape[1]
    tn = 1024
    return pl.pallas_call(
        _mod_kernel,
        out_shape=jax.ShapeDtypeStruct((nb, n), F32),
        grid=(n // tn,),
        in_specs=[pl.BlockSpec((nb, D_MODEL), lambda j: (0, 0)),
                  pl.BlockSpec((D_MODEL, tn), lambda j: (0, j)),
                  pl.BlockSpec((1, tn), lambda j: (0, j))],
        out_specs=pl.BlockSpec((nb, tn), lambda j: (0, j)),
        compiler_params=_cparams(("arbitrary",)),
        name="adaln_mod",
    )(c, ada_w, ada_b.reshape(1, n))


def _norm_mod(x, nw, sc, sh):
    ms = jnp.mean(x * x, axis=-1, keepdims=True)
    h = x * lax.rsqrt(ms + EPS) * nw
    return h * (1.0 + sc) + sh


def _inproj_kernel(x_ref, sh_ref, sc_ref, nw_ref, w_ref, o_ref, h_scr):
    @pl.when(pl.program_id(2) == 0)
    def _():
        h = _norm_mod(x_ref[0], nw_ref[...], sc_ref[0], sh_ref[0])
        h_scr[...] = h.astype(BF16)

    o_ref[0] = jnp.dot(h_scr[...], w_ref[...], preferred_element_type=F32).astype(o_ref.dtype)


def _in_projection(x, modr, norm_w, w_all):
    b, l, _ = x.shape
    tl = min(l, 1024)
    tn = 512
    return pl.pallas_call(
        _inproj_kernel,
        out_shape=jax.ShapeDtypeStruct((b, l, PROJ_W), BF16),
        grid=(b, l // tl, PROJ_W // tn),
        in_specs=[pl.BlockSpec((1, tl, D_MODEL), lambda bi, i, j: (bi, i, 0)),
                  pl.BlockSpec((1, 1, D_MODEL), lambda bi, i, j: (bi * N_MOD + 0, 0, 0)),
                  pl.BlockSpec((1, 1, D_MODEL), lambda bi, i, j: (bi * N_MOD + 1, 0, 0)),
                  pl.BlockSpec((1, D_MODEL), lambda bi, i, j: (0, 0)),
                  pl.BlockSpec((D_MODEL, tn), lambda bi, i, j: (0, j))],
        out_specs=pl.BlockSpec((1, tl, tn), lambda bi, i, j: (bi, i, j)),
        scratch_shapes=[pltpu.VMEM((tl, D_MODEL), BF16)],
        compiler_params=_cparams(("parallel", "parallel", "arbitrary")),
        name="in_projection",
    )(x, modr, modr, norm_w, w_all)


def _head_norm(t, w):
    return t * lax.rsqrt(jnp.mean(t * t, axis=-1, keepdims=True) + EPS) * w


def _attn_kernel(sink_ref, q_ref, kp_ref, kc_ref, kn_ref, vp_ref, vc_ref, vn_ref,
                 bias_ref, qw_ref, kw_ref, o_ref, *, nb):
    i = pl.program_id(1)
    q = q_ref[0].astype(F32)
    k = jnp.concatenate([kp_ref[0], kc_ref[0], kn_ref[0]], axis=0).astype(F32)
    v = jnp.concatenate([vp_ref[0], vc_ref[0], vn_ref[0]], axis=0)
    col = lax.broadcasted_iota(I32, (1, 3 * BAND_BLOCK), 1)
    invalid = jnp.logical_or(jnp.logical_and(col < BAND_BLOCK, i == 0),
                             jnp.logical_and(col >= 2 * BAND_BLOCK, i == nb - 1))
    scale = HEAD_DIM ** -0.5
    for h in range(N_KV_HEADS):
        kh = _head_norm(k[:, h * HEAD_DIM:(h + 1) * HEAD_DIM], kw_ref[...]).astype(BF16)
        vh = v[:, h * HEAD_DIM:(h + 1) * HEAD_DIM]
        for g in range(Q_PER_KV):
            hd = h * Q_PER_KV + g
            qh = _head_norm(q[:, hd * HEAD_DIM:(hd + 1) * HEAD_DIM], qw_ref[...]).astype(BF16)
            s = lax.dot_general(qh, kh, (((1,), (1,)), ((), ())), preferred_element_type=F32)
            s = s * scale + bias_ref[hd]
            s = jnp.where(invalid, NEG_INF, s)
            sk = sink_ref[hd]
            m = jnp.maximum(jnp.max(s, axis=-1, keepdims=True), sk)
            p = jnp.exp(s - m)
            denom = jnp.sum(p, axis=-1, keepdims=True) + jnp.exp(sk - m)
            probs = (p / denom).astype(BF16)
            o = jnp.dot(probs, vh, preferred_element_type=F32)
            o_ref[0, :, hd * HEAD_DIM:(hd + 1) * HEAD_DIM] = o.astype(o_ref.dtype)


def _attention(proj, bias_tab, sink, qw, kw):
    b, l, _ = proj.shape
    nb = l // BAND_BLOCK
    kcol = COL_K // KV_WIDTH
    vcol = COL_V // KV_WIDTH

    def prev(i):
        return jnp.maximum(i - 1, 0)

    def nxt(i):
        return jnp.minimum(i + 1, nb - 1)

    kv = lambda colb, f: pl.BlockSpec((1, BAND_BLOCK, KV_WIDTH), lambda bi, i: (bi, f(i), colb))
    same = lambda i: i
    return pl.pallas_call(
        functools.partial(_attn_kernel, nb=nb),
        out_shape=jax.ShapeDtypeStruct((b, l, ATTN_WIDTH), BF16),
        grid=(b, nb),
        in_specs=[pl.BlockSpec(memory_space=pltpu.SMEM),
                  pl.BlockSpec((1, BAND_BLOCK, ATTN_WIDTH), lambda bi, i: (bi, i, COL_Q // ATTN_WIDTH)),
                  kv(kcol, prev), kv(kcol, same), kv(kcol, nxt),
                  kv(vcol, prev), kv(vcol, same), kv(vcol, nxt),
                  pl.BlockSpec((N_Q_HEADS, BAND_BLOCK, 3 * BAND_BLOCK), lambda bi, i: (0, 0, 0)),
                  pl.BlockSpec((1, HEAD_DIM), lambda bi, i: (0, 0)),
                  pl.BlockSpec((1, HEAD_DIM), lambda bi, i: (0, 0))],
        out_specs=pl.BlockSpec((1, BAND_BLOCK, ATTN_WIDTH), lambda bi, i: (bi, i, 0)),
        compiler_params=_cparams(("parallel", "parallel")),
        name="window_attention",
    )(sink, proj, proj, proj, proj, proj, proj, proj, bias_tab, qw, kw)


def _t5_bucket(rel):
    half = NUM_BUCKETS // 2
    max_exact = half // 2
    bucket = jnp.where(rel > 0, half, 0)
    n = jnp.abs(rel)
    nf = jnp.maximum(n, 1).astype(F32)
    large = max_exact + (jnp.log(nf / max_exact) / math.log(MAX_DISTANCE / max_exact)
                         * (half - max_exact)).astype(I32)
    large = jnp.minimum(large, half - 1)
    return bucket + jnp.where(n < max_exact, n, large)


def _bias_table(rel_bias):
    rel = jnp.arange(3 * BAND_BLOCK)[None, :] - BAND_BLOCK - jnp.arange(BAND_BLOCK)[:, None]
    in_window = jnp.abs(rel) <= WINDOW
    bias = rel_bias[_t5_bucket(rel)].astype(F32).transpose(2, 0, 1)
    return jnp.where(in_window[None], bias, NEG_INF)


CONV_HALO = 16
CONV_TILE = 512


def _conv_kernel(prev_ref, cur_ref, next_ref, w_ref, b_ref, o_ref, *, nt):
    i = pl.program_id(1)
    tl = cur_ref.shape[1]
    cur = cur_ref[0].astype(F32)
    prev = prev_ref[0].astype(F32) * jnp.where(i > 0, 1.0, 0.0)
    nxt = next_ref[0].astype(F32) * jnp.where(i < nt - 1, 1.0, 0.0)
    ext = jnp.concatenate([prev, cur, nxt], axis=0)
    half = CONV_WIDTH // 2
    acc = jnp.broadcast_to(b_ref[...], cur.shape)
    for t in range(CONV_WIDTH):
        off = CONV_HALO - half + t
        acc = acc + ext[off:off + tl] * w_ref[t:t + 1, :]
    o_ref[0] = (acc * jax.nn.sigmoid(acc)).astype(o_ref.dtype)


def _conv_silu(proj, conv_w, conv_b):
    b, l, _ = proj.shape
    tl = min(l, CONV_TILE)
    nt = l // tl
    cw = CONV_TILE
    nch = CONV_CH // cw
    c0 = COL_XBC // cw
    hb = tl // CONV_HALO
    nh = l // CONV_HALO
    return pl.pallas_call(
        functools.partial(_conv_kernel, nt=nt),
        out_shape=jax.ShapeDtypeStruct((b, l, CONV_CH), BF16),
        grid=(b, nt, nch),
        in_specs=[pl.BlockSpec((1, CONV_HALO, cw), lambda bi, i, c: (bi, jnp.maximum(i * hb - 1, 0), c0 + c)),
                  pl.BlockSpec((1, tl, cw), lambda bi, i, c: (bi, i, c0 + c)),
                  pl.BlockSpec((1, CONV_HALO, cw), lambda bi, i, c: (bi, jnp.minimum((i + 1) * hb, nh - 1), c0 + c)),
                  pl.BlockSpec((CONV_WIDTH, cw), lambda bi, i, c: (0, c)),
                  pl.BlockSpec((1, cw), lambda bi, i, c: (0, c))],
        out_specs=pl.BlockSpec((1, tl, cw), lambda bi, i, c: (bi, i, c)),
        compiler_params=_cparams(("parallel", "parallel", "parallel")),
        name="conv_silu",
    )(proj, proj, proj, conv_w, conv_b.reshape(1, CONV_CH))


def _ssd_kernel(xs_ref, b_ref, c_ref, dt_ref, dtb_ref, alog_ref, e_ref, y_ref, st_ref, *, rev):
    @pl.when(pl.program_id(1) == 0)
    def _():
        st_ref[...] = jnp.zeros_like(st_ref)

    q = SSD_CHUNK
    off = SSD_HEADS if rev else 0
    z = dt_ref[0].astype(F32) + dtb_ref[...]
    dt = jnp.maximum(z, 0.0) + jnp.log1p(jnp.exp(-jnp.abs(z)))
    da = dt * (-jnp.exp(alog_ref[...]))
    ri = lax.broadcasted_iota(I32, (q, q), 0)
    ci = lax.broadcasted_iota(I32, (q, q), 1)
    if rev:
        mask = ci >= ri
        mask_t = ci <= ri
    else:
        mask = ci <= ri
        mask_t = ci >= ri
    tri = jnp.where(mask, 1.0, 0.0).astype(BF16)
    tri_t = jnp.where(mask_t, 1.0, 0.0).astype(BF16)
    cum = _dot2_left(tri, da)
    cum_t = _dot2(da.T, tri_t)
    tot = cum[0:1, :] if rev else cum[q - 1:q, :]
    e = e_ref[...]
    dt_full = _dot2(dt, e)
    cum_full = _dot2(cum, e)
    tot_full = _dot2(tot, e)
    xdt_f = xs_ref[0].astype(F32) * dt_full
    xdt = xdt_f.astype(BF16)
    xw = (xdt_f * jnp.exp(tot_full - cum_full)).astype(BF16)
    expcum = jnp.exp(cum_full)
    chunk_decay = jnp.exp(tot_full)
    gw = SSD_GROUP_WIDTH
    for g in range(SSD_GROUPS):
        bg = b_ref[0][:, g * SSD_STATE:(g + 1) * SSD_STATE]
        cg = c_ref[0][:, g * SSD_STATE:(g + 1) * SSD_STATE]
        bgt = bg.astype(F32).T.astype(BF16)
        cb = jnp.dot(cg, bgt, preferred_element_type=F32)
        ys = []
        for r in range(SSD_HEADS_PER_GROUP):
            h = g * SSD_HEADS_PER_GROUP + r
            seg = cum[:, off + h:off + h + 1] - cum_t[off + h:off + h + 1, :]
            dec = jnp.exp(jnp.where(mask, seg, NEG_INF))
            m = (cb * dec).astype(BF16)
            ys.append(jnp.dot(m, xdt[:, h * SSD_HEAD_DIM:(h + 1) * SSD_HEAD_DIM],
                              preferred_element_type=F32))
        y_diag = jnp.concatenate(ys, axis=1)
        st = st_ref[g]
        y_off = jnp.dot(cg, st.astype(BF16), preferred_element_type=F32) * expcum[:, g * gw:(g + 1) * gw]
        new = jnp.dot(bgt, xw[:, g * gw:(g + 1) * gw], preferred_element_type=F32)
        st_ref[g] = st * chunk_decay[:, g * gw:(g + 1) * gw] + new
        y_ref[0, :, g * gw:(g + 1) * gw] = (y_diag + y_off).astype(y_ref.dtype)


def _ssd_scan(xbc, proj, dtb_row, alog_row, expand, rev):
    b, l, _ = xbc.shape
    nc = l // SSD_CHUNK
    gs = SSD_GROUPS * SSD_STATE
    cidx = (lambda c: nc - 1 - c) if rev else (lambda c: c)
    return pl.pallas_call(
        functools.partial(_ssd_kernel, rev=rev),
        out_shape=jax.ShapeDtypeStruct((b, l, SSD_INNER), BF16),
        grid=(b, nc),
        in_specs=[pl.BlockSpec((1, SSD_CHUNK, SSD_INNER), lambda bi, c: (bi, cidx(c), 0)),
                  pl.BlockSpec((1, SSD_CHUNK, gs), lambda bi, c: (bi, cidx(c), SSD_INNER // gs)),
                  pl.BlockSpec((1, SSD_CHUNK, gs), lambda bi, c: (bi, cidx(c), SSD_INNER // gs + 1)),
                  pl.BlockSpec((1, SSD_CHUNK, LANES), lambda bi, c: (bi, cidx(c), COL_DT // LANES)),
                  pl.BlockSpec((1, LANES), lambda bi, c: (0, 0)),
                  pl.BlockSpec((1, LANES), lambda bi, c: (0, 0)),
                  pl.BlockSpec((LANES, SSD_INNER), lambda bi, c: (0, 0))],
        out_specs=pl.BlockSpec((1, SSD_CHUNK, SSD_INNER), lambda bi, c: (bi, cidx(c), 0)),
        scratch_shapes=[pltpu.VMEM((SSD_GROUPS, SSD_STATE, SSD_GROUP_WIDTH), F32)],
        compiler_params=_cparams(("parallel", "arbitrary")),
        name="ssd_scan_bwd" if rev else "ssd_scan_fwd",
    )(xbc, xbc, xbc, proj, dtb_row, alog_row, expand)


def _merge_kernel(x_ref, yf_ref, yb_ref, xs_ref, z_ref, ao_ref, ga_ref, gs_ref, g1_ref,
                  dsk_ref, snw_ref, wssd_ref, wattn_ref, wout_ref, o_ref):
    y = yf_ref[0].astype(F32) + yb_ref[0].astype(F32) + xs_ref[0].astype(F32) * dsk_ref[...]
    z = z_ref[0].astype(F32)
    y = y * (z * jax.nn.sigmoid(z))
    parts = []
    for g in range(SSD_GROUPS):
        yg = y[:, g * SSD_GROUP_WIDTH:(g + 1) * SSD_GROUP_WIDTH]
        parts.append(yg * lax.rsqrt(jnp.mean(yg * yg, axis=-1, keepdims=True) + EPS))
    y = jnp.concatenate(parts, axis=1) * snw_ref[...]
    ssd = jnp.dot(y.astype(BF16), wssd_ref[...], preferred_element_type=F32)
    attn = jnp.dot(ao_ref[0], wattn_ref[...], preferred_element_type=F32)
    merged = (jax.nn.sigmoid(ga_ref[0].astype(F32)) * attn
              + jax.nn.sigmoid(gs_ref[0].astype(F32)) * ssd)
    out = jnp.dot(merged.astype(BF16), wout_ref[...], preferred_element_type=F32)
    o_ref[0] = x_ref[0] + g1_ref[0] * out


def _merge(x, yf, yb, xbc, proj, attn_o, modr, dsk_full, snw, w_ssd, w_attn, w_out):
    b, l, _ = x.shape
    tl = min(l, 256)
    tok = lambda w, colb: pl.BlockSpec((1, tl, w), lambda bi, i: (bi, i, colb))
    full = lambda r, c: pl.BlockSpec((r, c), lambda bi, i: (0, 0))
    return pl.pallas_call(
        _merge_kernel,
        out_shape=jax.ShapeDtypeStruct((b, l, D_MODEL), F32),
        grid=(b, l // tl),
        in_specs=[tok(D_MODEL, 0), tok(SSD_INNER, 0), tok(SSD_INNER, 0), tok(SSD_INNER, 0),
                  tok(SSD_INNER, COL_Z // SSD_INNER), tok(ATTN_WIDTH, 0),
                  tok(D_MODEL, COL_GA // D_MODEL), tok(D_MODEL, COL_GS // D_MODEL),
                  pl.BlockSpec((1, 1, D_MODEL), lambda bi, i: (bi * N_MOD + 2, 0, 0)),
                  full(1, SSD_INNER), full(1, SSD_INNER),
                  full(SSD_INNER, D_MODEL), full(ATTN_WIDTH, D_MODEL), full(D_MODEL, D_MODEL)],
        out_specs=tok(D_MODEL, 0),
        compiler_params=_cparams(("parallel", "parallel")),
        name="merge_out_proj",
    )(x, yf, yb, xbc, proj, attn_o, proj, proj, modr, dsk_full, snw, w_ssd, w_attn, w_out)


def _perm_src(j):
    return (j % 2) * (D_MODEL // 2) + (j // 2) * LANES


def _topk_rows(s, k, nrows):
    rowi = lax.broadcasted_iota(I32, s.shape, 0)
    vals, ids = [], []
    for _ in range(k):
        m = jnp.max(s, axis=0, keepdims=True)
        am = jnp.min(jnp.where(s == m, rowi, nrows), axis=0, keepdims=True)
        vals.append(m)
        ids.append(am)
        s = jnp.where(rowi == am, -jnp.inf, s)
    return vals, ids


def _route_kernel(x_ref, sh_ref, sc_ref, nw_ref, wq_ref, keys_ref, hp_ref, idx_ref, gate_ref):
    h = _norm_mod(x_ref[0], nw_ref[...], sc_ref[0], sh_ref[0])
    for j in range(D_MODEL // LANES):
        hp_ref[0, :, j * LANES:(j + 1) * LANES] = h[:, _perm_src(j):_perm_src(j) + LANES]
    qall = jnp.dot(h.astype(BF16), wq_ref[...], preferred_element_type=F32)
    tops, topi = [], []
    for c in range(2 * PEER_HEADS):
        qc = qall[:, c * PEER_HALF:(c + 1) * PEER_HALF].astype(BF16)
        s = lax.dot_general(keys_ref[c], qc, (((1,), (1,)), ((), ())), preferred_element_type=F32)
        vals, ids = _topk_rows(s, PEER_TOPK, N_KEYS)
        tops.append(vals)
        topi.append(ids)
    idx_rows, gate_rows = [], []
    ncand = PEER_TOPK * PEER_TOPK
    for hd in range(PEER_HEADS):
        s1 = jnp.concatenate(tops[2 * hd + 1], axis=0)
        i1 = jnp.concatenate(topi[2 * hd + 1], axis=0)
        cand = jnp.concatenate([tops[2 * hd][a] + s1 for a in range(PEER_TOPK)], axis=0)
        cidx = jnp.concatenate([topi[2 * hd][a] * N_KEYS + i1 for a in range(PEER_TOPK)], axis=0)
        rowi = lax.broadcasted_iota(I32, cand.shape, 0)
        best = []
        for _ in range(PEER_TOPK):
            m = jnp.max(cand, axis=0, keepdims=True)
            pos = jnp.min(jnp.where(cand == m, rowi, ncand), axis=0, keepdims=True)
            sel = rowi == pos
            idx_rows.append(jnp.sum(jnp.where(sel, cidx, 0), axis=0, keepdims=True))
            best.append(m)
            cand = jnp.where(sel, -jnp.inf, cand)
        bs = jnp.concatenate(best, axis=0)
        p = jnp.exp(bs - best[0])
        gate_rows.append(p / jnp.sum(p, axis=0, keepdims=True))
    idx_t = jnp.concatenate(idx_rows, axis=0)
    gate_t = jnp.concatenate(gate_rows, axis=0)
    idx_ref[0] = (idx_t * ROW_WORDS).astype(F32).T.astype(I32)
    gate_ref[0] = gate_t.T


def _route(x1, modr, norm_w, wq, keys):
    b, l, _ = x1.shape
    tl = min(l, 256)
    tok = lambda w: pl.BlockSpec((1, tl, w), lambda bi, i: (bi, i, 0))
    return pl.pallas_call(
        _route_kernel,
        out_shape=(jax.ShapeDtypeStruct((b, l, D_MODEL), F32),
                   jax.ShapeDtypeStruct((b, l, N_SEL), I32),
                   jax.ShapeDtypeStruct((b, l, N_SEL), F32)),
        grid=(b, l // tl),
        in_specs=[tok(D_MODEL),
                  pl.BlockSpec((1, 1, D_MODEL), lambda bi, i: (bi * N_MOD + 3, 0, 0)),
                  pl.BlockSpec((1, 1, D_MODEL), lambda bi, i: (bi * N_MOD + 4, 0, 0)),
                  pl.BlockSpec((1, D_MODEL), lambda bi, i: (0, 0)),
                  pl.BlockSpec((D_MODEL, PEER_HEADS * PEER_KEY_DIM), lambda bi, i: (0, 0)),
                  pl.BlockSpec((2 * PEER_HEADS, N_KEYS, PEER_HALF), lambda bi, i: (0, 0, 0))],
        out_specs=(tok(D_MODEL), tok(N_SEL), tok(N_SEL)),
        compiler_params=_cparams(("parallel", "parallel")),
        name="peer_route",
    )(x1, modr, modr, norm_w, wq, keys)


def _pack_table(tab):
    n = tab.shape[0]
    bits = lax.bitcast_convert_type(tab.astype(BF16), jnp.uint16).astype(jnp.uint32)
    lo = bits[:, :D_MODEL // 2]
    hi = bits[:, D_MODEL // 2:]
    words = lax.bitcast_convert_type(lo | (hi << 16), I32)
    return words.reshape(n * ROW_WORDS, LANES)


def _gather_rows(idx_ref, base, tab_ref, tile_ref):
    for k in range(N_SEL):
        r = pl.multiple_of(idx_ref[base + k], ROW_WORDS)
        tile_ref[k * ROW_WORDS:(k + 1) * ROW_WORDS, :] = tab_ref[pl.ds(r, ROW_WORDS), :]


def _peer_act_kernel(idx_ref, x_ref, tab_ref, dmask_ref, gsum_ref, act_ref, tiles):
    def group(gi, carry):
        t0 = gi * PEER_TOK_GROUP
        vs = []
        for tt in range(PEER_TOK_GROUP):
            _gather_rows(idx_ref, (t0 + tt) * N_SEL, tab_ref, tiles.at[tt])
            rows = pltpu.bitcast(tiles[tt], BF16)
            x8 = x_ref[pl.ds(pl.multiple_of((t0 + tt) * SUBLANES, SUBLANES), SUBLANES), :].astype(BF16)
            r = lax.dot_general(x8, rows, (((1,), (1,)), ((), ())), preferred_element_type=F32)
            vs.append(jnp.sum(r * dmask_ref[...], axis=0, keepdims=True))
        v = jnp.concatenate(vs, axis=0)
        act_ref[pl.ds(pl.multiple_of(t0, PEER_TOK_GROUP), PEER_TOK_GROUP), :] = _dot2(v, gsum_ref[...])
        return carry

    lax.fori_loop(0, PEER_BLOCK // PEER_TOK_GROUP, group, 0)


def _gelu_tanh(x):
    return 0.5 * x * (1.0 + jnp.tanh(math.sqrt(2.0 / math.pi) * (x + 0.044715 * (x * x * x))))


def _peer_out_kernel(idx_ref, act_ref, gate_ref, tab_ref, dmask_ref, gexp_ref, out_ref, tiles):
    def group(gi, carry):
        t0 = pl.multiple_of(gi * PEER_TOK_GROUP, PEER_TOK_GROUP)
        a = act_ref[pl.ds(t0, PEER_TOK_GROUP), :]
        w = gate_ref[pl.ds(t0, PEER_TOK_GROUP), :] * _gelu_tanh(a)
        wexp = jnp.dot(w.astype(BF16), gexp_ref[...], preferred_element_type=F32)
        for tt in range(PEER_TOK_GROUP):
            _gather_rows(idx_ref, (t0 + tt) * N_SEL, tab_ref, tiles.at[tt])
            rows = pltpu.bitcast(tiles[tt], BF16)
            w8 = (jnp.broadcast_to(wexp[tt:tt + 1, :], dmask_ref.shape) * dmask_ref[...]).astype(BF16)
            o8 = jnp.dot(w8, rows, preferred_element_type=F32)
            out_ref[pl.ds(pl.multiple_of((t0 + tt) * SUBLANES, SUBLANES), SUBLANES), :] = o8
        return carry

    lax.fori_loop(0, PEER_BLOCK // PEER_TOK_GROUP, group, 0)


def _peer_consts():
    ncol = N_SEL * SUBLANES
    c = jnp.arange(ncol)
    dmask = (c[None, :] % SUBLANES == jnp.arange(SUBLANES)[:, None]).astype(F32)
    gsum = (c[:, None] // SUBLANES == jnp.arange(N_SEL)[None, :]).astype(BF16)
    return dmask, gsum, gsum.T


def _peer_specs():
    ncol = N_SEL * SUBLANES
    idx_spec = pl.BlockSpec((PEER_BLOCK * N_SEL,), lambda i: (i,), memory_space=pltpu.SMEM)
    tab_spec = pl.BlockSpec(memory_space=pltpu.VMEM)
    dmask_spec = pl.BlockSpec((SUBLANES, ncol), lambda i: (0, 0))
    tok_spec = pl.BlockSpec((PEER_BLOCK, N_SEL), lambda i: (i, 0))
    row_spec = pl.BlockSpec((PEER_BLOCK * SUBLANES, LANES), lambda i: (i, 0))
    tiles = pltpu.VMEM((PEER_TOK_GROUP, N_SEL * ROW_WORDS, LANES), I32)
    return idx_spec, tab_spec, dmask_spec, tok_spec, row_spec, tiles


def _peer_act(idx_flat, x8, tab_u, dmask, gsum):
    t = x8.shape[0] // SUBLANES
    idx_spec, tab_spec, dmask_spec, tok_spec, row_spec, tiles = _peer_specs()
    return pl.pallas_call(
        _peer_act_kernel,
        out_shape=jax.ShapeDtypeStruct((t, N_SEL), F32),
        grid=(t // PEER_BLOCK,),
        in_specs=[idx_spec, row_spec, tab_spec, dmask_spec,
                  pl.BlockSpec(gsum.shape, lambda i: (0, 0))],
        out_specs=tok_spec,
        scratch_shapes=[tiles],
        compiler_params=_cparams(("parallel",)),
        name="peer_act",
    )(idx_flat, x8, tab_u, dmask, gsum)


def _peer_out(idx_flat, act, gates, tab_v, dmask, gexp):
    t = act.shape[0]
    idx_spec, tab_spec, dmask_spec, tok_spec, row_spec, tiles = _peer_specs()
    return pl.pallas_call(
        _peer_out_kernel,
        out_shape=jax.ShapeDtypeStruct((t * SUBLANES, LANES), F32),
        grid=(t // PEER_BLOCK,),
        in_specs=[idx_spec, tok_spec, tok_spec, tab_spec, dmask_spec,
                  pl.BlockSpec(gexp.shape, lambda i: (0, 0))],
        out_specs=row_spec,
        scratch_shapes=[tiles],
        compiler_params=_cparams(("parallel",)),
        name="peer_out",
    )(idx_flat, act, gates, tab_v, dmask, gexp)


def _final_kernel(x_ref, o_ref, g2_ref, y_ref):
    for j in range(D_MODEL // LANES):
        s = _perm_src(j)
        y_ref[0, :, s:s + LANES] = (x_ref[0, :, s:s + LANES]
                                    + g2_ref[0, :, s:s + LANES] * o_ref[0, :, j * LANES:(j + 1) * LANES])


def _final_residual(x1, outp, modr):
    b, l, _ = x1.shape
    tl = min(l, 512)
    tok = pl.BlockSpec((1, tl, D_MODEL), lambda bi, i: (bi, i, 0))
    return pl.pallas_call(
        _final_kernel,
        out_shape=jax.ShapeDtypeStruct((b, l, D_MODEL), F32),
        grid=(b, l // tl),
        in_specs=[tok, tok, pl.BlockSpec((1, 1, D_MODEL), lambda bi, i: (bi * N_MOD + 5, 0, 0))],
        out_specs=tok,
        compiler_params=_cparams(("parallel", "parallel")),
        name="peer_residual",
    )(x1, outp, modr)


def _prepare(rel_bias, ada_w, ada_b, norm1_w, norm2_w, w_in, q_norm_w, k_norm_w, attn_sink, conv_w,
             conv_b, a_log, dt_bias, d_skip, ssd_norm_w, w_attn_br, w_ssd_br, w_out, peer_wq,
             peer_keys, peer_u, peer_v):
    lyr = 0
    w = w_in[lyr]
    o = 0
    parts = {}
    for name, width in (("q", ATTN_WIDTH), ("k", KV_WIDTH), ("v", KV_WIDTH), ("z", SSD_INNER),
                        ("xbc", CONV_CH), ("dt", 2 * SSD_HEADS), ("ga", D_MODEL), ("gs", D_MODEL)):
        parts[name] = w[:, o:o + width]
        o += width
    pad = jnp.zeros((D_MODEL, PROJ_W - (COL_DT + 2 * SSD_HEADS)), w.dtype)
    w_all = jnp.concatenate([parts["z"], parts["q"], parts["ga"], parts["gs"], parts["xbc"],
                             parts["k"], parts["v"], parts["dt"], pad], axis=1).astype(BF16)
    lane_pad = LANES - 2 * SSD_HEADS
    expand = (jnp.arange(SSD_INNER)[None, :] // SSD_HEAD_DIM == jnp.arange(LANES)[:, None])
    dmask, gsum, gexp = _peer_consts()
    return dict(
        ada_w=ada_w[lyr], ada_b=ada_b[lyr],
        norm1_w=norm1_w[lyr].reshape(1, D_MODEL), norm2_w=norm2_w[lyr].reshape(1, D_MODEL),
        w_all=w_all,
        bias_tab=_bias_table(rel_bias), sink=attn_sink[lyr].astype(F32),
        qw=q_norm_w[lyr].reshape(1, HEAD_DIM), kw=k_norm_w[lyr].reshape(1, HEAD_DIM),
        conv_w=conv_w[lyr], conv_b=conv_b[lyr],
        dtb_row=jnp.pad(dt_bias[lyr].astype(F32).reshape(1, -1), ((0, 0), (0, lane_pad))),
        alog_row=jnp.pad(a_log[lyr].astype(F32).reshape(1, -1), ((0, 0), (0, lane_pad))),
        expand_fwd=expand.astype(BF16),
        expand_bwd=jnp.roll(expand, SSD_HEADS, axis=0).astype(BF16),
        dsk_full=jnp.repeat(d_skip[lyr], SSD_HEAD_DIM).reshape(1, SSD_INNER),
        snw=ssd_norm_w[lyr].reshape(1, SSD_INNER),
        w_ssd=w_ssd_br[lyr].astype(BF16), w_attn=w_attn_br[lyr].astype(BF16), w_out=w_out[lyr].astype(BF16),
        wq=peer_wq[lyr].astype(BF16),
        keys=peer_keys[lyr].reshape(2 * PEER_HEADS, N_KEYS, PEER_HALF).astype(BF16),
        tab_u=_pack_table(peer_u[lyr]), tab_v=_pack_table(peer_v[lyr]),
        dmask=dmask, gsum=gsum, gexp=gexp,
    )


def _token_mixer_stage(x, modr, p):
    proj = _in_projection(x, modr, p["norm1_w"], p["w_all"])
    attn_o = _attention(proj, p["bias_tab"], p["sink"], p["qw"], p["kw"])
    xbc = _conv_silu(proj, p["conv_w"], p["conv_b"])
    yf = _ssd_scan(xbc, proj, p["dtb_row"], p["alog_row"], p["expand_fwd"], rev=False)
    yb = _ssd_scan(xbc, proj, p["dtb_row"], p["alog_row"], p["expand_bwd"], rev=True)
    return _merge(x, yf, yb, xbc, proj, attn_o, modr, p["dsk_full"], p["snw"],
                  p["w_ssd"], p["w_attn"], p["w_out"])


def _peer_stage(x1, modr, p):
    b, l, _ = x1.shape
    t = b * l
    hp, idx, gates = _route(x1, modr, p["norm2_w"], p["wq"], p["keys"])
    idx_flat = idx.reshape(t * N_SEL)
    x8 = hp.reshape(t * SUBLANES, LANES)
    act = _peer_act(idx_flat, x8, p["tab_u"], p["dmask"], p["gsum"])
    outp = _peer_out(idx_flat, act, gates.reshape(t, N_SEL), p["tab_v"], p["dmask"], p["gexp"])
    return _final_residual(x1, outp.reshape(b, l, D_MODEL), modr)


def _encoder(x, c, p):
    nb = c.shape[0]
    modr = _modulation(c, p["ada_w"], p["ada_b"]).reshape(nb * N_MOD, 1, D_MODEL)
    x1 = _token_mixer_stage(x, modr, p)
    return _peer_stage(x1, modr, p)


def kernel(x_prompt, x_sample, c_prompt, c_sample, rel_bias, ada_w, ada_b, norm1_w, norm2_w, w_in,
           q_norm_w, k_norm_w, attn_sink, conv_w, conv_b, a_log, dt_bias, d_skip, ssd_norm_w,
           w_attn_br, w_ssd_br, w_out, peer_wq, peer_keys, peer_u, peer_v):
    p = _prepare(rel_bias, ada_w, ada_b, norm1_w, norm2_w, w_in, q_norm_w, k_norm_w, attn_sink,
                 conv_w, conv_b, a_log, dt_bias, d_skip, ssd_norm_w, w_attn_br, w_ssd_br, w_out,
                 peer_wq, peer_keys, peer_u, peer_v)
    return (_encoder(x_prompt, c_prompt, p), _encoder(x_sample, c_sample, p))
```

```python
import functools
import math

import jax
import jax.numpy as jnp
from jax import lax
from jax.experimental import pallas as pl
from jax.experimental.pallas import tpu as pltpu

F32 = jnp.float32
BF16 = jnp.bfloat16
I32 = jnp.int32

D_MODEL = 1024
HEAD_DIM = 64
N_Q_HEADS = 16
N_KV_HEADS = 4
Q_PER_KV = N_Q_HEADS // N_KV_HEADS
ATTN_WIDTH = N_Q_HEADS * HEAD_DIM
KV_WIDTH = N_KV_HEADS * HEAD_DIM
WINDOW = 128
BAND_BLOCK = 128
NUM_BUCKETS = 32
MAX_DISTANCE = 128
NEG_INF = -1e30

SSD_INNER = 2 * D_MODEL
SSD_HEAD_DIM = 64
SSD_HEADS = SSD_INNER // SSD_HEAD_DIM
SSD_GROUPS = 4
SSD_HEADS_PER_GROUP = SSD_HEADS // SSD_GROUPS
SSD_STATE = 128
SSD_CHUNK = 128
SSD_GROUP_WIDTH = SSD_INNER // SSD_GROUPS
CONV_WIDTH = 5
CONV_CH = SSD_INNER + 2 * SSD_GROUPS * SSD_STATE

PEER_HEADS = 8
PEER_KEY_DIM = 256
PEER_HALF = PEER_KEY_DIM // 2
N_KEYS = 128
N_EXPERTS = N_KEYS * N_KEYS
PEER_TOPK = 16
N_SEL = PEER_HEADS * PEER_TOPK

N_MOD = 6
EPS = 1e-6

COL_Z = 0
COL_Q = 2048
COL_GA = 3072
COL_GS = 4096
COL_XBC = 5120
COL_K = 8192
COL_V = 8448
COL_DT = 8704
PROJ_W = 9216

LANES = 128
SUBLANES = 8
VMEM_LIMIT = 56 * 1024 * 1024
ROW_WORDS = D_MODEL // (2 * LANES)
PEER_TOK_GROUP = 8
PEER_BLOCK = 128


def _cparams(sem):
    return pltpu.CompilerParams(dimension_semantics=sem, vmem_limit_bytes=VMEM_LIMIT)


def _split_bf16(v):
    hi = v.astype(BF16)
    lo = (v - hi.astype(F32)).astype(BF16)
    return hi, lo


def _dot2(v, m_bf16):
    hi, lo = _split_bf16(v)
    return (jnp.dot(hi, m_bf16, preferred_element_type=F32)
            + jnp.dot(lo, m_bf16, preferred_element_type=F32))


def _dot2_left(m_bf16, v):
    hi, lo = _split_bf16(v)
    return (jnp.dot(m_bf16, hi, preferred_element_type=F32)
            + jnp.dot(m_bf16, lo, preferred_element_type=F32))


def _mod_kernel(c_ref, w_ref, b_ref, o_ref):
    c = c_ref[...]
    sc = c * jax.nn.sigmoid(c)
    o_ref[...] = jnp.dot(sc, w_ref[...], preferred_element_type=F32,
                         precision=lax.Precision.HIGHEST) + b_ref[...]


def _modulation(c, ada_w, ada_b):
    nb = c.shape[0]
    n = ada_w.shape[1]
    tn = 1024
    return pl.pallas_call(
        _mod_kernel,
        out_shape=jax.ShapeDtypeStruct((nb, n), F32),
        grid=(n // tn,),
        in_specs=[pl.BlockSpec((nb, D_MODEL), lambda j: (0, 0)),
                  pl.BlockSpec((D_MODEL, tn), lambda j: (0, j)),
                  pl.BlockSpec((1, tn), lambda j: (0, j))],
        out_specs=pl.BlockSpec((nb, tn), lambda j: (0, j)),
        compiler_params=_cparams(("arbitrary",)),
        name="adaln_mod",
    )(c, ada_w, ada_b.reshape(1, n))


def _norm_mod(x, nw, sc, sh):
    ms = jnp.mean(x * x, axis=-1, keepdims=True)
    h = x * lax.rsqrt(ms + EPS) * nw
    return h * (1.0 + sc) + sh


def _inproj_kernel(x_ref, sh_ref, sc_ref, nw_ref, w_ref, o_ref, h_scr):
    @pl.when(pl.program_id(2) == 0)
    def _():
        h = _norm_mod(x_ref[0], nw_ref[...], sc_ref[0], sh_ref[0])
        h_scr[...] = h.astype(BF16)

    o_ref[0] = jnp.dot(h_scr[...], w_ref[...], preferred_element_type=F32).astype(o_ref.dtype)


def _in_projection(x, modr, norm_w, w_all):
    b, l, _ = x.shape
    tl = min(l, 1024)
    tn = 512
    return pl.pallas_call(
        _inproj_kernel,
        out_shape=jax.ShapeDtypeStruct((b, l, PROJ_W), BF16),
        grid=(b, l // tl, PROJ_W // tn),
        in_specs=[pl.BlockSpec((1, tl, D_MODEL), lambda bi, i, j: (bi, i, 0)),
                  pl.BlockSpec((1, 1, D_MODEL), lambda bi, i, j: (bi * N_MOD + 0, 0, 0)),
                  pl.BlockSpec((1, 1, D_MODEL), lambda bi, i, j: (bi * N_MOD + 1, 0, 0)),
                  pl.BlockSpec((1, D_MODEL), lambda bi, i, j: (0, 0)),
                  pl.BlockSpec((D_MODEL, tn), lambda bi, i, j: (0, j))],
        out_specs=pl.BlockSpec((1, tl, tn), lambda bi, i, j: (bi, i, j)),
        scratch_shapes=[pltpu.VMEM((tl, D_MODEL), BF16)],
        compiler_params=_cparams(("parallel", "parallel", "arbitrary")),
        name="in_projection",
    )(x, modr, modr, norm_w, w_all)


def _head_sums(n_heads):
    c = jnp.arange(n_heads * HEAD_DIM)
    hsum = (c[:, None] // HEAD_DIM == jnp.arange(LANES)[None, :]).astype(BF16)
    return hsum, hsum.T


def _qk_norm(t, hsum, hexp, w_full):
    ssq = _dot2(t * t, hsum)
    inv = lax.rsqrt(ssq * (1.0 / HEAD_DIM) + EPS)
    return t * _dot2(inv, hexp) * w_full


def _attn_kernel(sink_ref, q_ref, kp_ref, kc_ref, kn_ref, vp_ref, vc_ref, vn_ref, bias_ref,
                 qw_ref, kw_ref, qsum_ref, qexp_ref, ksum_ref, kexp_ref, o_ref):
    q = _qk_norm(q_ref[0].astype(F32), qsum_ref[...], qexp_ref[...], qw_ref[...]).astype(BF16)
    k = jnp.concatenate([kp_ref[0], kc_ref[0], kn_ref[0]], axis=0).astype(F32)
    k = _qk_norm(k, ksum_ref[...], kexp_ref[...], kw_ref[...])
    v = jnp.concatenate([vp_ref[0], vc_ref[0], vn_ref[0]], axis=0).astype(F32)
    low = lax.broadcasted_iota(I32, (1, LANES), 1) < HEAD_DIM
    nt = (((1,), (1,)), ((), ()))
    for c in range(N_KV_HEADS // 2):
        kslab = k[:, c * LANES:(c + 1) * LANES]
        vslab = v[:, c * LANES:(c + 1) * LANES]
        kroll = pltpu.roll(kslab, HEAD_DIM, axis=1)
        vroll = pltpu.roll(vslab, HEAD_DIM, axis=1)
        for e in range(2):
            h = 2 * c + e
            k_lo = jnp.where(low, kroll if e else kslab, 0.0).astype(BF16)
            k_hi = jnp.where(low, 0.0, kslab if e else kroll).astype(BF16)
            v_lo = jnp.where(low, vroll if e else vslab, 0.0).astype(BF16)
            v_hi = jnp.where(low, 0.0, vslab if e else vroll).astype(BF16)
            for pair in range(Q_PER_KV // 2):
                slab = h * (Q_PER_KV // 2) + pair
                qs = q[:, slab * LANES:(slab + 1) * LANES]
                acc = None
                for par in range(2):
                    hd = 2 * slab + par
                    s = lax.dot_general(qs, k_hi if par else k_lo, nt, preferred_element_type=F32)
                    s = s + bias_ref[0, hd]
                    sk = sink_ref[hd]
                    m = jnp.maximum(jnp.max(s, axis=-1, keepdims=True), sk)
                    p = jnp.exp(s - m)
                    denom = jnp.sum(p, axis=-1, keepdims=True) + jnp.exp(sk - m)
                    pv = jnp.dot(p.astype(BF16), v_hi if par else v_lo, preferred_element_type=F32) / denom
                    acc = pv if acc is None else acc + pv
                o_ref[0, :, slab * LANES:(slab + 1) * LANES] = acc.astype(o_ref.dtype)


def _attention(proj, bias_tab, sink, qw_full, kw_full):
    b, l, _ = proj.shape
    nb = l // BAND_BLOCK
    assert nb >= 2
    kcol = COL_K // KV_WIDTH
    vcol = COL_V // KV_WIDTH
    qsum, qexp = _head_sums(N_Q_HEADS)
    ksum, kexp = _head_sums(N_KV_HEADS)

    def prev(i):
        return jnp.maximum(i - 1, 0)

    def nxt(i):
        return jnp.minimum(i + 1, nb - 1)

    def edge(i):
        return jnp.where(i == 0, 0, jnp.where(i == nb - 1, 2, 1))

    kv = lambda colb, f: pl.BlockSpec((1, BAND_BLOCK, KV_WIDTH), lambda bi, i: (bi, f(i), colb))
    same = lambda i: i
    const = lambda a: pl.BlockSpec(a.shape, lambda bi, i: (0,) * a.ndim)
    return pl.pallas_call(
        _attn_kernel,
        out_shape=jax.ShapeDtypeStruct((b, l, ATTN_WIDTH), BF16),
        grid=(b, nb),
        in_specs=[pl.BlockSpec(memory_space=pltpu.SMEM),
                  pl.BlockSpec((1, BAND_BLOCK, ATTN_WIDTH), lambda bi, i: (bi, i, COL_Q // ATTN_WIDTH)),
                  kv(kcol, prev), kv(kcol, same), kv(kcol, nxt),
                  kv(vcol, prev), kv(vcol, same), kv(vcol, nxt),
                  pl.BlockSpec((1, N_Q_HEADS, BAND_BLOCK, 3 * BAND_BLOCK), lambda bi, i: (edge(i), 0, 0, 0)),
                  const(qw_full), const(kw_full), const(qsum), const(qexp), const(ksum), const(kexp)],
        out_specs=pl.BlockSpec((1, BAND_BLOCK, ATTN_WIDTH), lambda bi, i: (bi, i, 0)),
        compiler_params=_cparams(("parallel", "parallel")),
        name="window_attention",
    )(sink, proj, proj, proj, proj, proj, proj, proj, bias_tab, qw_full, kw_full, qsum, qexp, ksum, kexp)


def _t5_bucket(rel):
    half = NUM_BUCKETS // 2
    max_exact = half // 2
    bucket = jnp.where(rel > 0, half, 0)
    n = jnp.abs(rel)
    nf = jnp.maximum(n, 1).astype(F32)
    large = max_exact + (jnp.log(nf / max_exact) / math.log(MAX_DISTANCE / max_exact)
                         * (half - max_exact)).astype(I32)
    large = jnp.minimum(large, half - 1)
    return bucket + jnp.where(n < max_exact, n, large)


def _bias_table(rel_bias):
    kpos = jnp.arange(3 * BAND_BLOCK)[None, :]
    rel = kpos - BAND_BLOCK - jnp.arange(BAND_BLOCK)[:, None]
    in_window = jnp.abs(rel) <= WINDOW
    bias = rel_bias[_t5_bucket(rel)].astype(F32).transpose(2, 0, 1)
    mid = jnp.where(in_window[None], bias, NEG_INF)
    first = jnp.where(kpos[None] < BAND_BLOCK, NEG_INF, mid)
    last = jnp.where(kpos[None] >= 2 * BAND_BLOCK, NEG_INF, mid)
    return jnp.stack([first, mid, last])


CONV_HALO = 16
CONV_TILE = 512


def _conv_kernel(prev_ref, cur_ref, next_ref, w_ref, b_ref, o_ref, *, nt):
    i = pl.program_id(1)
    tl = cur_ref.shape[1]
    cur = cur_ref[0].astype(F32)
    prev = prev_ref[0].astype(F32) * jnp.where(i > 0, 1.0, 0.0)
    nxt = next_ref[0].astype(F32) * jnp.where(i < nt - 1, 1.0, 0.0)
    ext = jnp.concatenate([prev, cur, nxt], axis=0)
    half = CONV_WIDTH // 2
    acc = jnp.broadcast_to(b_ref[...], cur.shape)
    for t in range(CONV_WIDTH):
        off = CONV_HALO - half + t
        acc = acc + ext[off:off + tl] * w_ref[t:t + 1, :]
    o_ref[0] = (acc * jax.nn.sigmoid(acc)).astype(o_ref.dtype)


def _conv_silu(proj, conv_w, conv_b):
    b, l, _ = proj.shape
    tl = min(l, CONV_TILE)
    nt = l // tl
    cw = CONV_TILE
    nch = CONV_CH // cw
    c0 = COL_XBC // cw
    hb = tl // CONV_HALO
    nh = l // CONV_HALO
    return pl.pallas_call(
        functools.partial(_conv_kernel, nt=nt),
        out_shape=jax.ShapeDtypeStruct((b, l, CONV_CH), BF16),
        grid=(b, nt, nch),
        in_specs=[pl.BlockSpec((1, CONV_HALO, cw), lambda bi, i, c: (bi, jnp.maximum(i * hb - 1, 0), c0 + c)),
                  pl.BlockSpec((1, tl, cw), lambda bi, i, c: (bi, i, c0 + c)),
                  pl.BlockSpec((1, CONV_HALO, cw), lambda bi, i, c: (bi, jnp.minimum((i + 1) * hb, nh - 1), c0 + c)),
                  pl.BlockSpec((CONV_WIDTH, cw), lambda bi, i, c: (0, c)),
                  pl.BlockSpec((1, cw), lambda bi, i, c: (0, c))],
        out_specs=pl.BlockSpec((1, tl, cw), lambda bi, i, c: (bi, i, c)),
        compiler_params=_cparams(("parallel", "parallel", "parallel")),
        name="conv_silu",
    )(proj, proj, proj, conv_w, conv_b.reshape(1, CONV_CH))


def _ssd_kernel(xs_ref, b_ref, c_ref, dt_ref, dtb_ref, alog_ref, e_ref, y_ref, st_ref, *, rev):
    @pl.when(pl.program_id(1) == 0)
    def _():
        st_ref[...] = jnp.zeros_like(st_ref)

    q = SSD_CHUNK
    off = SSD_HEADS if rev else 0
    z = dt_ref[0].astype(F32) + dtb_ref[...]
    dt = jnp.maximum(z, 0.0) + jnp.log1p(jnp.exp(-jnp.abs(z)))
    da = dt * (-jnp.exp(alog_ref[...]))
    ri = lax.broadcasted_iota(I32, (q, q), 0)
    ci = lax.broadcasted_iota(I32, (q, q), 1)
    if rev:
        mask = ci >= ri
        mask_t = ci <= ri
    else:
        mask = ci <= ri
        mask_t = ci >= ri
    tri = jnp.where(mask, 1.0, 0.0).astype(BF16)
    tri_t = jnp.where(mask_t, 1.0, 0.0).astype(BF16)
    cum = _dot2_left(tri, da)
    cum_t = _dot2(da.T, tri_t)
    tot = cum[0:1, :] if rev else cum[q - 1:q, :]
    e = e_ref[...]
    dt_full = _dot2(dt, e)
    cum_full = _dot2(cum, e)
    tot_full = _dot2(tot, e)
    xdt_f = xs_ref[0].astype(F32) * dt_full
    xdt = xdt_f.astype(BF16)
    xw = (xdt_f * jnp.exp(tot_full - cum_full)).astype(BF16)
    expcum = jnp.exp(cum_full)
    chunk_decay = jnp.exp(tot_full)
    gw = SSD_GROUP_WIDTH
    for g in range(SSD_GROUPS):
        bg = b_ref[0][:, g * SSD_STATE:(g + 1) * SSD_STATE]
        cg = c_ref[0][:, g * SSD_STATE:(g + 1) * SSD_STATE]
        bgt = bg.astype(F32).T.astype(BF16)
        cb = jnp.dot(cg, bgt, preferred_element_type=F32)
        ys = []
        for r in range(SSD_HEADS_PER_GROUP):
            h = g * SSD_HEADS_PER_GROUP + r
            seg = cum[:, off + h:off + h + 1] - cum_t[off + h:off + h + 1, :]
            dec = jnp.exp(jnp.where(mask, seg, NEG_INF))
            m = (cb * dec).astype(BF16)
            ys.append(jnp.dot(m, xdt[:, h * SSD_HEAD_DIM:(h + 1) * SSD_HEAD_DIM],
                              preferred_element_type=F32))
        y_diag = jnp.concatenate(ys, axis=1)
        st = st_ref[g]
        y_off = jnp.dot(cg, st.astype(BF16), preferred_element_type=F32) * expcum[:, g * gw:(g + 1) * gw]
        new = jnp.dot(bgt, xw[:, g * gw:(g + 1) * gw], preferred_element_type=F32)
        st_ref[g] = st * chunk_decay[:, g * gw:(g + 1) * gw] + new
        y_ref[0, :, g * gw:(g + 1) * gw] = (y_diag + y_off).astype(y_ref.dtype)


def _ssd_scan(xbc, proj, dtb_row, alog_row, expand, rev):
    b, l, _ = xbc.shape
    nc = l // SSD_CHUNK
    gs = SSD_GROUPS * SSD_STATE
    cidx = (lambda c: nc - 1 - c) if rev else (lambda c: c)
    return pl.pallas_call(
        functools.partial(_ssd_kernel, rev=rev),
        out_shape=jax.ShapeDtypeStruct((b, l, SSD_INNER), BF16),
        grid=(b, nc),
        in_specs=[pl.BlockSpec((1, SSD_CHUNK, SSD_INNER), lambda bi, c: (bi, cidx(c), 0)),
                  pl.BlockSpec((1, SSD_CHUNK, gs), lambda bi, c: (bi, cidx(c), SSD_INNER // gs)),
                  pl.BlockSpec((1, SSD_CHUNK, gs), lambda bi, c: (bi, cidx(c), SSD_INNER // gs + 1)),
                  pl.BlockSpec((1, SSD_CHUNK, LANES), lambda bi, c: (bi, cidx(c), COL_DT // LANES)),
                  pl.BlockSpec((1, LANES), lambda bi, c: (0, 0)),
                  pl.BlockSpec((1, LANES), lambda bi, c: (0, 0)),
                  pl.BlockSpec((LANES, SSD_INNER), lambda bi, c: (0, 0))],
        out_specs=pl.BlockSpec((1, SSD_CHUNK, SSD_INNER), lambda bi, c: (bi, cidx(c), 0)),
        scratch_shapes=[pltpu.VMEM((SSD_GROUPS, SSD_STATE, SSD_GROUP_WIDTH), F32)],
        compiler_params=_cparams(("parallel", "arbitrary")),
        name="ssd_scan_bwd" if rev else "ssd_scan_fwd",
    )(xbc, xbc, xbc, proj, dtb_row, alog_row, expand)


def _merge_kernel(x_ref, yf_ref, yb_ref, xs_ref, z_ref, ao_ref, ga_ref, gs_ref, g1_ref,
                  dsk_ref, snw_ref, wssd_ref, wattn_ref, wout_ref, o_ref):
    y = yf_ref[0].astype(F32) + yb_ref[0].astype(F32) + xs_ref[0].astype(F32) * dsk_ref[...]
    z = z_ref[0].astype(F32)
    y = y * (z * jax.nn.sigmoid(z))
    parts = []
    for g in range(SSD_GROUPS):
        yg = y[:, g * SSD_GROUP_WIDTH:(g + 1) * SSD_GROUP_WIDTH]
        parts.append(yg * lax.rsqrt(jnp.mean(yg * yg, axis=-1, keepdims=True) + EPS))
    y = jnp.concatenate(parts, axis=1) * snw_ref[...]
    ssd = jnp.dot(y.astype(BF16), wssd_ref[...], preferred_element_type=F32)
    attn = jnp.dot(ao_ref[0], wattn_ref[...], preferred_element_type=F32)
    merged = (jax.nn.sigmoid(ga_ref[0].astype(F32)) * attn
              + jax.nn.sigmoid(gs_ref[0].astype(F32)) * ssd)
    out = jnp.dot(merged.astype(BF16), wout_ref[...], preferred_element_type=F32)
    o_ref[0] = x_ref[0] + g1_ref[0] * out


def _merge(x, yf, yb, xbc, proj, attn_o, modr, dsk_full, snw, w_ssd, w_attn, w_out):
    b, l, _ = x.shape
    tl = min(l, 256)
    tok = lambda w, colb: pl.BlockSpec((1, tl, w), lambda bi, i: (bi, i, colb))
    full = lambda r, c: pl.BlockSpec((r, c), lambda bi, i: (0, 0))
    return pl.pallas_call(
        _merge_kernel,
        out_shape=jax.ShapeDtypeStruct((b, l, D_MODEL), F32),
        grid=(b, l // tl),
        in_specs=[tok(D_MODEL, 0), tok(SSD_INNER, 0), tok(SSD_INNER, 0), tok(SSD_INNER, 0),
                  tok(SSD_INNER, COL_Z // SSD_INNER), tok(ATTN_WIDTH, 0),
                  tok(D_MODEL, COL_GA // D_MODEL), tok(D_MODEL, COL_GS // D_MODEL),
                  pl.BlockSpec((1, 1, D_MODEL), lambda bi, i: (bi * N_MOD + 2, 0, 0)),
                  full(1, SSD_INNER), full(1, SSD_INNER),
                  full(SSD_INNER, D_MODEL), full(ATTN_WIDTH, D_MODEL), full(D_MODEL, D_MODEL)],
        out_specs=tok(D_MODEL, 0),
        compiler_params=_cparams(("parallel", "parallel")),
        name="merge_out_proj",
    )(x, yf, yb, xbc, proj, attn_o, proj, proj, modr, dsk_full, snw, w_ssd, w_attn, w_out)


def _perm_src(j):
    return (j % 2) * (D_MODEL // 2) + (j // 2) * LANES


def _topk_rows(s, k):
    rowf = lax.broadcasted_iota(I32, s.shape, 0).astype(F32)
    vals, ids = [], []
    for _ in range(k):
        m = jnp.max(s, axis=0, keepdims=True)
        am = jnp.min(jnp.where(s == m, rowf, float(s.shape[0])), axis=0, keepdims=True)
        vals.append(m)
        ids.append(am)
        s = jnp.where(rowf == am, -jnp.inf, s)
    return vals, ids


def _candidate_pieces():
    pieces = []
    for a in range(PEER_TOPK):
        nb = PEER_TOPK // (a + 1)
        if nb >= SUBLANES // 2:
            for b0 in range(0, nb, SUBLANES):
                pieces.append((a, 1, b0, SUBLANES))
    a_done = max(p[0] for p in pieces) + 1
    for b in range(PEER_TOPK):
        na = PEER_TOPK // (b + 1)
        if na > a_done:
            for a0 in range(0, na, SUBLANES):
                pieces.append((a0, SUBLANES, b, 1))
    return pieces, a_done


def _route_kernel(x_ref, sh_ref, sc_ref, nw_ref, wq_ref, keys_ref, hp_ref, idx_ref, gate_ref):
    h = _norm_mod(x_ref[0], nw_ref[...], sc_ref[0], sh_ref[0])
    for j in range(D_MODEL // LANES):
        hp_ref[0, :, j * LANES:(j + 1) * LANES] = h[:, _perm_src(j):_perm_src(j) + LANES]
    qall = jnp.dot(h.astype(BF16), wq_ref[...], preferred_element_type=F32)
    ntok = qall.shape[0]
    tops, topi = [], []
    for c in range(2 * PEER_HEADS):
        qc = qall[:, c * PEER_HALF:(c + 1) * PEER_HALF].astype(BF16)
        s = lax.dot_general(keys_ref[c], qc, (((1,), (1,)), ((), ())), preferred_element_type=F32)
        vals, ids = _topk_rows(s, PEER_TOPK)
        tops.append(vals)
        topi.append(ids)
    pieces, a_done = _candidate_pieces()
    sub = lax.broadcasted_iota(I32, (SUBLANES, ntok), 0)
    subf = sub.astype(F32)
    pos_parts, drop_parts = [], []
    for a0, na, b0, nb in pieces:
        if na == 1:
            pos_parts.append(subf + float(a0 * PEER_TOPK + b0))
            keep = sub + b0 < PEER_TOPK // (a0 + 1)
        else:
            pos_parts.append(subf * float(PEER_TOPK) + float(a0 * PEER_TOPK + b0))
            keep = jnp.where(sub + a0 >= a_done, sub + a0, PEER_TOPK) < PEER_TOPK // (b0 + 1)
        drop_parts.append(jnp.where(keep, 0.0, -jnp.inf))
    pos = jnp.concatenate(pos_parts, axis=0)
    drop = jnp.concatenate(drop_parts, axis=0)
    idx_rows, gate_rows = [], []
    for hd in range(PEER_HEADS):
        s0 = jnp.concatenate(tops[2 * hd], axis=0)
        i0 = jnp.concatenate(topi[2 * hd], axis=0)
        s1 = jnp.concatenate(tops[2 * hd + 1], axis=0)
        i1 = jnp.concatenate(topi[2 * hd + 1], axis=0)
        cparts, iparts = [], []
        for a0, na, b0, nb in pieces:
            if na == 1:
                cparts.append(tops[2 * hd][a0] + s1[b0:b0 + nb])
                iparts.append(topi[2 * hd][a0] * float(N_KEYS) + i1[b0:b0 + nb])
            else:
                cparts.append(s0[a0:a0 + na] + tops[2 * hd + 1][b0])
                iparts.append(i0[a0:a0 + na] * float(N_KEYS) + topi[2 * hd + 1][b0])
        cand = jnp.concatenate(cparts, axis=0) + drop
        cidx = jnp.concatenate(iparts, axis=0)
        best = []
        for _ in range(PEER_TOPK):
            m = jnp.max(cand, axis=0, keepdims=True)
            first = jnp.min(jnp.where(cand == m, pos, float(PEER_TOPK * PEER_TOPK)), axis=0, keepdims=True)
            sel = pos == first
            idx_rows.append(jnp.max(jnp.where(sel, cidx, -1.0), axis=0, keepdims=True))
            best.append(m)
            cand = jnp.where(sel, -jnp.inf, cand)
        bs = jnp.concatenate(best, axis=0)
        p = jnp.exp(bs - best[0])
        gate_rows.append(p / jnp.sum(p, axis=0, keepdims=True))
    idx_t = jnp.concatenate(idx_rows, axis=0)
    gate_t = jnp.concatenate(gate_rows, axis=0)
    idx_ref[0] = (idx_t * float(ROW_WORDS)).T.astype(I32)
    gate_ref[0] = gate_t.T


def _route(x1, modr, norm_w, wq, keys):
    b, l, _ = x1.shape
    tl = min(l, 256)
    tok = lambda w: pl.BlockSpec((1, tl, w), lambda bi, i: (bi, i, 0))
    return pl.pallas_call(
        _route_kernel,
        out_shape=(jax.ShapeDtypeStruct((b, l, D_MODEL), F32),
                   jax.ShapeDtypeStruct((b, l, N_SEL), I32),
                   jax.ShapeDtypeStruct((b, l, N_SEL), F32)),
        grid=(b, l // tl),
        in_specs=[tok(D_MODEL),
                  pl.BlockSpec((1, 1, D_MODEL), lambda bi, i: (bi * N_MOD + 3, 0, 0)),
                  pl.BlockSpec((1, 1, D_MODEL), lambda bi, i: (bi * N_MOD + 4, 0, 0)),
                  pl.BlockSpec((1, D_MODEL), lambda bi, i: (0, 0)),
                  pl.BlockSpec((D_MODEL, PEER_HEADS * PEER_KEY_DIM), lambda bi, i: (0, 0)),
                  pl.BlockSpec((2 * PEER_HEADS, N_KEYS, PEER_HALF), lambda bi, i: (0, 0, 0))],
        out_specs=(tok(D_MODEL), tok(N_SEL), tok(N_SEL)),
        compiler_params=_cparams(("parallel", "parallel")),
        name="peer_route",
    )(x1, modr, modr, norm_w, wq, keys)


def _pack_table(tab):
    n = tab.shape[0]
    bits = lax.bitcast_convert_type(tab.astype(BF16), jnp.uint16).astype(jnp.uint32)
    lo = bits[:, :D_MODEL // 2]
    hi = bits[:, D_MODEL // 2:]
    words = lax.bitcast_convert_type(lo | (hi << 16), I32)
    return words.reshape(n * ROW_WORDS, LANES)


def _gather_rows(idx_ref, base, tab_ref, tile_ref):
    for k in range(N_SEL):
        r = pl.multiple_of(idx_ref[base + k], ROW_WORDS)
        tile_ref[k * ROW_WORDS:(k + 1) * ROW_WORDS, :] = tab_ref[pl.ds(r, ROW_WORDS), :]


def _aligned(v, m):
    return v if isinstance(v, int) else pl.multiple_of(v, m)


def _for_groups(ngroups, body):
    if ngroups == 1:
        body(0)
    else:
        def step(gi, carry):
            body(gi)
            return carry
        lax.fori_loop(0, ngroups, step, 0)


def _peer_act_kernel(idx_ref, x_ref, tab_ref, dmask_ref, gsum_ref, act_ref, tiles, *, block, group):
    def body(gi):
        t0 = gi * group
        vs = []
        for tt in range(group):
            _gather_rows(idx_ref, (t0 + tt) * N_SEL, tab_ref, tiles.at[tt])
            rows = pltpu.bitcast(tiles[tt], BF16)
            x8 = x_ref[pl.ds(_aligned((t0 + tt) * SUBLANES, SUBLANES), SUBLANES), :].astype(BF16)
            r = lax.dot_general(x8, rows, (((1,), (1,)), ((), ())), preferred_element_type=F32)
            vs.append(jnp.sum(r * dmask_ref[...], axis=0, keepdims=True))
        v = jnp.concatenate(vs, axis=0)
        act_ref[pl.ds(_aligned(t0, group), group), :] = _dot2(v, gsum_ref[...])

    _for_groups(block // group, body)


def _gelu_tanh(x):
    return 0.5 * x * (1.0 + jnp.tanh(math.sqrt(2.0 / math.pi) * (x + 0.044715 * (x * x * x))))


def _peer_out_kernel(idx_ref, act_ref, gate_ref, tab_ref, dmask_ref, gexp_ref, out_ref, tiles, *, block, group):
    def body(gi):
        t0 = _aligned(gi * group, group)
        a = act_ref[pl.ds(t0, group), :]
        w = gate_ref[pl.ds(t0, group), :] * _gelu_tanh(a)
        wexp = jnp.dot(w.astype(BF16), gexp_ref[...], preferred_element_type=F32)
        for tt in range(group):
            _gather_rows(idx_ref, (t0 + tt) * N_SEL, tab_ref, tiles.at[tt])
            rows = pltpu.bitcast(tiles[tt], BF16)
            w8 = (jnp.broadcast_to(wexp[tt:tt + 1, :], dmask_ref.shape) * dmask_ref[...]).astype(BF16)
            o8 = jnp.dot(w8, rows, preferred_element_type=F32)
            out_ref[pl.ds(_aligned((t0 + tt) * SUBLANES, SUBLANES), SUBLANES), :] = o8

    _for_groups(block // group, body)


def _peer_consts():
    ncol = N_SEL * SUBLANES
    c = jnp.arange(ncol)
    dmask = (c[None, :] % SUBLANES == jnp.arange(SUBLANES)[:, None]).astype(F32)
    gsum = (c[:, None] // SUBLANES == jnp.arange(N_SEL)[None, :]).astype(BF16)
    return dmask, gsum, gsum.T


def _peer_specs(block, group):
    ncol = N_SEL * SUBLANES
    idx_spec = pl.BlockSpec((block * N_SEL,), lambda i: (i,), memory_space=pltpu.SMEM)
    tab_spec = pl.BlockSpec(memory_space=pltpu.VMEM)
    dmask_spec = pl.BlockSpec((SUBLANES, ncol), lambda i: (0, 0))
    tok_spec = pl.BlockSpec((block, N_SEL), lambda i: (i, 0))
    row_spec = pl.BlockSpec((block * SUBLANES, LANES), lambda i: (i, 0))
    tiles = pltpu.VMEM((group, N_SEL * ROW_WORDS, LANES), I32)
    return idx_spec, tab_spec, dmask_spec, tok_spec, row_spec, tiles


def _peer_act(idx_flat, x8, tab_u, dmask, gsum, block, group):
    t = x8.shape[0] // SUBLANES
    idx_spec, tab_spec, dmask_spec, tok_spec, row_spec, tiles = _peer_specs(block, group)
    return pl.pallas_call(
        functools.partial(_peer_act_kernel, block=block, group=group),
        out_shape=jax.ShapeDtypeStruct((t, N_SEL), F32),
        grid=(t // block,),
        in_specs=[idx_spec, row_spec, tab_spec, dmask_spec,
                  pl.BlockSpec(gsum.shape, lambda i: (0, 0))],
        out_specs=tok_spec,
        scratch_shapes=[tiles],
        compiler_params=_cparams(("parallel",)),
        name="peer_act",
    )(idx_flat, x8, tab_u, dmask, gsum)


def _peer_out(idx_flat, act, gates, tab_v, dmask, gexp, block, group):
    t = act.shape[0]
    idx_spec, tab_spec, dmask_spec, tok_spec, row_spec, tiles = _peer_specs(block, group)
    return pl.pallas_call(
        functools.partial(_peer_out_kernel, block=block, group=group),
        out_shape=jax.ShapeDtypeStruct((t * SUBLANES, LANES), F32),
        grid=(t // block,),
        in_specs=[idx_spec, tok_spec, tok_spec, tab_spec, dmask_spec,
                  pl.BlockSpec(gexp.shape, lambda i: (0, 0))],
        out_specs=row_spec,
        scratch_shapes=[tiles],
        compiler_params=_cparams(("parallel",)),
        name="peer_out",
    )(idx_flat, act, gates, tab_v, dmask, gexp)


def _final_kernel(x_ref, o_ref, g2_ref, y_ref):
    for j in range(D_MODEL // LANES):
        s = _perm_src(j)
        y_ref[0, :, s:s + LANES] = (x_ref[0, :, s:s + LANES]
                                    + g2_ref[0, :, s:s + LANES] * o_ref[0, :, j * LANES:(j + 1) * LANES])


def _final_residual(x1, outp, modr):
    b, l, _ = x1.shape
    tl = min(l, 512)
    tok = pl.BlockSpec((1, tl, D_MODEL), lambda bi, i: (bi, i, 0))
    return pl.pallas_call(
        _final_kernel,
        out_shape=jax.ShapeDtypeStruct((b, l, D_MODEL), F32),
        grid=(b, l // tl),
        in_specs=[tok, tok, pl.BlockSpec((1, 1, D_MODEL), lambda bi, i: (bi * N_MOD + 5, 0, 0))],
        out_specs=tok,
        compiler_params=_cparams(("parallel", "parallel")),
        name="peer_residual",
    )(x1, outp, modr)


def _prepare(rel_bias, ada_w, ada_b, norm1_w, norm2_w, w_in, q_norm_w, k_norm_w, attn_sink, conv_w,
             conv_b, a_log, dt_bias, d_skip, ssd_norm_w, w_attn_br, w_ssd_br, w_out, peer_wq,
             peer_keys, peer_u, peer_v):
    lyr = 0
    w = w_in[lyr]
    o = 0
    parts = {}
    for name, width in (("q", ATTN_WIDTH), ("k", KV_WIDTH), ("v", KV_WIDTH), ("z", SSD_INNER),
                        ("xbc", CONV_CH), ("dt", 2 * SSD_HEADS), ("ga", D_MODEL), ("gs", D_MODEL)):
        parts[name] = w[:, o:o + width]
        o += width
    pad = jnp.zeros((D_MODEL, PROJ_W - (COL_DT + 2 * SSD_HEADS)), w.dtype)
    w_all = jnp.concatenate([parts["z"], parts["q"], parts["ga"], parts["gs"], parts["xbc"],
                             parts["k"], parts["v"], parts["dt"], pad], axis=1).astype(BF16)
    lane_pad = LANES - 2 * SSD_HEADS
    expand = (jnp.arange(SSD_INNER)[None, :] // SSD_HEAD_DIM == jnp.arange(LANES)[:, None])
    dmask, gsum, gexp = _peer_consts()
    return dict(
        ada_w=ada_w[lyr], ada_b=ada_b[lyr],
        norm1_w=norm1_w[lyr].reshape(1, D_MODEL), norm2_w=norm2_w[lyr].reshape(1, D_MODEL),
        w_all=w_all,
        bias_tab=_bias_table(rel_bias), sink=attn_sink[lyr].astype(F32),
        qw=(jnp.tile(q_norm_w[lyr], N_Q_HEADS) * HEAD_DIM ** -0.5).reshape(1, ATTN_WIDTH),
        kw=jnp.tile(k_norm_w[lyr], N_KV_HEADS).reshape(1, KV_WIDTH),
        conv_w=conv_w[lyr], conv_b=conv_b[lyr],
        dtb_row=jnp.pad(dt_bias[lyr].astype(F32).reshape(1, -1), ((0, 0), (0, lane_pad))),
        alog_row=jnp.pad(a_log[lyr].astype(F32).reshape(1, -1), ((0, 0), (0, lane_pad))),
        expand_fwd=expand.astype(BF16),
        expand_bwd=jnp.roll(expand, SSD_HEADS, axis=0).astype(BF16),
        dsk_full=jnp.repeat(d_skip[lyr], SSD_HEAD_DIM).reshape(1, SSD_INNER),
        snw=ssd_norm_w[lyr].reshape(1, SSD_INNER),
        w_ssd=w_ssd_br[lyr].astype(BF16), w_attn=w_attn_br[lyr].astype(BF16), w_out=w_out[lyr].astype(BF16),
        wq=peer_wq[lyr].astype(BF16),
        keys=peer_keys[lyr].reshape(2 * PEER_HEADS, N_KEYS, PEER_HALF).astype(BF16),
        tab_u=_pack_table(peer_u[lyr]), tab_v=_pack_table(peer_v[lyr]),
        dmask=dmask, gsum=gsum, gexp=gexp,
    )


def _token_mixer_stage(x, modr, p):
    proj = _in_projection(x, modr, p["norm1_w"], p["w_all"])
    attn_o = _attention(proj, p["bias_tab"], p["sink"], p["qw"], p["kw"])
    xbc = _conv_silu(proj, p["conv_w"], p["conv_b"])
    yf = _ssd_scan(xbc, proj, p["dtb_row"], p["alog_row"], p["expand_fwd"], rev=False)
    yb = _ssd_scan(xbc, proj, p["dtb_row"], p["alog_row"], p["expand_bwd"], rev=True)
    return _merge(x, yf, yb, xbc, proj, attn_o, modr, p["dsk_full"], p["snw"],
                  p["w_ssd"], p["w_attn"], p["w_out"])


def _peer_stage(x1, modr, p):
    b, l, _ = x1.shape
    t = b * l
    hp, idx, gates = _route(x1, modr, p["norm2_w"], p["wq"], p["keys"])
    idx_flat = idx.reshape(t * N_SEL)
    x8 = hp.reshape(t * SUBLANES, LANES)
    block, group = (PEER_BLOCK, PEER_TOK_GROUP) if b > 4 else (16, 16)
    act = _peer_act(idx_flat, x8, p["tab_u"], p["dmask"], p["gsum"], block, group)
    outp = _peer_out(idx_flat, act, gates.reshape(t, N_SEL), p["tab_v"], p["dmask"], p["gexp"], block, group)
    return _final_residual(x1, outp.reshape(b, l, D_MODEL), modr)


def _encoder(x, c, p):
    nb = c.shape[0]
    modr = _modulation(c, p["ada_w"], p["ada_b"]).reshape(nb * N_MOD, 1, D_MODEL)
    x1 = _token_mixer_stage(x, modr, p)
    return _peer_stage(x1, modr, p)


def kernel(x_prompt, x_sample, c_prompt, c_sample, rel_bias, ada_w, ada_b, norm1_w, norm2_w, w_in,
           q_norm_w, k_norm_w, attn_sink, conv_w, conv_b, a_log, dt_bias, d_skip, ssd_norm_w,
           w_attn_br, w_ssd_br, w_out, peer_wq, peer_keys, peer_u, peer_v):
    p = _prepare(rel_bias, ada_w, ada_b, norm1_w, norm2_w, w_in, q_norm_w, k_norm_w, attn_sink,
                 conv_w, conv_b, a_log, dt_bias, d_skip, ssd_norm_w, w_attn_br, w_ssd_br, w_out,
                 peer_wq, peer_keys, peer_u, peer_v)
    return (_encoder(x_prompt, c_prompt, p), _encoder(x_sample, c_sample, p))
```

```python
import functools
import math

import jax
import jax.numpy as jnp
from jax import lax
from jax.experimental import pallas as pl
from jax.experimental.pallas import tpu as pltpu

F32 = jnp.float32
BF16 = jnp.bfloat16
I32 = jnp.int32

D_MODEL = 1024
HEAD_DIM = 64
N_Q_HEADS = 16
N_KV_HEADS = 4
Q_PER_KV = N_Q_HEADS // N_KV_HEADS
ATTN_WIDTH = N_Q_HEADS * HEAD_DIM
KV_WIDTH = N_KV_HEADS * HEAD_DIM
WINDOW = 128
BAND_BLOCK = 128
NUM_BUCKETS = 32
MAX_DISTANCE = 128
NEG_INF = -1e30

SSD_INNER = 2 * D_MODEL
SSD_HEAD_DIM = 64
SSD_HEADS = SSD_INNER // SSD_HEAD_DIM
SSD_GROUPS = 4
SSD_HEADS_PER_GROUP = SSD_HEADS // SSD_GROUPS
SSD_STATE = 128
SSD_CHUNK = 128
SSD_GROUP_WIDTH = SSD_INNER // SSD_GROUPS
CONV_WIDTH = 5
CONV_CH = SSD_INNER + 2 * SSD_GROUPS * SSD_STATE

PEER_HEADS = 8
PEER_KEY_DIM = 256
PEER_HALF = PEER_KEY_DIM // 2
N_KEYS = 128
N_EXPERTS = N_KEYS * N_KEYS
PEER_TOPK = 16
N_SEL = PEER_HEADS * PEER_TOPK

N_MOD = 6
EPS = 1e-6

COL_Z = 0
COL_Q = 2048
COL_GA = 3072
COL_GS = 4096
COL_XBC = 5120
COL_K = 8192
COL_V = 8448
COL_DT = 8704
PROJ_W = 9216

LANES = 128
SUBLANES = 8
VMEM_LIMIT = 56 * 1024 * 1024
ROW_WORDS = D_MODEL // (2 * LANES)
SLAB_ROWS = 2 * ROW_WORDS
PEER_TOK_GROUP = 16
PEER_BLOCK = 128
PEER_CHUNK = 16
PEER_STATIC_CHUNKS = 5


def _cparams(sem):
    return pltpu.CompilerParams(dimension_semantics=sem, vmem_limit_bytes=VMEM_LIMIT)


def _split_bf16(v):
    hi = v.astype(BF16)
    lo = (v - hi.astype(F32)).astype(BF16)
    return hi, lo


def _dot2(v, m_bf16):
    hi, lo = _split_bf16(v)
    return (jnp.dot(hi, m_bf16, preferred_element_type=F32)
            + jnp.dot(lo, m_bf16, preferred_element_type=F32))


def _dot2_left(m_bf16, v):
    hi, lo = _split_bf16(v)
    return (jnp.dot(m_bf16, hi, preferred_element_type=F32)
            + jnp.dot(m_bf16, lo, preferred_element_type=F32))


def _mod_kernel(c_ref, w_ref, b_ref, o_ref):
    c = c_ref[...]
    sc = c * jax.nn.sigmoid(c)
    o_ref[...] = jnp.dot(sc, w_ref[...], preferred_element_type=F32,
                         precision=lax.Precision.HIGHEST) + b_ref[...]


def _modulation(c, ada_w, ada_b):
    nb = c.shape[0]
    n = ada_w.shape[1]
    tn = 1024
    return pl.pallas_call(
        _mod_kernel,
        out_shape=jax.ShapeDtypeStruct((nb, n), F32),
        grid=(n // tn,),
        in_specs=[pl.BlockSpec((nb, D_MODEL), lambda j: (0, 0)),
                  pl.BlockSpec((D_MODEL, tn), lambda j: (0, j)),
                  pl.BlockSpec((1, tn), lambda j: (0, j))],
        out_specs=pl.BlockSpec((nb, tn), lambda j: (0, j)),
        compiler_params=_cparams(("arbitrary",)),
        name="adaln_mod",
    )(c, ada_w, ada_b.reshape(1, n))


def _norm_mod(x, nw, sc, sh):
    ms = jnp.mean(x * x, axis=-1, keepdims=True)
    h = x * lax.rsqrt(ms + EPS) * nw
    return h * (1.0 + sc) + sh


def _inproj_kernel(x_ref, sh_ref, sc_ref, nw_ref, w_ref, o_ref, h_scr):
    @pl.when(pl.program_id(2) == 0)
    def _():
        h = _norm_mod(x_ref[0], nw_ref[...], sc_ref[0], sh_ref[0])
        h_scr[...] = h.astype(BF16)

    o_ref[0] = jnp.dot(h_scr[...], w_ref[...], preferred_element_type=F32).astype(o_ref.dtype)


def _in_projection(x, modr, norm_w, w_all):
    b, l, _ = x.shape
    tl = min(l, 1024)
    tn = 512
    return pl.pallas_call(
        _inproj_kernel,
        out_shape=jax.ShapeDtypeStruct((b, l, PROJ_W), BF16),
        grid=(b, l // tl, PROJ_W // tn),
        in_specs=[pl.BlockSpec((1, tl, D_MODEL), lambda bi, i, j: (bi, i, 0)),
                  pl.BlockSpec((1, 1, D_MODEL), lambda bi, i, j: (bi * N_MOD + 0, 0, 0)),
                  pl.BlockSpec((1, 1, D_MODEL), lambda bi, i, j: (bi * N_MOD + 1, 0, 0)),
                  pl.BlockSpec((1, D_MODEL), lambda bi, i, j: (0, 0)),
                  pl.BlockSpec((D_MODEL, tn), lambda bi, i, j: (0, j))],
        out_specs=pl.BlockSpec((1, tl, tn), lambda bi, i, j: (bi, i, j)),
        scratch_shapes=[pltpu.VMEM((tl, D_MODEL), BF16)],
        compiler_params=_cparams(("parallel", "parallel", "arbitrary")),
        name="in_projection",
    )(x, modr, modr, norm_w, w_all)


def _head_sums(n_heads):
    c = jnp.arange(n_heads * HEAD_DIM)
    hsum = (c[:, None] // HEAD_DIM == jnp.arange(LANES)[None, :]).astype(BF16)
    return hsum, hsum.T


def _qk_norm(t, hsum, hexp, w_full):
    ssq = _dot2(t * t, hsum)
    inv = lax.rsqrt(ssq * (1.0 / HEAD_DIM) + EPS)
    return t * _dot2(inv, hexp) * w_full


def _attn_kernel(sink_ref, q_ref, kp_ref, kc_ref, kn_ref, vp_ref, vc_ref, vn_ref, bias_ref,
                 qw_ref, kw_ref, qsum_ref, qexp_ref, ksum_ref, kexp_ref, o_ref):
    q = _qk_norm(q_ref[0].astype(F32), qsum_ref[...], qexp_ref[...], qw_ref[...]).astype(BF16)
    k = jnp.concatenate([kp_ref[0], kc_ref[0], kn_ref[0]], axis=0).astype(F32)
    k = _qk_norm(k, ksum_ref[...], kexp_ref[...], kw_ref[...])
    v = jnp.concatenate([vp_ref[0], vc_ref[0], vn_ref[0]], axis=0).astype(F32)
    low = lax.broadcasted_iota(I32, (1, LANES), 1) < HEAD_DIM
    nt = (((1,), (1,)), ((), ()))
    for c in range(N_KV_HEADS // 2):
        kslab = k[:, c * LANES:(c + 1) * LANES]
        vslab = v[:, c * LANES:(c + 1) * LANES]
        kroll = pltpu.roll(kslab, HEAD_DIM, axis=1)
        vroll = pltpu.roll(vslab, HEAD_DIM, axis=1)
        for e in range(2):
            h = 2 * c + e
            k_lo = jnp.where(low, kroll if e else kslab, 0.0).astype(BF16)
            k_hi = jnp.where(low, 0.0, kslab if e else kroll).astype(BF16)
            v_lo = jnp.where(low, vroll if e else vslab, 0.0).astype(BF16)
            v_hi = jnp.where(low, 0.0, vslab if e else vroll).astype(BF16)
            for pair in range(Q_PER_KV // 2):
                slab = h * (Q_PER_KV // 2) + pair
                qs = q[:, slab * LANES:(slab + 1) * LANES]
                acc = None
                for par in range(2):
                    hd = 2 * slab + par
                    s = lax.dot_general(qs, k_hi if par else k_lo, nt, preferred_element_type=F32)
                    s = s + bias_ref[0, hd]
                    sk = sink_ref[hd]
                    m = jnp.maximum(jnp.max(s, axis=-1, keepdims=True), sk)
                    p = jnp.exp(s - m)
                    denom = jnp.sum(p, axis=-1, keepdims=True) + jnp.exp(sk - m)
                    pv = jnp.dot(p.astype(BF16), v_hi if par else v_lo, preferred_element_type=F32) / denom
                    acc = pv if acc is None else acc + pv
                o_ref[0, :, slab * LANES:(slab + 1) * LANES] = acc.astype(o_ref.dtype)


def _attention(proj, bias_tab, sink, qw_full, kw_full):
    b, l, _ = proj.shape
    nb = l // BAND_BLOCK
    assert nb >= 2
    kcol = COL_K // KV_WIDTH
    vcol = COL_V // KV_WIDTH
    qsum, qexp = _head_sums(N_Q_HEADS)
    ksum, kexp = _head_sums(N_KV_HEADS)

    def prev(i):
        return jnp.maximum(i - 1, 0)

    def nxt(i):
        return jnp.minimum(i + 1, nb - 1)

    def edge(i):
        return jnp.where(i == 0, 0, jnp.where(i == nb - 1, 2, 1))

    kv = lambda colb, f: pl.BlockSpec((1, BAND_BLOCK, KV_WIDTH), lambda bi, i: (bi, f(i), colb))
    same = lambda i: i
    const = lambda a: pl.BlockSpec(a.shape, lambda bi, i: (0,) * a.ndim)
    return pl.pallas_call(
        _attn_kernel,
        out_shape=jax.ShapeDtypeStruct((b, l, ATTN_WIDTH), BF16),
        grid=(b, nb),
        in_specs=[pl.BlockSpec(memory_space=pltpu.SMEM),
                  pl.BlockSpec((1, BAND_BLOCK, ATTN_WIDTH), lambda bi, i: (bi, i, COL_Q // ATTN_WIDTH)),
                  kv(kcol, prev), kv(kcol, same), kv(kcol, nxt),
                  kv(vcol, prev), kv(vcol, same), kv(vcol, nxt),
                  pl.BlockSpec((1, N_Q_HEADS, BAND_BLOCK, 3 * BAND_BLOCK), lambda bi, i: (edge(i), 0, 0, 0)),
                  const(qw_full), const(kw_full), const(qsum), const(qexp), const(ksum), const(kexp)],
        out_specs=pl.BlockSpec((1, BAND_BLOCK, ATTN_WIDTH), lambda bi, i: (bi, i, 0)),
        compiler_params=_cparams(("parallel", "parallel")),
        name="window_attention",
    )(sink, proj, proj, proj, proj, proj, proj, proj, bias_tab, qw_full, kw_full, qsum, qexp, ksum, kexp)


def _t5_bucket(rel):
    half = NUM_BUCKETS // 2
    max_exact = half // 2
    bucket = jnp.where(rel > 0, half, 0)
    n = jnp.abs(rel)
    nf = jnp.maximum(n, 1).astype(F32)
    large = max_exact + (jnp.log(nf / max_exact) / math.log(MAX_DISTANCE / max_exact)
                         * (half - max_exact)).astype(I32)
    large = jnp.minimum(large, half - 1)
    return bucket + jnp.where(n < max_exact, n, large)


def _bias_table(rel_bias):
    kpos = jnp.arange(3 * BAND_BLOCK)[None, :]
    rel = kpos - BAND_BLOCK - jnp.arange(BAND_BLOCK)[:, None]
    in_window = jnp.abs(rel) <= WINDOW
    bias = rel_bias[_t5_bucket(rel)].astype(F32).transpose(2, 0, 1)
    mid = jnp.where(in_window[None], bias, NEG_INF)
    first = jnp.where(kpos[None] < BAND_BLOCK, NEG_INF, mid)
    last = jnp.where(kpos[None] >= 2 * BAND_BLOCK, NEG_INF, mid)
    return jnp.stack([first, mid, last])


CONV_HALO = 16
CONV_TILE = 512


def _conv_kernel(prev_ref, cur_ref, next_ref, w_ref, b_ref, o_ref, *, nt):
    i = pl.program_id(1)
    tl = cur_ref.shape[1]
    cur = cur_ref[0].astype(F32)
    prev = prev_ref[0].astype(F32) * jnp.where(i > 0, 1.0, 0.0)
    nxt = next_ref[0].astype(F32) * jnp.where(i < nt - 1, 1.0, 0.0)
    ext = jnp.concatenate([prev, cur, nxt], axis=0)
    half = CONV_WIDTH // 2
    acc = jnp.broadcast_to(b_ref[...], cur.shape)
    for t in range(CONV_WIDTH):
        off = CONV_HALO - half + t
        acc = acc + ext[off:off + tl] * w_ref[t:t + 1, :]
    o_ref[0] = (acc * jax.nn.sigmoid(acc)).astype(o_ref.dtype)


def _conv_silu(proj, conv_w, conv_b):
    b, l, _ = proj.shape
    tl = min(l, CONV_TILE)
    nt = l // tl
    cw = CONV_TILE
    nch = CONV_CH // cw
    c0 = COL_XBC // cw
    hb = tl // CONV_HALO
    nh = l // CONV_HALO
    return pl.pallas_call(
        functools.partial(_conv_kernel, nt=nt),
        out_shape=jax.ShapeDtypeStruct((b, l, CONV_CH), BF16),
        grid=(b, nt, nch),
        in_specs=[pl.BlockSpec((1, CONV_HALO, cw), lambda bi, i, c: (bi, jnp.maximum(i * hb - 1, 0), c0 + c)),
                  pl.BlockSpec((1, tl, cw), lambda bi, i, c: (bi, i, c0 + c)),
                  pl.BlockSpec((1, CONV_HALO, cw), lambda bi, i, c: (bi, jnp.minimum((i + 1) * hb, nh - 1), c0 + c)),
                  pl.BlockSpec((CONV_WIDTH, cw), lambda bi, i, c: (0, c)),
                  pl.BlockSpec((1, cw), lambda bi, i, c: (0, c))],
        out_specs=pl.BlockSpec((1, tl, cw), lambda bi, i, c: (bi, i, c)),
        compiler_params=_cparams(("parallel", "parallel", "parallel")),
        name="conv_silu",
    )(proj, proj, proj, conv_w, conv_b.reshape(1, CONV_CH))


def _ssd_kernel(xs_ref, b_ref, c_ref, dt_ref, dtb_ref, alog_ref, e_ref, y_ref, st_ref, *, rev):
    @pl.when(pl.program_id(1) == 0)
    def _():
        st_ref[...] = jnp.zeros_like(st_ref)

    q = SSD_CHUNK
    off = SSD_HEADS if rev else 0
    z = dt_ref[0].astype(F32) + dtb_ref[...]
    dt = jnp.maximum(z, 0.0) + jnp.log1p(jnp.exp(-jnp.abs(z)))
    da = dt * (-jnp.exp(alog_ref[...]))
    ri = lax.broadcasted_iota(I32, (q, q), 0)
    ci = lax.broadcasted_iota(I32, (q, q), 1)
    if rev:
        mask = ci >= ri
        mask_t = ci <= ri
    else:
        mask = ci <= ri
        mask_t = ci >= ri
    tri = jnp.where(mask, 1.0, 0.0).astype(BF16)
    tri_t = jnp.where(mask_t, 1.0, 0.0).astype(BF16)
    cum = _dot2_left(tri, da)
    cum_t = _dot2(da.T, tri_t)
    tot = cum[0:1, :] if rev else cum[q - 1:q, :]
    e = e_ref[...]
    dt_full = _dot2(dt, e)
    cum_full = _dot2(cum, e)
    tot_full = _dot2(tot, e)
    xdt_f = xs_ref[0].astype(F32) * dt_full
    xdt = xdt_f.astype(BF16)
    xw = (xdt_f * jnp.exp(tot_full - cum_full)).astype(BF16)
    expcum = jnp.exp(cum_full)
    chunk_decay = jnp.exp(tot_full)
    gw = SSD_GROUP_WIDTH
    for g in range(SSD_GROUPS):
        bg = b_ref[0][:, g * SSD_STATE:(g + 1) * SSD_STATE]
        cg = c_ref[0][:, g * SSD_STATE:(g + 1) * SSD_STATE]
        bgt = bg.astype(F32).T.astype(BF16)
        cb = jnp.dot(cg, bgt, preferred_element_type=F32)
        ys = []
        for r in range(SSD_HEADS_PER_GROUP):
            h = g * SSD_HEADS_PER_GROUP + r
            seg = cum[:, off + h:off + h + 1] - cum_t[off + h:off + h + 1, :]
            dec = jnp.exp(jnp.where(mask, seg, NEG_INF))
            m = (cb * dec).astype(BF16)
            ys.append(jnp.dot(m, xdt[:, h * SSD_HEAD_DIM:(h + 1) * SSD_HEAD_DIM],
                              preferred_element_type=F32))
        y_diag = jnp.concatenate(ys, axis=1)
        st = st_ref[g]
        y_off = jnp.dot(cg, st.astype(BF16), preferred_element_type=F32) * expcum[:, g * gw:(g + 1) * gw]
        new = jnp.dot(bgt, xw[:, g * gw:(g + 1) * gw], preferred_element_type=F32)
        st_ref[g] = st * chunk_decay[:, g * gw:(g + 1) * gw] + new
        y_ref[0, :, g * gw:(g + 1) * gw] = (y_diag + y_off).astype(y_ref.dtype)


def _ssd_scan(xbc, proj, dtb_row, alog_row, expand, rev):
    b, l, _ = xbc.shape
    nc = l // SSD_CHUNK
    gs = SSD_GROUPS * SSD_STATE
    cidx = (lambda c: nc - 1 - c) if rev else (lambda c: c)
    return pl.pallas_call(
        functools.partial(_ssd_kernel, rev=rev),
        out_shape=jax.ShapeDtypeStruct((b, l, SSD_INNER), BF16),
        grid=(b, nc),
        in_specs=[pl.BlockSpec((1, SSD_CHUNK, SSD_INNER), lambda bi, c: (bi, cidx(c), 0)),
                  pl.BlockSpec((1, SSD_CHUNK, gs), lambda bi, c: (bi, cidx(c), SSD_INNER // gs)),
                  pl.BlockSpec((1, SSD_CHUNK, gs), lambda bi, c: (bi, cidx(c), SSD_INNER // gs + 1)),
                  pl.BlockSpec((1, SSD_CHUNK, LANES), lambda bi, c: (bi, cidx(c), COL_DT // LANES)),
                  pl.BlockSpec((1, LANES), lambda bi, c: (0, 0)),
                  pl.BlockSpec((1, LANES), lambda bi, c: (0, 0)),
                  pl.BlockSpec((LANES, SSD_INNER), lambda bi, c: (0, 0))],
        out_specs=pl.BlockSpec((1, SSD_CHUNK, SSD_INNER), lambda bi, c: (bi, cidx(c), 0)),
        scratch_shapes=[pltpu.VMEM((SSD_GROUPS, SSD_STATE, SSD_GROUP_WIDTH), F32)],
        compiler_params=_cparams(("parallel", "arbitrary")),
        name="ssd_scan_bwd" if rev else "ssd_scan_fwd",
    )(xbc, xbc, xbc, proj, dtb_row, alog_row, expand)


def _merge_kernel(x_ref, yf_ref, yb_ref, xs_ref, z_ref, ao_ref, ga_ref, gs_ref, g1_ref,
                  dsk_ref, snw_ref, wssd_ref, wattn_ref, wout_ref, o_ref):
    y = yf_ref[0].astype(F32) + yb_ref[0].astype(F32) + xs_ref[0].astype(F32) * dsk_ref[...]
    z = z_ref[0].astype(F32)
    y = y * (z * jax.nn.sigmoid(z))
    parts = []
    for g in range(SSD_GROUPS):
        yg = y[:, g * SSD_GROUP_WIDTH:(g + 1) * SSD_GROUP_WIDTH]
        parts.append(yg * lax.rsqrt(jnp.mean(yg * yg, axis=-1, keepdims=True) + EPS))
    y = jnp.concatenate(parts, axis=1) * snw_ref[...]
    ssd = jnp.dot(y.astype(BF16), wssd_ref[...], preferred_element_type=F32)
    attn = jnp.dot(ao_ref[0], wattn_ref[...], preferred_element_type=F32)
    merged = (jax.nn.sigmoid(ga_ref[0].astype(F32)) * attn
              + jax.nn.sigmoid(gs_ref[0].astype(F32)) * ssd)
    out = jnp.dot(merged.astype(BF16), wout_ref[...], preferred_element_type=F32)
    o_ref[0] = x_ref[0] + g1_ref[0] * out


def _merge(x, yf, yb, xbc, proj, attn_o, modr, dsk_full, snw, w_ssd, w_attn, w_out):
    b, l, _ = x.shape
    tl = min(l, 256)
    tok = lambda w, colb: pl.BlockSpec((1, tl, w), lambda bi, i: (bi, i, colb))
    full = lambda r, c: pl.BlockSpec((r, c), lambda bi, i: (0, 0))
    return pl.pallas_call(
        _merge_kernel,
        out_shape=jax.ShapeDtypeStruct((b, l, D_MODEL), F32),
        grid=(b, l // tl),
        in_specs=[tok(D_MODEL, 0), tok(SSD_INNER, 0), tok(SSD_INNER, 0), tok(SSD_INNER, 0),
                  tok(SSD_INNER, COL_Z // SSD_INNER), tok(ATTN_WIDTH, 0),
                  tok(D_MODEL, COL_GA // D_MODEL), tok(D_MODEL, COL_GS // D_MODEL),
                  pl.BlockSpec((1, 1, D_MODEL), lambda bi, i: (bi * N_MOD + 2, 0, 0)),
                  full(1, SSD_INNER), full(1, SSD_INNER),
                  full(SSD_INNER, D_MODEL), full(ATTN_WIDTH, D_MODEL), full(D_MODEL, D_MODEL)],
        out_specs=tok(D_MODEL, 0),
        compiler_params=_cparams(("parallel", "parallel")),
        name="merge_out_proj",
    )(x, yf, yb, xbc, proj, attn_o, proj, proj, modr, dsk_full, snw, w_ssd, w_attn, w_out)


def _perm_src(j):
    return (j % 2) * (D_MODEL // 2) + (j // 2) * LANES


def _topk_rows(s, k):
    rowf = lax.broadcasted_iota(I32, s.shape, 0).astype(F32)
    vals, ids = [], []
    for _ in range(k):
        m = jnp.max(s, axis=0, keepdims=True)
        am = jnp.min(jnp.where(s == m, rowf, float(s.shape[0])), axis=0, keepdims=True)
        vals.append(m)
        ids.append(am)
        s = jnp.where(rowf == am, -jnp.inf, s)
    return vals, ids


def _candidate_pieces():
    pieces = []
    for a in range(PEER_TOPK):
        nb = PEER_TOPK // (a + 1)
        if nb >= SUBLANES // 2:
            for b0 in range(0, nb, SUBLANES):
                pieces.append((a, 1, b0, SUBLANES))
    a_done = max(p[0] for p in pieces) + 1
    for b in range(PEER_TOPK):
        na = PEER_TOPK // (b + 1)
        if na > a_done:
            for a0 in range(0, na, SUBLANES):
                pieces.append((a0, SUBLANES, b, 1))
    return pieces, a_done


def _route_kernel(x_ref, sh_ref, sc_ref, nw_ref, wq_ref, keys_ref, hp_ref, idx_ref, gate_ref, cnt_ref):
    h = _norm_mod(x_ref[0], nw_ref[...], sc_ref[0], sh_ref[0])
    for j in range(D_MODEL // LANES):
        hp_ref[0, :, j * LANES:(j + 1) * LANES] = h[:, _perm_src(j):_perm_src(j) + LANES]
    qall = jnp.dot(h.astype(BF16), wq_ref[...], preferred_element_type=F32)
    ntok = qall.shape[0]
    tops, topi = [], []
    for c in range(2 * PEER_HEADS):
        qc = qall[:, c * PEER_HALF:(c + 1) * PEER_HALF].astype(BF16)
        s = lax.dot_general(keys_ref[c], qc, (((1,), (1,)), ((), ())), preferred_element_type=F32)
        vals, ids = _topk_rows(s, PEER_TOPK)
        tops.append(vals)
        topi.append(ids)
    pieces, a_done = _candidate_pieces()
    sub = lax.broadcasted_iota(I32, (SUBLANES, ntok), 0)
    subf = sub.astype(F32)
    pos_parts, drop_parts = [], []
    for a0, na, b0, nb in pieces:
        if na == 1:
            pos_parts.append(subf + float(a0 * PEER_TOPK + b0))
            keep = sub + b0 < PEER_TOPK // (a0 + 1)
        else:
            pos_parts.append(subf * float(PEER_TOPK) + float(a0 * PEER_TOPK + b0))
            keep = jnp.where(sub + a0 >= a_done, sub + a0, PEER_TOPK) < PEER_TOPK // (b0 + 1)
        drop_parts.append(jnp.where(keep, 0.0, -jnp.inf))
    pos = jnp.concatenate(pos_parts, axis=0)
    drop = jnp.concatenate(drop_parts, axis=0)
    idx_rows, gate_rows = [], []
    for hd in range(PEER_HEADS):
        s0 = jnp.concatenate(tops[2 * hd], axis=0)
        i0 = jnp.concatenate(topi[2 * hd], axis=0)
        s1 = jnp.concatenate(tops[2 * hd + 1], axis=0)
        i1 = jnp.concatenate(topi[2 * hd + 1], axis=0)
        cparts, iparts = [], []
        for a0, na, b0, nb in pieces:
            if na == 1:
                cparts.append(tops[2 * hd][a0] + s1[b0:b0 + nb])
                iparts.append(topi[2 * hd][a0] * float(N_KEYS) + i1[b0:b0 + nb])
            else:
                cparts.append(s0[a0:a0 + na] + tops[2 * hd + 1][b0])
                iparts.append(i0[a0:a0 + na] * float(N_KEYS) + topi[2 * hd + 1][b0])
        cand = jnp.concatenate(cparts, axis=0) + drop
        cidx = jnp.concatenate(iparts, axis=0)
        best = []
        for _ in range(PEER_TOPK):
            m = jnp.max(cand, axis=0, keepdims=True)
            first = jnp.min(jnp.where(cand == m, pos, float(PEER_TOPK * PEER_TOPK)), axis=0, keepdims=True)
            sel = pos == first
            idx_rows.append(jnp.max(jnp.where(sel, cidx, -1.0), axis=0, keepdims=True))
            best.append(m)
            cand = jnp.where(sel, -jnp.inf, cand)
        bs = jnp.concatenate(best, axis=0)
        p = jnp.exp(bs - best[0])
        gate_rows.append(p / jnp.sum(p, axis=0, keepdims=True))
    idx_t = jnp.concatenate(idx_rows, axis=0)
    gate_t = jnp.concatenate(gate_rows, axis=0)
    idx_t, gate_t, n_low = _partition_by_half(idx_t, gate_t)
    idx_ref[0] = (idx_t * float(SLAB_ROWS)).T.astype(I32)
    gate_ref[0] = gate_t.T
    cnt_ref[0, 0] = n_low.astype(I32)


def _compact_rows(vals, shift, valid, up):
    n = valid.shape[0]
    for bit in range(n.bit_length() - 1):
        step = 1 << bit
        roll = (lambda a: pltpu.roll(a, (n - step) if up else step, axis=0))
        move = valid * ((shift >> bit) & 1)
        bring = roll(move) != 0
        vals = [jnp.where(bring, roll(v), v) for v in vals]
        shift = jnp.where(bring, roll(shift), shift)
        valid = jnp.where(bring, 1, valid - move)
    return vals


def _partition_by_half(idx_t, gate_t):
    n = idx_t.shape[0]
    half = float(N_EXPERTS // 2)
    high = idx_t >= half
    local = jnp.where(high, idx_t - half, idx_t)
    is_high = jnp.where(high, 1.0, 0.0).astype(BF16)
    is_low = jnp.where(high, 0.0, 1.0).astype(BF16)
    ri = lax.broadcasted_iota(I32, (n, n), 0)
    ci = lax.broadcasted_iota(I32, (n, n), 1)
    before = jnp.where(ci < ri, 1.0, 0.0).astype(BF16)
    after = jnp.where(ci > ri, 1.0, 0.0).astype(BF16)
    high_before = jnp.dot(before, is_high, preferred_element_type=F32).astype(I32)
    low_after = jnp.dot(after, is_low, preferred_element_type=F32).astype(I32)
    n_low = jnp.sum(is_low.astype(F32), axis=0, keepdims=True)
    lo_idx, lo_gate = _compact_rows([local, gate_t], high_before, jnp.where(high, 0, 1), up=True)
    hi_idx, hi_gate = _compact_rows([local, gate_t], low_after, jnp.where(high, 1, 0), up=False)
    first = lax.broadcasted_iota(I32, idx_t.shape, 0).astype(F32) < n_low
    return jnp.where(first, lo_idx, hi_idx), jnp.where(first, lo_gate, hi_gate), n_low


def _route(x1, modr, norm_w, wq, keys):
    b, l, _ = x1.shape
    tl = min(l, 256)
    tok = lambda w: pl.BlockSpec((1, tl, w), lambda bi, i: (bi, i, 0))
    return pl.pallas_call(
        _route_kernel,
        out_shape=(jax.ShapeDtypeStruct((b, l, D_MODEL), F32),
                   jax.ShapeDtypeStruct((b, l, N_SEL), I32),
                   jax.ShapeDtypeStruct((b, l, N_SEL), F32),
                   jax.ShapeDtypeStruct((b, l // tl, 1, tl), I32)),
        grid=(b, l // tl),
        in_specs=[tok(D_MODEL),
                  pl.BlockSpec((1, 1, D_MODEL), lambda bi, i: (bi * N_MOD + 3, 0, 0)),
                  pl.BlockSpec((1, 1, D_MODEL), lambda bi, i: (bi * N_MOD + 4, 0, 0)),
                  pl.BlockSpec((1, D_MODEL), lambda bi, i: (0, 0)),
                  pl.BlockSpec((D_MODEL, PEER_HEADS * PEER_KEY_DIM), lambda bi, i: (0, 0)),
                  pl.BlockSpec((2 * PEER_HEADS, N_KEYS, PEER_HALF), lambda bi, i: (0, 0, 0))],
        out_specs=(tok(D_MODEL), tok(N_SEL), tok(N_SEL),
                   pl.BlockSpec((1, 1, 1, tl), lambda bi, i: (bi, i, 0, 0))),
        compiler_params=_cparams(("parallel", "parallel")),
        name="peer_route",
    )(x1, modr, modr, norm_w, wq, keys)


def _pack_rows(tab):
    n = tab.shape[0]
    bits = lax.bitcast_convert_type(tab.astype(BF16), jnp.uint16).astype(jnp.uint32)
    lo = bits[:, :D_MODEL // 2]
    hi = bits[:, D_MODEL // 2:]
    words = lax.bitcast_convert_type(lo | (hi << 16), I32)
    return words.reshape(n, ROW_WORDS, LANES)


def _pack_tables(tab_u, tab_v):
    slabs = jnp.stack([_pack_rows(tab_u), _pack_rows(tab_v)], axis=2).reshape(N_EXPERTS, SLAB_ROWS, LANES)
    half = N_EXPERTS // 2
    return (slabs[:half].reshape(half * SLAB_ROWS, LANES), slabs[half:].reshape(half * SLAB_ROWS, LANES))


def _gelu_tanh(x):
    return 0.5 * x * (1.0 + jnp.tanh(math.sqrt(2.0 / math.pi) * (x + 0.044715 * (x * x * x))))


def _peer_first_kernel(idx_ref, cnt_ref, x_ref, gate_ref, tab_ref, dmask_ref, gsum_ref, rep_ref, ones_ref,
                       out_ref, tile0, tile1, wbuf0, wbuf1, **static):
    _peer_pass_body(idx_ref, cnt_ref, x_ref, gate_ref, None, tab_ref, dmask_ref, gsum_ref, rep_ref, ones_ref,
                    out_ref, (tile0, tile1), (wbuf0, wbuf1), **static)


def _peer_next_kernel(idx_ref, cnt_ref, x_ref, gate_ref, prev_ref, tab_ref, dmask_ref, gsum_ref, rep_ref, ones_ref,
                      out_ref, tile0, tile1, wbuf0, wbuf1, **static):
    _peer_pass_body(idx_ref, cnt_ref, x_ref, gate_ref, prev_ref, tab_ref, dmask_ref, gsum_ref, rep_ref, ones_ref,
                    out_ref, (tile0, tile1), (wbuf0, wbuf1), **static)


def _peer_pass_body(idx_ref, cnt_ref, x_ref, gate_ref, prev_ref, tab_ref, dmask_ref, gsum_ref, rep_ref, ones_ref,
                    out_ref, tiles, wbufs, *, high, s_lo, s_hi, rare):
    @pl.when(pl.program_id(0) == 0)
    def _():
        for tile in tiles:
            tile[...] = jnp.zeros_like(tile)

    lane = lax.broadcasted_iota(I32, (1, N_SEL), 1)
    in_range = jnp.logical_and(lane >= s_lo * PEER_CHUNK, lane < s_hi * PEER_CHUNK)
    crows = PEER_CHUNK * SLAB_ROWS
    nrows = (s_hi - s_lo) * PEER_CHUNK * ROW_WORDS

    def gather_chunk(tile, tt, base, c):
        k0 = base + c * PEER_CHUNK
        r0 = (c - s_lo) * crows
        r0 = r0 if isinstance(c, int) else pl.multiple_of(r0, crows)
        for kk in range(PEER_CHUNK):
            r = pl.multiple_of(idx_ref[k0 + kk], SLAB_ROWS)
            tile[tt, pl.ds(r0 + kk * SLAB_ROWS, SLAB_ROWS), :] = tab_ref[pl.ds(r, SLAB_ROWS), :]

    def valid_row(n_low):
        mine = (lane >= n_low) if high else (lane < n_low)
        return jnp.where(jnp.logical_and(mine, in_range), 1.0, 0.0)

    def needed_chunks(n_low):
        if high:
            return jnp.maximum(n_low // PEER_CHUNK, s_lo), s_hi
        return s_lo, jnp.minimum((n_low + PEER_CHUNK - 1) // PEER_CHUNK, s_hi)

    def token0(g):
        return g * PEER_TOK_GROUP if isinstance(g, int) else pl.multiple_of(g * PEER_TOK_GROUP, PEER_TOK_GROUP)

    def rows8(t):
        return pl.ds(t * SUBLANES if isinstance(t, int) else pl.multiple_of(t * SUBLANES, SUBLANES), SUBLANES)

    def first_stage(g, par):
        t0 = token0(g)
        tile, wbuf = tiles[par], wbufs[par]
        counts = [cnt_ref[t0 + tt] for tt in range(PEER_TOK_GROUP)]
        bases = [(t0 + tt) * N_SEL for tt in range(PEER_TOK_GROUP)]
        vs, valid = [], []

        def token(tt):
            valid.append(valid_row(counts[tt]))
            if rare:
                def chunk(c, carry):
                    gather_chunk(tile, tt, bases[tt], c)
                    return carry
                lax.fori_loop(*needed_chunks(counts[tt]), chunk, 0)
            else:
                for c in range(s_lo, s_hi):
                    gather_chunk(tile, tt, bases[tt], c)
            rows = pltpu.bitcast(tile[tt, pl.ds(0, nrows, stride=2), :], BF16)
            x8 = x_ref[rows8(t0 + tt), :].astype(BF16)
            r = lax.dot_general(x8, rows, (((1,), (1,)), ((), ())), preferred_element_type=F32)
            vs.append(jnp.sum(r * dmask_ref[...], axis=0, keepdims=True))

        def finish():
            act = _dot2(jnp.concatenate(vs, axis=0), gsum_ref[...])
            wbuf[...] = gate_ref[pl.ds(t0, PEER_TOK_GROUP), :] * _gelu_tanh(act) * jnp.concatenate(valid, axis=0)

        return token, finish

    def second_stage(g, par):
        t0 = token0(g)
        tile = tiles[par]
        w = wbufs[par][...]
        half_rows = SUBLANES // 2

        def token(tt):
            vt = tile[tt, pl.ds(1, nrows, stride=2), :]
            v_lo = pltpu.bitcast(vt << 16, F32)
            v_hi = pltpu.bitcast(vt & jnp.int32(-65536), F32)
            wrows = jnp.dot((rep_ref[...] * w[tt:tt + 1, :]).astype(BF16), ones_ref[...],
                            preferred_element_type=F32)
            acc_lo = jnp.sum((v_lo * wrows).reshape(nrows // SUBLANES, SUBLANES, LANES), axis=0)
            acc_hi = jnp.sum((v_hi * wrows).reshape(nrows // SUBLANES, SUBLANES, LANES), axis=0)
            o8 = jnp.concatenate([acc_lo[:half_rows] + acc_lo[half_rows:],
                                  acc_hi[:half_rows] + acc_hi[half_rows:]], axis=0)
            sl = rows8(t0 + tt)
            out_ref[sl, :] = o8 if prev_ref is None else prev_ref[sl, :] + o8

        return token

    def step(first, second):
        tok1, fin1 = first_stage(*first) if first else (None, None)
        tok2 = second_stage(*second) if second else None
        for tt in range(PEER_TOK_GROUP):
            if tok1:
                tok1(tt)
            if tok2:
                tok2(tt)
        if fin1:
            fin1()

    ngroups = PEER_BLOCK // PEER_TOK_GROUP
    if rare:
        out_ref[...] = prev_ref[...]

        def group(g, carry):
            t0 = token0(g)
            counts = [cnt_ref[t0 + tt] for tt in range(PEER_TOK_GROUP)]
            if high:
                needed = functools.reduce(jnp.minimum, counts) < s_hi * PEER_CHUNK
            else:
                needed = functools.reduce(jnp.maximum, counts) > s_lo * PEER_CHUNK

            @pl.when(needed)
            def _():
                step((g, 0), None)
                step(None, (g, 0))
            return carry

        lax.fori_loop(0, ngroups, group, 0)
    else:
        step((0, 0), None)

        def pair(j, carry):
            step((2 * j + 1, 1), (2 * j, 0))
            step((2 * j + 2, 0), (2 * j + 1, 1))
            return carry

        lax.fori_loop(0, ngroups // 2 - 1, pair, 0)
        step((ngroups - 1, 1), (ngroups - 2, 0))
        step(None, (ngroups - 1, 1))


def _peer_consts(s_lo, s_hi):
    ncol = (s_hi - s_lo) * PEER_CHUNK * SUBLANES
    c = jnp.arange(ncol)
    dmask = (c[None, :] % SUBLANES == jnp.arange(SUBLANES)[:, None]).astype(F32)
    sel = c[:, None] // SUBLANES + s_lo * PEER_CHUNK
    gsum = (sel == jnp.arange(N_SEL)[None, :]).astype(BF16)
    row_sel = jnp.arange(ncol // 2)[:, None] // ROW_WORDS + s_lo * PEER_CHUNK
    rep = (row_sel == jnp.arange(N_SEL)[None, :]).astype(F32)
    return dmask, gsum, rep, jnp.ones((N_SEL, LANES), BF16)


def _peer_pass(idx_flat, cnt_flat, x8, gates, prev, tab, *, high, s_lo, s_hi, rare):
    t = gates.shape[0]
    dmask, gsum, rep, ones = _peer_consts(s_lo, s_hi)
    static = dict(high=high, s_lo=s_lo, s_hi=s_hi, rare=rare)
    const = lambda a: pl.BlockSpec(a.shape, lambda i: (0, 0))
    row_spec = pl.BlockSpec((PEER_BLOCK * SUBLANES, LANES), lambda i: (i, 0))
    tok_spec = pl.BlockSpec((PEER_BLOCK, N_SEL), lambda i: (i, 0))
    in_specs = [pl.BlockSpec((PEER_BLOCK * N_SEL,), lambda i: (i,), memory_space=pltpu.SMEM),
                pl.BlockSpec((PEER_BLOCK,), lambda i: (i,), memory_space=pltpu.SMEM),
                row_spec, tok_spec]
    args = [idx_flat, cnt_flat, x8, gates]
    if prev is not None:
        in_specs.append(row_spec)
        args.append(prev)
    in_specs += [pl.BlockSpec(memory_space=pltpu.VMEM), const(dmask), const(gsum), const(rep), const(ones)]
    args += [tab, dmask, gsum, rep, ones]
    tile = pltpu.VMEM((PEER_TOK_GROUP, (s_hi - s_lo) * PEER_CHUNK * SLAB_ROWS, LANES), I32)
    wbuf = pltpu.VMEM((PEER_TOK_GROUP, N_SEL), F32)
    return pl.pallas_call(
        functools.partial(_peer_first_kernel if prev is None else _peer_next_kernel, **static),
        out_shape=jax.ShapeDtypeStruct((t * SUBLANES, LANES), F32),
        grid=(t // PEER_BLOCK,),
        in_specs=in_specs,
        out_specs=row_spec,
        scratch_shapes=[tile, tile, wbuf, wbuf],
        compiler_params=_cparams(("arbitrary",)),
        name="peer_%s_%s" % ("high" if high else "low", "rare" if rare else "common"),
    )(*args)


def _final_kernel(x_ref, o_ref, g2_ref, y_ref):
    y_ref[0] = x_ref[0] + g2_ref[0] * o_ref[0]


def _final_residual(x1, outp, modr):
    b, l, _ = x1.shape
    tl = min(l, 512)
    tok = pl.BlockSpec((1, tl, D_MODEL), lambda bi, i: (bi, i, 0))
    return pl.pallas_call(
        _final_kernel,
        out_shape=jax.ShapeDtypeStruct((b, l, D_MODEL), F32),
        grid=(b, l // tl),
        in_specs=[tok, tok, pl.BlockSpec((1, 1, D_MODEL), lambda bi, i: (bi * N_MOD + 5, 0, 0))],
        out_specs=tok,
        compiler_params=_cparams(("parallel", "parallel")),
        name="peer_residual",
    )(x1, outp, modr)


def _prepare(rel_bias, ada_w, ada_b, norm1_w, norm2_w, w_in, q_norm_w, k_norm_w, attn_sink, conv_w,
             conv_b, a_log, dt_bias, d_skip, ssd_norm_w, w_attn_br, w_ssd_br, w_out, peer_wq,
             peer_keys, peer_u, peer_v):
    lyr = 0
    w = w_in[lyr]
    o = 0
    parts = {}
    for name, width in (("q", ATTN_WIDTH), ("k", KV_WIDTH), ("v", KV_WIDTH), ("z", SSD_INNER),
                        ("xbc", CONV_CH), ("dt", 2 * SSD_HEADS), ("ga", D_MODEL), ("gs", D_MODEL)):
        parts[name] = w[:, o:o + width]
        o += width
    pad = jnp.zeros((D_MODEL, PROJ_W - (COL_DT + 2 * SSD_HEADS)), w.dtype)
    w_all = jnp.concatenate([parts["z"], parts["q"], parts["ga"], parts["gs"], parts["xbc"],
                             parts["k"], parts["v"], parts["dt"], pad], axis=1).astype(BF16)
    lane_pad = LANES - 2 * SSD_HEADS
    expand = (jnp.arange(SSD_INNER)[None, :] // SSD_HEAD_DIM == jnp.arange(LANES)[:, None])
    tab_low, tab_high = _pack_tables(peer_u[lyr], peer_v[lyr])
    return dict(
        ada_w=ada_w[lyr], ada_b=ada_b[lyr],
        norm1_w=norm1_w[lyr].reshape(1, D_MODEL), norm2_w=norm2_w[lyr].reshape(1, D_MODEL),
        w_all=w_all,
        bias_tab=_bias_table(rel_bias), sink=attn_sink[lyr].astype(F32),
        qw=(jnp.tile(q_norm_w[lyr], N_Q_HEADS) * HEAD_DIM ** -0.5).reshape(1, ATTN_WIDTH),
        kw=jnp.tile(k_norm_w[lyr], N_KV_HEADS).reshape(1, KV_WIDTH),
        conv_w=conv_w[lyr], conv_b=conv_b[lyr],
        dtb_row=jnp.pad(dt_bias[lyr].astype(F32).reshape(1, -1), ((0, 0), (0, lane_pad))),
        alog_row=jnp.pad(a_log[lyr].astype(F32).reshape(1, -1), ((0, 0), (0, lane_pad))),
        expand_fwd=expand.astype(BF16),
        expand_bwd=jnp.roll(expand, SSD_HEADS, axis=0).astype(BF16),
        dsk_full=jnp.repeat(d_skip[lyr], SSD_HEAD_DIM).reshape(1, SSD_INNER),
        snw=ssd_norm_w[lyr].reshape(1, SSD_INNER),
        w_ssd=w_ssd_br[lyr].astype(BF16), w_attn=w_attn_br[lyr].astype(BF16), w_out=w_out[lyr].astype(BF16),
        wq=peer_wq[lyr].astype(BF16),
        keys=peer_keys[lyr].reshape(2 * PEER_HEADS, N_KEYS, PEER_HALF).astype(BF16),
        tab_low=tab_low, tab_high=tab_high,
    )


def _token_mixer_stage(x, modr, p):
    proj = _in_projection(x, modr, p["norm1_w"], p["w_all"])
    attn_o = _attention(proj, p["bias_tab"], p["sink"], p["qw"], p["kw"])
    xbc = _conv_silu(proj, p["conv_w"], p["conv_b"])
    yf = _ssd_scan(xbc, proj, p["dtb_row"], p["alog_row"], p["expand_fwd"], rev=False)
    yb = _ssd_scan(xbc, proj, p["dtb_row"], p["alog_row"], p["expand_bwd"], rev=True)
    return _merge(x, yf, yb, xbc, proj, attn_o, modr, p["dsk_full"], p["snw"],
                  p["w_ssd"], p["w_attn"], p["w_out"])


def _peer_stage(x1, modr, p):
    b, l, _ = x1.shape
    t = b * l
    hp, idx, gates, cnt = _route(x1, modr, p["norm2_w"], p["wq"], p["keys"])
    idx_flat = idx.reshape(t * N_SEL)
    cnt_flat = cnt.reshape(t)
    gates = gates.reshape(t, N_SEL)
    x8 = hp.reshape(t * SUBLANES, LANES)
    nchunk = N_SEL // PEER_CHUNK
    split = nchunk - PEER_STATIC_CHUNKS
    args = (idx_flat, cnt_flat, x8, gates)
    out = _peer_pass(*args, None, p["tab_low"], high=False, s_lo=0, s_hi=PEER_STATIC_CHUNKS, rare=False)
    out = _peer_pass(*args, out, p["tab_high"], high=True, s_lo=split, s_hi=nchunk, rare=False)
    out = _peer_pass(*args, out, p["tab_low"], high=False, s_lo=PEER_STATIC_CHUNKS, s_hi=nchunk, rare=True)
    out = _peer_pass(*args, out, p["tab_high"], high=True, s_lo=0, s_hi=split, rare=True)
    return _final_residual(x1, out.reshape(b, l, D_MODEL), modr)


def _encoder(x, c, p):
    nb = c.shape[0]
    modr = _modulation(c, p["ada_w"], p["ada_b"]).reshape(nb * N_MOD, 1, D_MODEL)
    x1 = _token_mixer_stage(x, modr, p)
    return _peer_stage(x1, modr, p)


def kernel(x_prompt, x_sample, c_prompt, c_sample, rel_bias, ada_w, ada_b, norm1_w, norm2_w, w_in,
           q_norm_w, k_norm_w, attn_sink, conv_w, conv_b, a_log, dt_bias, d_skip, ssd_norm_w,
           w_attn_br, w_ssd_br, w_out, peer_wq, peer_keys, peer_u, peer_v):
    p = _prepare(rel_bias, ada_w, ada_b, norm1_w, norm2_w, w_in, q_norm_w, k_norm_w, attn_sink,
                 conv_w, conv_b, a_log, dt_bias, d_skip, ssd_norm_w, w_attn_br, w_ssd_br, w_out,
                 peer_wq, peer_keys, peer_u, peer_v)
    return (_encoder(x_prompt, c_prompt, p), _encoder(x_sample, c_sample, p))
```

```python
import functools
import math

import jax
import jax.numpy as jnp
from jax import lax
from jax.experimental import pallas as pl
from jax.experimental.pallas import tpu as pltpu

F32 = jnp.float32
BF16 = jnp.bfloat16
I32 = jnp.int32

D_MODEL = 1024
HEAD_DIM = 64
N_Q_HEADS = 16
N_KV_HEADS = 4
Q_PER_KV = N_Q_HEADS // N_KV_HEADS
ATTN_WIDTH = N_Q_HEADS * HEAD_DIM
KV_WIDTH = N_KV_HEADS * HEAD_DIM
WINDOW = 128
BAND_BLOCK = 128
NUM_BUCKETS = 32
MAX_DISTANCE = 128
NEG_INF = -1e30

SSD_INNER = 2 * D_MODEL
SSD_HEAD_DIM = 64
SSD_HEADS = SSD_INNER // SSD_HEAD_DIM
SSD_GROUPS = 4
SSD_HEADS_PER_GROUP = SSD_HEADS // SSD_GROUPS
SSD_STATE = 128
SSD_CHUNK = 128
SSD_GROUP_WIDTH = SSD_INNER // SSD_GROUPS
CONV_WIDTH = 5
CONV_CH = SSD_INNER + 2 * SSD_GROUPS * SSD_STATE

PEER_HEADS = 8
PEER_KEY_DIM = 256
PEER_HALF = PEER_KEY_DIM // 2
N_KEYS = 128
N_EXPERTS = N_KEYS * N_KEYS
PEER_TOPK = 16
N_SEL = PEER_HEADS * PEER_TOPK

N_MOD = 6
EPS = 1e-6

COL_Z = 0
COL_Q = 2048
COL_GA = 3072
COL_GS = 4096
COL_XBC = 5120
COL_K = 8192
COL_V = 8448
COL_DT = 8704
PROJ_W = 9216

LANES = 128
SUBLANES = 8
VMEM_LIMIT = 56 * 1024 * 1024
ROW_WORDS = D_MODEL // (2 * LANES)
SLAB_ROWS = 2 * ROW_WORDS
PEER_TOK_GROUP = 8
PEER_BLOCK = 128
PEER_CHUNK = 16
PEER_STATIC_CHUNKS = 5


def _cparams(sem):
    return pltpu.CompilerParams(dimension_semantics=sem, vmem_limit_bytes=VMEM_LIMIT)


def _split_bf16(v):
    hi = v.astype(BF16)
    lo = (v - hi.astype(F32)).astype(BF16)
    return hi, lo


def _dot2(v, m_bf16):
    hi, lo = _split_bf16(v)
    return (jnp.dot(hi, m_bf16, preferred_element_type=F32)
            + jnp.dot(lo, m_bf16, preferred_element_type=F32))


def _dot2_left(m_bf16, v):
    hi, lo = _split_bf16(v)
    return (jnp.dot(m_bf16, hi, preferred_element_type=F32)
            + jnp.dot(m_bf16, lo, preferred_element_type=F32))


def _mod_kernel(c_ref, w_ref, b_ref, o_ref):
    c = c_ref[...]
    sc = c * jax.nn.sigmoid(c)
    o_ref[...] = jnp.dot(sc, w_ref[...], preferred_element_type=F32,
                         precision=lax.Precision.HIGHEST) + b_ref[...]


def _modulation(c, ada_w, ada_b):
    nb = c.shape[0]
    n = ada_w.shape[1]
    tn = 1024
    return pl.pallas_call(
        _mod_kernel,
        out_shape=jax.ShapeDtypeStruct((nb, n), F32),
        grid=(n // tn,),
        in_specs=[pl.BlockSpec((nb, D_MODEL), lambda j: (0, 0)),
                  pl.BlockSpec((D_MODEL, tn), lambda j: (0, j)),
                  pl.BlockSpec((1, tn), lambda j: (0, j))],
        out_specs=pl.BlockSpec((nb, tn), lambda j: (0, j)),
        compiler_params=_cparams(("arbitrary",)),
        name="adaln_mod",
    )(c, ada_w, ada_b.reshape(1, n))


def _norm_mod(x, nw, sc, sh):
    ms = jnp.mean(x * x, axis=-1, keepdims=True)
    h = x * lax.rsqrt(ms + EPS) * nw
    return h * (1.0 + sc) + sh


def _inproj_kernel(x_ref, sh_ref, sc_ref, nw_ref, w_ref, o_ref, h_scr):
    @pl.when(pl.program_id(2) == 0)
    def _():
        h = _norm_mod(x_ref[0], nw_ref[...], sc_ref[0], sh_ref[0])
        h_scr[...] = h.astype(BF16)

    o_ref[0] = jnp.dot(h_scr[...], w_ref[...], preferred_element_type=F32).astype(o_ref.dtype)


def _in_projection(x, modr, norm_w, w_all):
    b, l, _ = x.shape
    tl = min(l, 1024)
    tn = 512
    return pl.pallas_call(
        _inproj_kernel,
        out_shape=jax.ShapeDtypeStruct((b, l, PROJ_W), BF16),
        grid=(b, l // tl, PROJ_W // tn),
        in_specs=[pl.BlockSpec((1, tl, D_MODEL), lambda bi, i, j: (bi, i, 0)),
                  pl.BlockSpec((1, 1, D_MODEL), lambda bi, i, j: (bi * N_MOD + 0, 0, 0)),
                  pl.BlockSpec((1, 1, D_MODEL), lambda bi, i, j: (bi * N_MOD + 1, 0, 0)),
                  pl.BlockSpec((1, D_MODEL), lambda bi, i, j: (0, 0)),
                  pl.BlockSpec((D_MODEL, tn), lambda bi, i, j: (0, j))],
        out_specs=pl.BlockSpec((1, tl, tn), lambda bi, i, j: (bi, i, j)),
        scratch_shapes=[pltpu.VMEM((tl, D_MODEL), BF16)],
        compiler_params=_cparams(("parallel", "parallel", "arbitrary")),
        name="in_projection",
    )(x, modr, modr, norm_w, w_all)


def _head_sums(n_heads):
    c = jnp.arange(n_heads * HEAD_DIM)
    hsum = (c[:, None] // HEAD_DIM == jnp.arange(LANES)[None, :]).astype(BF16)
    return hsum, hsum.T


def _qk_norm(t, hsum, hexp, w_full):
    ssq = _dot2(t * t, hsum)
    inv = lax.rsqrt(ssq * (1.0 / HEAD_DIM) + EPS)
    return t * _dot2(inv, hexp) * w_full


def _attn_kernel(sink_ref, q_ref, kp_ref, kc_ref, kn_ref, vp_ref, vc_ref, vn_ref, bias_ref,
                 qw_ref, kw_ref, qsum_ref, qexp_ref, ksum_ref, kexp_ref, o_ref):
    q = _qk_norm(q_ref[0].astype(F32), qsum_ref[...], qexp_ref[...], qw_ref[...]).astype(BF16)
    k = jnp.concatenate([kp_ref[0], kc_ref[0], kn_ref[0]], axis=0).astype(F32)
    k = _qk_norm(k, ksum_ref[...], kexp_ref[...], kw_ref[...])
    v = jnp.concatenate([vp_ref[0], vc_ref[0], vn_ref[0]], axis=0).astype(F32)
    low = lax.broadcasted_iota(I32, (1, LANES), 1) < HEAD_DIM
    nt = (((1,), (1,)), ((), ()))
    for c in range(N_KV_HEADS // 2):
        kslab = k[:, c * LANES:(c + 1) * LANES]
        vslab = v[:, c * LANES:(c + 1) * LANES]
        kroll = pltpu.roll(kslab, HEAD_DIM, axis=1)
        vroll = pltpu.roll(vslab, HEAD_DIM, axis=1)
        for e in range(2):
            h = 2 * c + e
            k_lo = jnp.where(low, kroll if e else kslab, 0.0).astype(BF16)
            k_hi = jnp.where(low, 0.0, kslab if e else kroll).astype(BF16)
            v_lo = jnp.where(low, vroll if e else vslab, 0.0).astype(BF16)
            v_hi = jnp.where(low, 0.0, vslab if e else vroll).astype(BF16)
            for pair in range(Q_PER_KV // 2):
                slab = h * (Q_PER_KV // 2) + pair
                qs = q[:, slab * LANES:(slab + 1) * LANES]
                acc = None
                for par in range(2):
                    hd = 2 * slab + par
                    s = lax.dot_general(qs, k_hi if par else k_lo, nt, preferred_element_type=F32)
                    s = s + bias_ref[0, hd]
                    sk = sink_ref[hd]
                    m = jnp.maximum(jnp.max(s, axis=-1, keepdims=True), sk)
                    p = jnp.exp(s - m)
                    denom = jnp.sum(p, axis=-1, keepdims=True) + jnp.exp(sk - m)
                    pv = jnp.dot(p.astype(BF16), v_hi if par else v_lo, preferred_element_type=F32) / denom
                    acc = pv if acc is None else acc + pv
                o_ref[0, :, slab * LANES:(slab + 1) * LANES] = acc.astype(o_ref.dtype)


def _attention(proj, bias_tab, sink, qw_full, kw_full):
    b, l, _ = proj.shape
    nb = l // BAND_BLOCK
    assert nb >= 2
    kcol = COL_K // KV_WIDTH
    vcol = COL_V // KV_WIDTH
    qsum, qexp = _head_sums(N_Q_HEADS)
    ksum, kexp = _head_sums(N_KV_HEADS)

    def prev(i):
        return jnp.maximum(i - 1, 0)

    def nxt(i):
        return jnp.minimum(i + 1, nb - 1)

    def edge(i):
        return jnp.where(i == 0, 0, jnp.where(i == nb - 1, 2, 1))

    kv = lambda colb, f: pl.BlockSpec((1, BAND_BLOCK, KV_WIDTH), lambda bi, i: (bi, f(i), colb))
    same = lambda i: i
    const = lambda a: pl.BlockSpec(a.shape, lambda bi, i: (0,) * a.ndim)
    return pl.pallas_call(
        _attn_kernel,
        out_shape=jax.ShapeDtypeStruct((b, l, ATTN_WIDTH), BF16),
        grid=(b, nb),
        in_specs=[pl.BlockSpec(memory_space=pltpu.SMEM),
                  pl.BlockSpec((1, BAND_BLOCK, ATTN_WIDTH), lambda bi, i: (bi, i, COL_Q // ATTN_WIDTH)),
                  kv(kcol, prev), kv(kcol, same), kv(kcol, nxt),
                  kv(vcol, prev), kv(vcol, same), kv(vcol, nxt),
                  pl.BlockSpec((1, N_Q_HEADS, BAND_BLOCK, 3 * BAND_BLOCK), lambda bi, i: (edge(i), 0, 0, 0)),
                  const(qw_full), const(kw_full), const(qsum), const(qexp), const(ksum), const(kexp)],
        out_specs=pl.BlockSpec((1, BAND_BLOCK, ATTN_WIDTH), lambda bi, i: (bi, i, 0)),
        compiler_params=_cparams(("parallel", "parallel")),
        name="window_attention",
    )(sink, proj, proj, proj, proj, proj, proj, proj, bias_tab, qw_full, kw_full, qsum, qexp, ksum, kexp)


def _t5_bucket(rel):
    half = NUM_BUCKETS // 2
    max_exact = half // 2
    bucket = jnp.where(rel > 0, half, 0)
    n = jnp.abs(rel)
    nf = jnp.maximum(n, 1).astype(F32)
    large = max_exact + (jnp.log(nf / max_exact) / math.log(MAX_DISTANCE / max_exact)
                         * (half - max_exact)).astype(I32)
    large = jnp.minimum(large, half - 1)
    return bucket + jnp.where(n < max_exact, n, large)


def _bias_table(rel_bias):
    kpos = jnp.arange(3 * BAND_BLOCK)[None, :]
    rel = kpos - BAND_BLOCK - jnp.arange(BAND_BLOCK)[:, None]
    in_window = jnp.abs(rel) <= WINDOW
    bias = rel_bias[_t5_bucket(rel)].astype(F32).transpose(2, 0, 1)
    mid = jnp.where(in_window[None], bias, NEG_INF)
    first = jnp.where(kpos[None] < BAND_BLOCK, NEG_INF, mid)
    last = jnp.where(kpos[None] >= 2 * BAND_BLOCK, NEG_INF, mid)
    return jnp.stack([first, mid, last])


CONV_HALO = 16
CONV_TILE = 512


def _conv_kernel(prev_ref, cur_ref, next_ref, w_ref, b_ref, o_ref, *, nt):
    i = pl.program_id(1)
    tl = cur_ref.shape[1]
    cur = cur_ref[0].astype(F32)
    prev = prev_ref[0].astype(F32) * jnp.where(i > 0, 1.0, 0.0)
    nxt = next_ref[0].astype(F32) * jnp.where(i < nt - 1, 1.0, 0.0)
    ext = jnp.concatenate([prev, cur, nxt], axis=0)
    half = CONV_WIDTH // 2
    acc = jnp.broadcast_to(b_ref[...], cur.shape)
    for t in range(CONV_WIDTH):
        off = CONV_HALO - half + t
        acc = acc + ext[off:off + tl] * w_ref[t:t + 1, :]
    o_ref[0] = (acc * jax.nn.sigmoid(acc)).astype(o_ref.dtype)


def _conv_silu(proj, conv_w, conv_b):
    b, l, _ = proj.shape
    tl = min(l, CONV_TILE)
    nt = l // tl
    cw = CONV_TILE
    nch = CONV_CH // cw
    c0 = COL_XBC // cw
    hb = tl // CONV_HALO
    nh = l // CONV_HALO
    return pl.pallas_call(
        functools.partial(_conv_kernel, nt=nt),
        out_shape=jax.ShapeDtypeStruct((b, l, CONV_CH), BF16),
        grid=(b, nt, nch),
        in_specs=[pl.BlockSpec((1, CONV_HALO, cw), lambda bi, i, c: (bi, jnp.maximum(i * hb - 1, 0), c0 + c)),
                  pl.BlockSpec((1, tl, cw), lambda bi, i, c: (bi, i, c0 + c)),
                  pl.BlockSpec((1, CONV_HALO, cw), lambda bi, i, c: (bi, jnp.minimum((i + 1) * hb, nh - 1), c0 + c)),
                  pl.BlockSpec((CONV_WIDTH, cw), lambda bi, i, c: (0, c)),
                  pl.BlockSpec((1, cw), lambda bi, i, c: (0, c))],
        out_specs=pl.BlockSpec((1, tl, cw), lambda bi, i, c: (bi, i, c)),
        compiler_params=_cparams(("parallel", "parallel", "parallel")),
        name="conv_silu",
    )(proj, proj, proj, conv_w, conv_b.reshape(1, CONV_CH))


def _ssd_kernel(xs_ref, b_ref, c_ref, dt_ref, dtb_ref, alog_ref, e_ref, y_ref, st_ref, *, rev):
    @pl.when(pl.program_id(1) == 0)
    def _():
        st_ref[...] = jnp.zeros_like(st_ref)

    q = SSD_CHUNK
    off = SSD_HEADS if rev else 0
    z = dt_ref[0].astype(F32) + dtb_ref[...]
    dt = jnp.maximum(z, 0.0) + jnp.log1p(jnp.exp(-jnp.abs(z)))
    da = dt * (-jnp.exp(alog_ref[...]))
    ri = lax.broadcasted_iota(I32, (q, q), 0)
    ci = lax.broadcasted_iota(I32, (q, q), 1)
    if rev:
        mask = ci >= ri
        mask_t = ci <= ri
    else:
        mask = ci <= ri
        mask_t = ci >= ri
    tri = jnp.where(mask, 1.0, 0.0).astype(BF16)
    tri_t = jnp.where(mask_t, 1.0, 0.0).astype(BF16)
    cum = _dot2_left(tri, da)
    cum_t = _dot2(da.T, tri_t)
    tot = cum[0:1, :] if rev else cum[q - 1:q, :]
    e = e_ref[...]
    dt_full = _dot2(dt, e)
    cum_full = _dot2(cum, e)
    tot_full = _dot2(tot, e)
    xdt_f = xs_ref[0].astype(F32) * dt_full
    xdt = xdt_f.astype(BF16)
    xw = (xdt_f * jnp.exp(tot_full - cum_full)).astype(BF16)
    expcum = jnp.exp(cum_full)
    chunk_decay = jnp.exp(tot_full)
    gw = SSD_GROUP_WIDTH
    for g in range(SSD_GROUPS):
        bg = b_ref[0][:, g * SSD_STATE:(g + 1) * SSD_STATE]
        cg = c_ref[0][:, g * SSD_STATE:(g + 1) * SSD_STATE]
        bgt = bg.astype(F32).T.astype(BF16)
        cb = jnp.dot(cg, bgt, preferred_element_type=F32)
        ys = []
        for r in range(SSD_HEADS_PER_GROUP):
            h = g * SSD_HEADS_PER_GROUP + r
            seg = cum[:, off + h:off + h + 1] - cum_t[off + h:off + h + 1, :]
            dec = jnp.exp(jnp.where(mask, seg, NEG_INF))
            m = (cb * dec).astype(BF16)
            ys.append(jnp.dot(m, xdt[:, h * SSD_HEAD_DIM:(h + 1) * SSD_HEAD_DIM],
                              preferred_element_type=F32))
        y_diag = jnp.concatenate(ys, axis=1)
        st = st_ref[g]
        y_off = jnp.dot(cg, st.astype(BF16), preferred_element_type=F32) * expcum[:, g * gw:(g + 1) * gw]
        new = jnp.dot(bgt, xw[:, g * gw:(g + 1) * gw], preferred_element_type=F32)
        st_ref[g] = st * chunk_decay[:, g * gw:(g + 1) * gw] + new
        y_ref[0, :, g * gw:(g + 1) * gw] = (y_diag + y_off).astype(y_ref.dtype)


def _ssd_scan(xbc, proj, dtb_row, alog_row, expand, rev):
    b, l, _ = xbc.shape
    nc = l // SSD_CHUNK
    gs = SSD_GROUPS * SSD_STATE
    cidx = (lambda c: nc - 1 - c) if rev else (lambda c: c)
    return pl.pallas_call(
        functools.partial(_ssd_kernel, rev=rev),
        out_shape=jax.ShapeDtypeStruct((b, l, SSD_INNER), BF16),
        grid=(b, nc),
        in_specs=[pl.BlockSpec((1, SSD_CHUNK, SSD_INNER), lambda bi, c: (bi, cidx(c), 0)),
                  pl.BlockSpec((1, SSD_CHUNK, gs), lambda bi, c: (bi, cidx(c), SSD_INNER // gs)),
                  pl.BlockSpec((1, SSD_CHUNK, gs), lambda bi, c: (bi, cidx(c), SSD_INNER // gs + 1)),
                  pl.BlockSpec((1, SSD_CHUNK, LANES), lambda bi, c: (bi, cidx(c), COL_DT // LANES)),
                  pl.BlockSpec((1, LANES), lambda bi, c: (0, 0)),
                  pl.BlockSpec((1, LANES), lambda bi, c: (0, 0)),
                  pl.BlockSpec((LANES, SSD_INNER), lambda bi, c: (0, 0))],
        out_specs=pl.BlockSpec((1, SSD_CHUNK, SSD_INNER), lambda bi, c: (bi, cidx(c), 0)),
        scratch_shapes=[pltpu.VMEM((SSD_GROUPS, SSD_STATE, SSD_GROUP_WIDTH), F32)],
        compiler_params=_cparams(("parallel", "arbitrary")),
        name="ssd_scan_bwd" if rev else "ssd_scan_fwd",
    )(xbc, xbc, xbc, proj, dtb_row, alog_row, expand)


def _merge_kernel(x_ref, yf_ref, yb_ref, xs_ref, z_ref, ao_ref, ga_ref, gs_ref, g1_ref,
                  dsk_ref, snw_ref, wssd_ref, wattn_ref, wout_ref, o_ref):
    y = yf_ref[0].astype(F32) + yb_ref[0].astype(F32) + xs_ref[0].astype(F32) * dsk_ref[...]
    z = z_ref[0].astype(F32)
    y = y * (z * jax.nn.sigmoid(z))
    parts = []
    for g in range(SSD_GROUPS):
        yg = y[:, g * SSD_GROUP_WIDTH:(g + 1) * SSD_GROUP_WIDTH]
        parts.append(yg * lax.rsqrt(jnp.mean(yg * yg, axis=-1, keepdims=True) + EPS))
    y = jnp.concatenate(parts, axis=1) * snw_ref[...]
    ssd = jnp.dot(y.astype(BF16), wssd_ref[...], preferred_element_type=F32)
    attn = jnp.dot(ao_ref[0], wattn_ref[...], preferred_element_type=F32)
    merged = (jax.nn.sigmoid(ga_ref[0].astype(F32)) * attn
              + jax.nn.sigmoid(gs_ref[0].astype(F32)) * ssd)
    out = jnp.dot(merged.astype(BF16), wout_ref[...], preferred_element_type=F32)
    o_ref[0] = x_ref[0] + g1_ref[0] * out


def _merge(x, yf, yb, xbc, proj, attn_o, modr, dsk_full, snw, w_ssd, w_attn, w_out):
    b, l, _ = x.shape
    tl = min(l, 256)
    tok = lambda w, colb: pl.BlockSpec((1, tl, w), lambda bi, i: (bi, i, colb))
    full = lambda r, c: pl.BlockSpec((r, c), lambda bi, i: (0, 0))
    return pl.pallas_call(
        _merge_kernel,
        out_shape=jax.ShapeDtypeStruct((b, l, D_MODEL), F32),
        grid=(b, l // tl),
        in_specs=[tok(D_MODEL, 0), tok(SSD_INNER, 0), tok(SSD_INNER, 0), tok(SSD_INNER, 0),
                  tok(SSD_INNER, COL_Z // SSD_INNER), tok(ATTN_WIDTH, 0),
                  tok(D_MODEL, COL_GA // D_MODEL), tok(D_MODEL, COL_GS // D_MODEL),
                  pl.BlockSpec((1, 1, D_MODEL), lambda bi, i: (bi * N_MOD + 2, 0, 0)),
                  full(1, SSD_INNER), full(1, SSD_INNER),
                  full(SSD_INNER, D_MODEL), full(ATTN_WIDTH, D_MODEL), full(D_MODEL, D_MODEL)],
        out_specs=tok(D_MODEL, 0),
        compiler_params=_cparams(("parallel", "parallel")),
        name="merge_out_proj",
    )(x, yf, yb, xbc, proj, attn_o, proj, proj, modr, dsk_full, snw, w_ssd, w_attn, w_out)


def _perm_src(j):
    return (j % 2) * (D_MODEL // 2) + (j // 2) * LANES


def _topk_rows(s, k):
    rowf = lax.broadcasted_iota(I32, s.shape, 0).astype(F32)
    vals, ids = [], []
    for _ in range(k):
        m = jnp.max(s, axis=0, keepdims=True)
        am = jnp.min(jnp.where(s == m, rowf, float(s.shape[0])), axis=0, keepdims=True)
        vals.append(m)
        ids.append(am)
        s = jnp.where(rowf == am, -jnp.inf, s)
    return vals, ids


def _candidate_pieces():
    pieces = []
    for a in range(PEER_TOPK):
        nb = PEER_TOPK // (a + 1)
        if nb >= SUBLANES // 2:
            for b0 in range(0, nb, SUBLANES):
                pieces.append((a, 1, b0, SUBLANES))
    a_done = max(p[0] for p in pieces) + 1
    for b in range(PEER_TOPK):
        na = PEER_TOPK // (b + 1)
        if na > a_done:
            for a0 in range(0, na, SUBLANES):
                pieces.append((a0, SUBLANES, b, 1))
    return pieces, a_done


def _route_kernel(x_ref, sh_ref, sc_ref, nw_ref, wq_ref, keys_ref, hp_ref, idx_ref, gate_ref):
    h = _norm_mod(x_ref[0], nw_ref[...], sc_ref[0], sh_ref[0])
    hp_ref[0] = h
    qall = jnp.dot(h.astype(BF16), wq_ref[...], preferred_element_type=F32)
    ntok = qall.shape[0]
    tops, topi = [], []
    for c in range(2 * PEER_HEADS):
        qc = qall[:, c * PEER_HALF:(c + 1) * PEER_HALF].astype(BF16)
        s = lax.dot_general(keys_ref[c], qc, (((1,), (1,)), ((), ())), preferred_element_type=F32)
        vals, ids = _topk_rows(s, PEER_TOPK)
        tops.append(vals)
        topi.append(ids)
    pieces, a_done = _candidate_pieces()
    sub = lax.broadcasted_iota(I32, (SUBLANES, ntok), 0)
    subf = sub.astype(F32)
    pos_parts, drop_parts = [], []
    for a0, na, b0, nb in pieces:
        if na == 1:
            pos_parts.append(subf + float(a0 * PEER_TOPK + b0))
            keep = sub + b0 < PEER_TOPK // (a0 + 1)
        else:
            pos_parts.append(subf * float(PEER_TOPK) + float(a0 * PEER_TOPK + b0))
            keep = jnp.where(sub + a0 >= a_done, sub + a0, PEER_TOPK) < PEER_TOPK // (b0 + 1)
        drop_parts.append(jnp.where(keep, 0.0, -jnp.inf))
    pos = jnp.concatenate(pos_parts, axis=0)
    drop = jnp.concatenate(drop_parts, axis=0)
    idx_rows, gate_rows = [], []
    for hd in range(PEER_HEADS):
        s0 = jnp.concatenate(tops[2 * hd], axis=0)
        i0 = jnp.concatenate(topi[2 * hd], axis=0)
        s1 = jnp.concatenate(tops[2 * hd + 1], axis=0)
        i1 = jnp.concatenate(topi[2 * hd + 1], axis=0)
        cparts, iparts = [], []
        for a0, na, b0, nb in pieces:
            if na == 1:
                cparts.append(tops[2 * hd][a0] + s1[b0:b0 + nb])
                iparts.append(topi[2 * hd][a0] * float(N_KEYS) + i1[b0:b0 + nb])
            else:
                cparts.append(s0[a0:a0 + na] + tops[2 * hd + 1][b0])
                iparts.append(i0[a0:a0 + na] * float(N_KEYS) + topi[2 * hd + 1][b0])
        cand = jnp.concatenate(cparts, axis=0) + drop
        cidx = jnp.concatenate(iparts, axis=0)
        best = []
        for _ in range(PEER_TOPK):
            m = jnp.max(cand, axis=0, keepdims=True)
            first = jnp.min(jnp.where(cand == m, pos, float(PEER_TOPK * PEER_TOPK)), axis=0, keepdims=True)
            sel = pos == first
            idx_rows.append(jnp.max(jnp.where(sel, cidx, -1.0), axis=0, keepdims=True))
            best.append(m)
            cand = jnp.where(sel, -jnp.inf, cand)
        bs = jnp.concatenate(best, axis=0)
        p = jnp.exp(bs - best[0])
        gate_rows.append(p / jnp.sum(p, axis=0, keepdims=True))
    idx_t = jnp.concatenate(idx_rows, axis=0)
    gate_t = jnp.concatenate(gate_rows, axis=0)
    idx_ref[0] = (idx_t * float(ROW_WORDS)).T.astype(I32)
    gate_ref[0] = gate_t.T


def _compact_rows(vals, shift, valid, up):
    n = valid.shape[0]
    for bit in range(n.bit_length() - 1):
        step = 1 << bit
        roll = (lambda a: pltpu.roll(a, (n - step) if up else step, axis=0))
        move = valid * ((shift >> bit) & 1)
        bring = roll(move) != 0
        vals = [jnp.where(bring, roll(v), v) for v in vals]
        shift = jnp.where(bring, roll(shift), shift)
        valid = jnp.where(bring, 1, valid - move)
    return vals


def _partition_by_half(idx_t, gate_t):
    n = idx_t.shape[0]
    half = float(N_EXPERTS // 2)
    high = idx_t >= half
    local = jnp.where(high, idx_t - half, idx_t)
    is_high = jnp.where(high, 1.0, 0.0).astype(BF16)
    is_low = jnp.where(high, 0.0, 1.0).astype(BF16)
    ri = lax.broadcasted_iota(I32, (n, n), 0)
    ci = lax.broadcasted_iota(I32, (n, n), 1)
    before = jnp.where(ci < ri, 1.0, 0.0).astype(BF16)
    after = jnp.where(ci > ri, 1.0, 0.0).astype(BF16)
    high_before = jnp.dot(before, is_high, preferred_element_type=F32).astype(I32)
    low_after = jnp.dot(after, is_low, preferred_element_type=F32).astype(I32)
    n_low = jnp.sum(is_low.astype(F32), axis=0, keepdims=True)
    lo_idx, lo_gate = _compact_rows([local, gate_t], high_before, jnp.where(high, 0, 1), up=True)
    hi_idx, hi_gate = _compact_rows([local, gate_t], low_after, jnp.where(high, 1, 0), up=False)
    first = lax.broadcasted_iota(I32, idx_t.shape, 0).astype(F32) < n_low
    return jnp.where(first, lo_idx, hi_idx), jnp.where(first, lo_gate, hi_gate), n_low


def _route(x1, modr, norm_w, wq, keys):
    b, l, _ = x1.shape
    tl = min(l, 256)
    tok = lambda w: pl.BlockSpec((1, tl, w), lambda bi, i: (bi, i, 0))
    return pl.pallas_call(
        _route_kernel,
        out_shape=(jax.ShapeDtypeStruct((b, l, D_MODEL), F32),
                   jax.ShapeDtypeStruct((b, l, N_SEL), I32),
                   jax.ShapeDtypeStruct((b, l, N_SEL), F32)),
        grid=(b, l // tl),
        in_specs=[tok(D_MODEL),
                  pl.BlockSpec((1, 1, D_MODEL), lambda bi, i: (bi * N_MOD + 3, 0, 0)),
                  pl.BlockSpec((1, 1, D_MODEL), lambda bi, i: (bi * N_MOD + 4, 0, 0)),
                  pl.BlockSpec((1, D_MODEL), lambda bi, i: (0, 0)),
                  pl.BlockSpec((D_MODEL, PEER_HEADS * PEER_KEY_DIM), lambda bi, i: (0, 0)),
                  pl.BlockSpec((2 * PEER_HEADS, N_KEYS, PEER_HALF), lambda bi, i: (0, 0, 0))],
        out_specs=(tok(D_MODEL), tok(N_SEL), tok(N_SEL)),
        compiler_params=_cparams(("parallel", "parallel")),
        name="peer_route",
    )(x1, modr, modr, norm_w, wq, keys)


def _pack_rows(tab):
    n = tab.shape[0]
    bits = lax.bitcast_convert_type(tab.astype(BF16), jnp.uint16).astype(jnp.uint32)
    lo = bits[:, :D_MODEL // 2]
    hi = bits[:, D_MODEL // 2:]
    words = lax.bitcast_convert_type(lo | (hi << 16), I32)
    return words.reshape(n, ROW_WORDS, LANES)


def _pack_tables(tab_u, tab_v):
    slabs = jnp.stack([_pack_rows(tab_u), _pack_rows(tab_v)], axis=2).reshape(N_EXPERTS, SLAB_ROWS, LANES)
    half = N_EXPERTS // 2
    return (slabs[:half].reshape(half * SLAB_ROWS, LANES), slabs[half:].reshape(half * SLAB_ROWS, LANES))


def _gelu_tanh(x):
    return 0.5 * x * (1.0 + jnp.tanh(math.sqrt(2.0 / math.pi) * (x + 0.044715 * (x * x * x))))


def _peer_first_kernel(idx_ref, cnt_ref, x_ref, gate_ref, tab_ref, dmask_ref, gsum_ref, rep_ref, ones_ref,
                       out_ref, tile0, tile1, wbuf0, wbuf1, **static):
    _peer_pass_body(idx_ref, cnt_ref, x_ref, gate_ref, None, tab_ref, dmask_ref, gsum_ref, rep_ref, ones_ref,
                    out_ref, (tile0, tile1), (wbuf0, wbuf1), **static)


def _peer_next_kernel(idx_ref, cnt_ref, x_ref, gate_ref, prev_ref, tab_ref, dmask_ref, gsum_ref, rep_ref, ones_ref,
                      out_ref, tile0, tile1, wbuf0, wbuf1, **static):
    _peer_pass_body(idx_ref, cnt_ref, x_ref, gate_ref, prev_ref, tab_ref, dmask_ref, gsum_ref, rep_ref, ones_ref,
                    out_ref, (tile0, tile1), (wbuf0, wbuf1), **static)


def _peer_pass_body(idx_ref, cnt_ref, x_ref, gate_ref, prev_ref, tab_ref, dmask_ref, gsum_ref, rep_ref, ones_ref,
                    out_ref, tiles, wbufs, *, high, s_lo, s_hi, rare):
    @pl.when(pl.program_id(0) == 0)
    def _():
        for tile in tiles:
            tile[...] = jnp.zeros_like(tile)

    lane = lax.broadcasted_iota(I32, (1, N_SEL), 1)
    in_range = jnp.logical_and(lane >= s_lo * PEER_CHUNK, lane < s_hi * PEER_CHUNK)
    crows = PEER_CHUNK * SLAB_ROWS
    nrows = (s_hi - s_lo) * PEER_CHUNK * ROW_WORDS

    def gather_chunk(tile, tt, base, c):
        k0 = base + c * PEER_CHUNK
        r0 = (c - s_lo) * crows
        r0 = r0 if isinstance(c, int) else pl.multiple_of(r0, crows)
        for kk in range(PEER_CHUNK):
            r = pl.multiple_of(idx_ref[k0 + kk], SLAB_ROWS)
            tile[tt, pl.ds(r0 + kk * SLAB_ROWS, SLAB_ROWS), :] = tab_ref[pl.ds(r, SLAB_ROWS), :]

    def valid_row(n_low):
        mine = (lane >= n_low) if high else (lane < n_low)
        return jnp.where(jnp.logical_and(mine, in_range), 1.0, 0.0)

    def needed_chunks(n_low):
        if high:
            return jnp.maximum(n_low // PEER_CHUNK, s_lo), s_hi
        return s_lo, jnp.minimum((n_low + PEER_CHUNK - 1) // PEER_CHUNK, s_hi)

    def token0(g):
        return g * PEER_TOK_GROUP if isinstance(g, int) else pl.multiple_of(g * PEER_TOK_GROUP, PEER_TOK_GROUP)

    def rows8(t):
        return pl.ds(t * SUBLANES if isinstance(t, int) else pl.multiple_of(t * SUBLANES, SUBLANES), SUBLANES)

    def first_stage(g, par):
        t0 = token0(g)
        tile, wbuf = tiles[par], wbufs[par]
        counts = [cnt_ref[t0 + tt] for tt in range(PEER_TOK_GROUP)]
        bases = [(t0 + tt) * N_SEL for tt in range(PEER_TOK_GROUP)]
        vs, valid = [], []

        def token(tt):
            valid.append(valid_row(counts[tt]))
            if rare:
                def chunk(c, carry):
                    gather_chunk(tile, tt, bases[tt], c)
                    return carry
                lax.fori_loop(*needed_chunks(counts[tt]), chunk, 0)
            else:
                for c in range(s_lo, s_hi):
                    gather_chunk(tile, tt, bases[tt], c)
            rows = pltpu.bitcast(tile[tt, pl.ds(0, nrows, stride=2), :], BF16)
            x8 = x_ref[rows8(t0 + tt), :].astype(BF16)
            r = lax.dot_general(x8, rows, (((1,), (1,)), ((), ())), preferred_element_type=F32)
            vs.append(jnp.sum(r * dmask_ref[...], axis=0, keepdims=True))

        def finish():
            act = _dot2(jnp.concatenate(vs, axis=0), gsum_ref[...])
            wbuf[...] = gate_ref[pl.ds(t0, PEER_TOK_GROUP), :] * _gelu_tanh(act) * jnp.concatenate(valid, axis=0)

        return token, finish

    def second_stage(g, par):
        t0 = token0(g)
        tile = tiles[par]
        w = wbufs[par][...]
        half_rows = SUBLANES // 2

        def token(tt):
            vt = tile[tt, pl.ds(1, nrows, stride=2), :]
            v_lo = pltpu.bitcast(vt << 16, F32)
            v_hi = pltpu.bitcast(vt & jnp.int32(-65536), F32)
            wrows = jnp.dot((rep_ref[...] * w[tt:tt + 1, :]).astype(BF16), ones_ref[...],
                            preferred_element_type=F32)
            acc_lo = jnp.sum((v_lo * wrows).reshape(nrows // SUBLANES, SUBLANES, LANES), axis=0)
            acc_hi = jnp.sum((v_hi * wrows).reshape(nrows // SUBLANES, SUBLANES, LANES), axis=0)
            o8 = jnp.concatenate([acc_lo[:half_rows] + acc_lo[half_rows:],
                                  acc_hi[:half_rows] + acc_hi[half_rows:]], axis=0)
            sl = rows8(t0 + tt)
            out_ref[sl, :] = o8 if prev_ref is None else prev_ref[sl, :] + o8

        return token

    def step(first, second):
        tok1, fin1 = first_stage(*first) if first else (None, None)
        tok2 = second_stage(*second) if second else None
        for tt in range(PEER_TOK_GROUP):
            if tok1:
                tok1(tt)
            if tok2:
                tok2(tt)
        if fin1:
            fin1()

    ngroups = PEER_BLOCK // PEER_TOK_GROUP
    if rare:
        out_ref[...] = prev_ref[...]

        def group(g, carry):
            t0 = token0(g)
            counts = [cnt_ref[t0 + tt] for tt in range(PEER_TOK_GROUP)]
            if high:
                needed = functools.reduce(jnp.minimum, counts) < s_hi * PEER_CHUNK
            else:
                needed = functools.reduce(jnp.maximum, counts) > s_lo * PEER_CHUNK

            @pl.when(needed)
            def _():
                step((g, 0), None)
                step(None, (g, 0))
            return carry

        lax.fori_loop(0, ngroups, group, 0)
    else:
        step((0, 0), None)

        def pair(j, carry):
            step((2 * j + 1, 1), (2 * j, 0))
            step((2 * j + 2, 0), (2 * j + 1, 1))
            return carry

        lax.fori_loop(0, ngroups // 2 - 1, pair, 0)
        step((ngroups - 1, 1), (ngroups - 2, 0))
        step(None, (ngroups - 1, 1))


def _peer_consts(s_lo, s_hi):
    ncol = (s_hi - s_lo) * PEER_CHUNK * SUBLANES
    c = jnp.arange(ncol)
    dmask = (c[None, :] % SUBLANES == jnp.arange(SUBLANES)[:, None]).astype(F32)
    sel = c[:, None] // SUBLANES + s_lo * PEER_CHUNK
    gsum = (sel == jnp.arange(N_SEL)[None, :]).astype(BF16)
    row_sel = jnp.arange(ncol // 2)[:, None] // ROW_WORDS + s_lo * PEER_CHUNK
    rep = (row_sel == jnp.arange(N_SEL)[None, :]).astype(F32)
    return dmask, gsum, rep, jnp.ones((N_SEL, LANES), BF16)


def _peer_pass(idx_flat, cnt_flat, x8, gates, prev, tab, *, high, s_lo, s_hi, rare):
    t = gates.shape[0]
    dmask, gsum, rep, ones = _peer_consts(s_lo, s_hi)
    static = dict(high=high, s_lo=s_lo, s_hi=s_hi, rare=rare)
    const = lambda a: pl.BlockSpec(a.shape, lambda i: (0, 0))
    row_spec = pl.BlockSpec((PEER_BLOCK * SUBLANES, LANES), lambda i: (i, 0))
    tok_spec = pl.BlockSpec((PEER_BLOCK, N_SEL), lambda i: (i, 0))
    in_specs = [pl.BlockSpec((PEER_BLOCK * N_SEL,), lambda i: (i,), memory_space=pltpu.SMEM),
                pl.BlockSpec((PEER_BLOCK,), lambda i: (i,), memory_space=pltpu.SMEM),
                row_spec, tok_spec]
    args = [idx_flat, cnt_flat, x8, gates]
    if prev is not None:
        in_specs.append(row_spec)
        args.append(prev)
    in_specs += [pl.BlockSpec(memory_space=pltpu.VMEM), const(dmask), const(gsum), const(rep), const(ones)]
    args += [tab, dmask, gsum, rep, ones]
    tile = pltpu.VMEM((PEER_TOK_GROUP, (s_hi - s_lo) * PEER_CHUNK * SLAB_ROWS, LANES), I32)
    wbuf = pltpu.VMEM((PEER_TOK_GROUP, N_SEL), F32)
    return pl.pallas_call(
        functools.partial(_peer_first_kernel if prev is None else _peer_next_kernel, **static),
        out_shape=jax.ShapeDtypeStruct((t * SUBLANES, LANES), F32),
        grid=(t // PEER_BLOCK,),
        in_specs=in_specs,
        out_specs=row_spec,
        scratch_shapes=[tile, tile, wbuf, wbuf],
        compiler_params=_cparams(("arbitrary",)),
        name="peer_%s_%s" % ("high" if high else "low", "rare" if rare else "common"),
    )(*args)


def _gather_slabs(idx_ref, base, tab_ref, tile_ref, tt):
    for k in range(N_SEL):
        r = pl.multiple_of(idx_ref[base + k], ROW_WORDS)
        tile_ref[tt, k * ROW_WORDS:(k + 1) * ROW_WORDS, :] = tab_ref[pl.ds(r, ROW_WORDS), :]


def _unpack_rows(words):
    return pltpu.bitcast(words << 16, F32), pltpu.bitcast(words & jnp.int32(-65536), F32)


def _peer_act_kernel(idx_ref, x_ref, tab_ref, act_ref, tile, psum):
    nrows = N_SEL * ROW_WORDS
    half = SUBLANES // 2
    ones = jnp.ones((SUBLANES, LANES), BF16)
    nt = (((1,), (1,)), ((), ()))

    def group(gi, carry):
        t0 = pl.multiple_of(gi * PEER_TOK_GROUP, PEER_TOK_GROUP)
        acts = []
        for tt in range(PEER_TOK_GROUP):
            _gather_slabs(idx_ref, (t0 + tt) * N_SEL, tab_ref, tile, tt)
            x8 = x_ref[pl.ds(pl.multiple_of((t0 + tt) * SUBLANES, SUBLANES), SUBLANES), :]
            x_lo = jnp.concatenate([x8[:half], x8[:half]], axis=0)
            x_hi = jnp.concatenate([x8[half:], x8[half:]], axis=0)
            u_lo, u_hi = _unpack_rows(tile[tt])
            prod = (u_lo.reshape(nrows // SUBLANES, SUBLANES, LANES) * x_lo
                    + u_hi.reshape(nrows // SUBLANES, SUBLANES, LANES) * x_hi)
            part = prod + pltpu.roll(prod, 1, axis=1)
            part = part + pltpu.roll(part, 2, axis=1)
            psum[tt] = part.reshape(nrows, LANES)
            q = psum[tt, pl.ds(ROW_WORDS - 1, N_SEL, stride=ROW_WORDS), :]
            hi, lo = _split_bf16(q)
            a = (lax.dot_general(ones, hi, nt, preferred_element_type=F32)
                 + lax.dot_general(ones, lo, nt, preferred_element_type=F32))
            acts.append(a[:1])
        act_ref[pl.ds(t0, PEER_TOK_GROUP), :] = jnp.concatenate(acts, axis=0)
        return carry

    lax.fori_loop(0, PEER_BLOCK // PEER_TOK_GROUP, group, 0)


def _peer_sum_kernel(idx_ref, act_ref, gate_ref, tab_ref, rep_ref, ones_ref, out_ref, tile):
    nrows = N_SEL * ROW_WORDS
    half = SUBLANES // 2

    def group(gi, carry):
        t0 = pl.multiple_of(gi * PEER_TOK_GROUP, PEER_TOK_GROUP)
        w = gate_ref[pl.ds(t0, PEER_TOK_GROUP), :] * _gelu_tanh(act_ref[pl.ds(t0, PEER_TOK_GROUP), :])
        for tt in range(PEER_TOK_GROUP):
            _gather_slabs(idx_ref, (t0 + tt) * N_SEL, tab_ref, tile, tt)
            v_lo, v_hi = _unpack_rows(tile[tt])
            wrows = jnp.dot((rep_ref[...] * w[tt:tt + 1, :]).astype(BF16), ones_ref[...],
                            preferred_element_type=F32)
            acc_lo = jnp.sum((v_lo * wrows).reshape(nrows // SUBLANES, SUBLANES, LANES), axis=0)
            acc_hi = jnp.sum((v_hi * wrows).reshape(nrows // SUBLANES, SUBLANES, LANES), axis=0)
            o8 = jnp.concatenate([acc_lo[:half] + acc_lo[half:], acc_hi[:half] + acc_hi[half:]], axis=0)
            out_ref[pl.ds(pl.multiple_of((t0 + tt) * SUBLANES, SUBLANES), SUBLANES), :] = o8
        return carry

    lax.fori_loop(0, PEER_BLOCK // PEER_TOK_GROUP, group, 0)


def _peer_experts(idx_flat, x8, gates, tab_u, tab_v):
    t = gates.shape[0]
    idx_spec = pl.BlockSpec((PEER_BLOCK * N_SEL,), lambda i: (i,), memory_space=pltpu.SMEM)
    tab_spec = pl.BlockSpec(memory_space=pltpu.VMEM)
    tok_spec = pl.BlockSpec((PEER_BLOCK, N_SEL), lambda i: (i, 0))
    row_spec = pl.BlockSpec((PEER_BLOCK * SUBLANES, LANES), lambda i: (i, 0))
    const = lambda a: pl.BlockSpec(a.shape, lambda i: (0, 0))
    nrows = N_SEL * ROW_WORDS
    tile = pltpu.VMEM((PEER_TOK_GROUP, nrows, LANES), I32)
    act = pl.pallas_call(
        _peer_act_kernel,
        out_shape=jax.ShapeDtypeStruct((t, N_SEL), F32),
        grid=(t // PEER_BLOCK,),
        in_specs=[idx_spec, row_spec, tab_spec],
        out_specs=tok_spec,
        scratch_shapes=[tile, pltpu.VMEM((PEER_TOK_GROUP, nrows, LANES), F32)],
        compiler_params=_cparams(("parallel",)),
        name="peer_act",
    )(idx_flat, x8, tab_u)
    rep = (jnp.arange(nrows)[:, None] // ROW_WORDS == jnp.arange(N_SEL)[None, :]).astype(F32)
    ones = jnp.ones((N_SEL, LANES), BF16)
    return pl.pallas_call(
        _peer_sum_kernel,
        out_shape=jax.ShapeDtypeStruct((t * SUBLANES, LANES), F32),
        grid=(t // PEER_BLOCK,),
        in_specs=[idx_spec, tok_spec, tok_spec, tab_spec, const(rep), const(ones)],
        out_specs=row_spec,
        scratch_shapes=[tile],
        compiler_params=_cparams(("parallel",)),
        name="peer_sum",
    )(idx_flat, act, gates, tab_v, rep, ones)


def _final_kernel(x_ref, o_ref, g2_ref, y_ref):
    y_ref[0] = x_ref[0] + g2_ref[0] * o_ref[0]


def _final_residual(x1, outp, modr):
    b, l, _ = x1.shape
    tl = min(l, 512)
    tok = pl.BlockSpec((1, tl, D_MODEL), lambda bi, i: (bi, i, 0))
    return pl.pallas_call(
        _final_kernel,
        out_shape=jax.ShapeDtypeStruct((b, l, D_MODEL), F32),
        grid=(b, l // tl),
        in_specs=[tok, tok, pl.BlockSpec((1, 1, D_MODEL), lambda bi, i: (bi * N_MOD + 5, 0, 0))],
        out_specs=tok,
        compiler_params=_cparams(("parallel", "parallel")),
        name="peer_residual",
    )(x1, outp, modr)


def _prepare(rel_bias, ada_w, ada_b, norm1_w, norm2_w, w_in, q_norm_w, k_norm_w, attn_sink, conv_w,
             conv_b, a_log, dt_bias, d_skip, ssd_norm_w, w_attn_br, w_ssd_br, w_out, peer_wq,
             peer_keys, peer_u, peer_v):
    lyr = 0
    w = w_in[lyr]
    o = 0
    parts = {}
    for name, width in (("q", ATTN_WIDTH), ("k", KV_WIDTH), ("v", KV_WIDTH), ("z", SSD_INNER),
                        ("xbc", CONV_CH), ("dt", 2 * SSD_HEADS), ("ga", D_MODEL), ("gs", D_MODEL)):
        parts[name] = w[:, o:o + width]
        o += width
    pad = jnp.zeros((D_MODEL, PROJ_W - (COL_DT + 2 * SSD_HEADS)), w.dtype)
    w_all = jnp.concatenate([parts["z"], parts["q"], parts["ga"], parts["gs"], parts["xbc"],
                             parts["k"], parts["v"], parts["dt"], pad], axis=1).astype(BF16)
    lane_pad = LANES - 2 * SSD_HEADS
    expand = (jnp.arange(SSD_INNER)[None, :] // SSD_HEAD_DIM == jnp.arange(LANES)[:, None])
    pack = lambda tab: _pack_rows(tab).reshape(N_EXPERTS * ROW_WORDS, LANES)
    return dict(
        ada_w=ada_w[lyr], ada_b=ada_b[lyr],
        norm1_w=norm1_w[lyr].reshape(1, D_MODEL), norm2_w=norm2_w[lyr].reshape(1, D_MODEL),
        w_all=w_all,
        bias_tab=_bias_table(rel_bias), sink=attn_sink[lyr].astype(F32),
        qw=(jnp.tile(q_norm_w[lyr], N_Q_HEADS) * HEAD_DIM ** -0.5).reshape(1, ATTN_WIDTH),
        kw=jnp.tile(k_norm_w[lyr], N_KV_HEADS).reshape(1, KV_WIDTH),
        conv_w=conv_w[lyr], conv_b=conv_b[lyr],
        dtb_row=jnp.pad(dt_bias[lyr].astype(F32).reshape(1, -1), ((0, 0), (0, lane_pad))),
        alog_row=jnp.pad(a_log[lyr].astype(F32).reshape(1, -1), ((0, 0), (0, lane_pad))),
        expand_fwd=expand.astype(BF16),
        expand_bwd=jnp.roll(expand, SSD_HEADS, axis=0).astype(BF16),
        dsk_full=jnp.repeat(d_skip[lyr], SSD_HEAD_DIM).reshape(1, SSD_INNER),
        snw=ssd_norm_w[lyr].reshape(1, SSD_INNER),
        w_ssd=w_ssd_br[lyr].astype(BF16), w_attn=w_attn_br[lyr].astype(BF16), w_out=w_out[lyr].astype(BF16),
        wq=peer_wq[lyr].astype(BF16),
        keys=peer_keys[lyr].reshape(2 * PEER_HEADS, N_KEYS, PEER_HALF).astype(BF16),
        tab_u=pack(peer_u[lyr]), tab_v=pack(peer_v[lyr]),
    )


def _token_mixer_stage(x, modr, p):
    proj = _in_projection(x, modr, p["norm1_w"], p["w_all"])
    attn_o = _attention(proj, p["bias_tab"], p["sink"], p["qw"], p["kw"])
    xbc = _conv_silu(proj, p["conv_w"], p["conv_b"])
    yf = _ssd_scan(xbc, proj, p["dtb_row"], p["alog_row"], p["expand_fwd"], rev=False)
    yb = _ssd_scan(xbc, proj, p["dtb_row"], p["alog_row"], p["expand_bwd"], rev=True)
    return _merge(x, yf, yb, xbc, proj, attn_o, modr, p["dsk_full"], p["snw"],
                  p["w_ssd"], p["w_attn"], p["w_out"])


def _peer_stage(x1, modr, p):
    b, l, _ = x1.shape
    t = b * l
    hp, idx, gates = _route(x1, modr, p["norm2_w"], p["wq"], p["keys"])
    out = _peer_experts(idx.reshape(t * N_SEL), hp.reshape(t * SUBLANES, LANES), gates.reshape(t, N_SEL),
                        p["tab_u"], p["tab_v"])
    return _final_residual(x1, out.reshape(b, l, D_MODEL), modr)


def _encoder(x, c, p):
    nb = c.shape[0]
    modr = _modulation(c, p["ada_w"], p["ada_b"]).reshape(nb * N_MOD, 1, D_MODEL)
    x1 = _token_mixer_stage(x, modr, p)
    return _peer_stage(x1, modr, p)


def kernel(x_prompt, x_sample, c_prompt, c_sample, rel_bias, ada_w, ada_b, norm1_w, norm2_w, w_in,
           q_norm_w, k_norm_w, attn_sink, conv_w, conv_b, a_log, dt_bias, d_skip, ssd_norm_w,
           w_attn_br, w_ssd_br, w_out, peer_wq, peer_keys, peer_u, peer_v):
    p = _prepare(rel_bias, ada_w, ada_b, norm1_w, norm2_w, w_in, q_norm_w, k_norm_w, attn_sink,
                 conv_w, conv_b, a_log, dt_bias, d_skip, ssd_norm_w, w_attn_br, w_ssd_br, w_out,
                 peer_wq, peer_keys, peer_u, peer_v)
    return (_encoder(x_prompt, c_prompt, p), _encoder(x_sample, c_sample, p))
```

```python
import functools
import math

import jax
import jax.numpy as jnp
from jax import lax
from jax.experimental import pallas as pl
from jax.experimental.pallas import tpu as pltpu

F32 = jnp.float32
BF16 = jnp.bfloat16
I32 = jnp.int32

D_MODEL = 1024
HEAD_DIM = 64
N_Q_HEADS = 16
N_KV_HEADS = 4
Q_PER_KV = N_Q_HEADS // N_KV_HEADS
ATTN_WIDTH = N_Q_HEADS * HEAD_DIM
KV_WIDTH = N_KV_HEADS * HEAD_DIM
WINDOW = 128
BAND_BLOCK = 128
NUM_BUCKETS = 32
MAX_DISTANCE = 128
NEG_INF = -1e30

SSD_INNER = 2 * D_MODEL
SSD_HEAD_DIM = 64
SSD_HEADS = SSD_INNER // SSD_HEAD_DIM
SSD_GROUPS = 4
SSD_HEADS_PER_GROUP = SSD_HEADS // SSD_GROUPS
SSD_STATE = 128
SSD_CHUNK = 128
SSD_GROUP_WIDTH = SSD_INNER // SSD_GROUPS
CONV_WIDTH = 5
CONV_CH = SSD_INNER + 2 * SSD_GROUPS * SSD_STATE

PEER_HEADS = 8
PEER_KEY_DIM = 256
PEER_HALF = PEER_KEY_DIM // 2
N_KEYS = 128
N_EXPERTS = N_KEYS * N_KEYS
PEER_TOPK = 16
N_SEL = PEER_HEADS * PEER_TOPK

N_MOD = 6
EPS = 1e-6

COL_Z = 0
COL_Q = 2048
COL_GA = 3072
COL_GS = 4096
COL_XBC = 5120
COL_K = 8192
COL_V = 8448
COL_DT = 8704
PROJ_W = 9216

LANES = 128
SUBLANES = 8
VMEM_LIMIT = 56 * 1024 * 1024
ROW_WORDS = D_MODEL // (2 * LANES)
PEER_TOK_GROUP = 8
PEER_BLOCK = 128


def _cparams(sem):
    return pltpu.CompilerParams(dimension_semantics=sem, vmem_limit_bytes=VMEM_LIMIT)


def _split_bf16(v):
    hi = v.astype(BF16)
    lo = (v - hi.astype(F32)).astype(BF16)
    return hi, lo


def _dot2(v, m_bf16):
    hi, lo = _split_bf16(v)
    return (jnp.dot(hi, m_bf16, preferred_element_type=F32)
            + jnp.dot(lo, m_bf16, preferred_element_type=F32))


def _dot2_left(m_bf16, v):
    hi, lo = _split_bf16(v)
    return (jnp.dot(m_bf16, hi, preferred_element_type=F32)
            + jnp.dot(m_bf16, lo, preferred_element_type=F32))


def _mod_kernel(c_ref, w_ref, b_ref, o_ref):
    c = c_ref[...]
    sc = c * jax.nn.sigmoid(c)
    o_ref[...] = jnp.dot(sc, w_ref[...], preferred_element_type=F32,
                         precision=lax.Precision.HIGHEST) + b_ref[...]


def _modulation(c, ada_w, ada_b):
    nb = c.shape[0]
    n = ada_w.shape[1]
    tn = 1024
    return pl.pallas_call(
        _mod_kernel,
        out_shape=jax.ShapeDtypeStruct((nb, n), F32),
        grid=(n // tn,),
        in_specs=[pl.BlockSpec((nb, D_MODEL), lambda j: (0, 0)),
                  pl.BlockSpec((D_MODEL, tn), lambda j: (0, j)),
                  pl.BlockSpec((1, tn), lambda j: (0, j))],
        out_specs=pl.BlockSpec((nb, tn), lambda j: (0, j)),
        compiler_params=_cparams(("arbitrary",)),
        name="adaln_mod",
    )(c, ada_w, ada_b.reshape(1, n))


def _norm_mod(x, nw, sc, sh):
    ms = jnp.mean(x * x, axis=-1, keepdims=True)
    h = x * lax.rsqrt(ms + EPS) * nw
    return h * (1.0 + sc) + sh


def _inproj_kernel(x_ref, sh_ref, sc_ref, nw_ref, w_ref, o_ref, h_scr):
    @pl.when(pl.program_id(2) == 0)
    def _():
        h = _norm_mod(x_ref[0], nw_ref[...], sc_ref[0], sh_ref[0])
        h_scr[...] = h.astype(BF16)

    o_ref[0] = jnp.dot(h_scr[...], w_ref[...], preferred_element_type=F32).astype(o_ref.dtype)


def _in_projection(x, modr, norm_w, w_all):
    b, l, _ = x.shape
    tl = min(l, 1024)
    tn = 1024
    return pl.pallas_call(
        _inproj_kernel,
        out_shape=jax.ShapeDtypeStruct((b, l, PROJ_W), BF16),
        grid=(b, l // tl, PROJ_W // tn),
        in_specs=[pl.BlockSpec((1, tl, D_MODEL), lambda bi, i, j: (bi, i, 0)),
                  pl.BlockSpec((1, 1, D_MODEL), lambda bi, i, j: (bi * N_MOD + 0, 0, 0)),
                  pl.BlockSpec((1, 1, D_MODEL), lambda bi, i, j: (bi * N_MOD + 1, 0, 0)),
                  pl.BlockSpec((1, D_MODEL), lambda bi, i, j: (0, 0)),
                  pl.BlockSpec((D_MODEL, tn), lambda bi, i, j: (0, j))],
        out_specs=pl.BlockSpec((1, tl, tn), lambda bi, i, j: (bi, i, j)),
        scratch_shapes=[pltpu.VMEM((tl, D_MODEL), BF16)],
        compiler_params=_cparams(("parallel", "parallel", "arbitrary")),
        name="in_projection",
    )(x, modr, modr, norm_w, w_all)


def _head_sums(n_heads):
    c = jnp.arange(n_heads * HEAD_DIM)
    hsum = (c[:, None] // HEAD_DIM == jnp.arange(LANES)[None, :]).astype(BF16)
    return hsum, hsum.T


def _qk_norm(t, hsum, hexp, w_full):
    ssq = _dot2(t * t, hsum)
    inv = lax.rsqrt(ssq * (1.0 / HEAD_DIM) + EPS)
    return t * _dot2(inv, hexp) * w_full


def _attn_kernel(sink_ref, q_ref, kp_ref, kc_ref, kn_ref, vp_ref, vc_ref, vn_ref, bias_ref,
                 qw_ref, kw_ref, qsum_ref, qexp_ref, ksum_ref, kexp_ref, o_ref, s_scr, p_scr):
    q = _qk_norm(q_ref[0].astype(F32), qsum_ref[...], qexp_ref[...], qw_ref[...]).astype(BF16)
    k = jnp.concatenate([kp_ref[0], kc_ref[0], kn_ref[0]], axis=0).astype(F32)
    k = _qk_norm(k, ksum_ref[...], kexp_ref[...], kw_ref[...])
    v = jnp.concatenate([vp_ref[0], vc_ref[0], vn_ref[0]], axis=0).astype(F32)
    low = lax.broadcasted_iota(I32, (1, LANES), 1) < HEAD_DIM
    nt = (((1,), (1,)), ((), ()))
    k_sel, v_sel = [], []
    for c in range(N_KV_HEADS // 2):
        kslab = k[:, c * LANES:(c + 1) * LANES]
        vslab = v[:, c * LANES:(c + 1) * LANES]
        kroll = pltpu.roll(kslab, HEAD_DIM, axis=1)
        vroll = pltpu.roll(vslab, HEAD_DIM, axis=1)
        for e in range(2):
            k_sel.append((jnp.where(low, kroll if e else kslab, 0.0).astype(BF16),
                          jnp.where(low, 0.0, kslab if e else kroll).astype(BF16)))
            v_sel.append((jnp.where(low, vroll if e else vslab, 0.0).astype(BF16),
                          jnp.where(low, 0.0, vslab if e else vroll).astype(BF16)))
    for hd in range(N_Q_HEADS):
        qs = q[:, (hd // 2) * LANES:(hd // 2 + 1) * LANES]
        s = lax.dot_general(qs, k_sel[hd // Q_PER_KV][hd % 2], nt, preferred_element_type=F32)
        s_scr[hd] = s + bias_ref[0, hd]
    for hd in range(N_Q_HEADS):
        s = s_scr[hd]
        sk = sink_ref[hd]
        m = jnp.maximum(jnp.max(s, axis=-1, keepdims=True), sk)
        p = jnp.exp(s - m)
        denom = jnp.sum(p, axis=-1, keepdims=True) + jnp.exp(sk - m)
        p_scr[hd] = (p / denom).astype(BF16)
    for slab in range(N_Q_HEADS // 2):
        vs = v_sel[(2 * slab) // Q_PER_KV]
        acc = (jnp.dot(p_scr[2 * slab], vs[0], preferred_element_type=F32)
               + jnp.dot(p_scr[2 * slab + 1], vs[1], preferred_element_type=F32))
        o_ref[0, :, slab * LANES:(slab + 1) * LANES] = acc.astype(o_ref.dtype)


def _attention(proj, bias_tab, sink, qw_full, kw_full):
    b, l, _ = proj.shape
    nb = l // BAND_BLOCK
    assert nb >= 2
    kcol = COL_K // KV_WIDTH
    vcol = COL_V // KV_WIDTH
    qsum, qexp = _head_sums(N_Q_HEADS)
    ksum, kexp = _head_sums(N_KV_HEADS)

    def prev(i):
        return jnp.maximum(i - 1, 0)

    def nxt(i):
        return jnp.minimum(i + 1, nb - 1)

    def edge(i):
        return jnp.where(i == 0, 0, jnp.where(i == nb - 1, 2, 1))

    kv = lambda colb, f: pl.BlockSpec((1, BAND_BLOCK, KV_WIDTH), lambda bi, i: (bi, f(i), colb))
    same = lambda i: i
    const = lambda a: pl.BlockSpec(a.shape, lambda bi, i: (0,) * a.ndim)
    return pl.pallas_call(
        _attn_kernel,
        out_shape=jax.ShapeDtypeStruct((b, l, ATTN_WIDTH), BF16),
        grid=(b, nb),
        in_specs=[pl.BlockSpec(memory_space=pltpu.SMEM),
                  pl.BlockSpec((1, BAND_BLOCK, ATTN_WIDTH), lambda bi, i: (bi, i, COL_Q // ATTN_WIDTH)),
                  kv(kcol, prev), kv(kcol, same), kv(kcol, nxt),
                  kv(vcol, prev), kv(vcol, same), kv(vcol, nxt),
                  pl.BlockSpec((1, N_Q_HEADS, BAND_BLOCK, 3 * BAND_BLOCK), lambda bi, i: (edge(i), 0, 0, 0)),
                  const(qw_full), const(kw_full), const(qsum), const(qexp), const(ksum), const(kexp)],
        out_specs=pl.BlockSpec((1, BAND_BLOCK, ATTN_WIDTH), lambda bi, i: (bi, i, 0)),
        scratch_shapes=[pltpu.VMEM((N_Q_HEADS, BAND_BLOCK, 3 * BAND_BLOCK), F32),
                        pltpu.VMEM((N_Q_HEADS, BAND_BLOCK, 3 * BAND_BLOCK), BF16)],
        compiler_params=_cparams(("parallel", "parallel")),
        name="window_attention",
    )(sink, proj, proj, proj, proj, proj, proj, proj, bias_tab, qw_full, kw_full, qsum, qexp, ksum, kexp)


def _t5_bucket(rel):
    half = NUM_BUCKETS // 2
    max_exact = half // 2
    bucket = jnp.where(rel > 0, half, 0)
    n = jnp.abs(rel)
    nf = jnp.maximum(n, 1).astype(F32)
    large = max_exact + (jnp.log(nf / max_exact) / math.log(MAX_DISTANCE / max_exact)
                         * (half - max_exact)).astype(I32)
    large = jnp.minimum(large, half - 1)
    return bucket + jnp.where(n < max_exact, n, large)


def _bias_table(rel_bias):
    kpos = jnp.arange(3 * BAND_BLOCK)[None, :]
    rel = kpos - BAND_BLOCK - jnp.arange(BAND_BLOCK)[:, None]
    in_window = jnp.abs(rel) <= WINDOW
    bias = rel_bias[_t5_bucket(rel)].astype(F32).transpose(2, 0, 1)
    mid = jnp.where(in_window[None], bias, NEG_INF)
    first = jnp.where(kpos[None] < BAND_BLOCK, NEG_INF, mid)
    last = jnp.where(kpos[None] >= 2 * BAND_BLOCK, NEG_INF, mid)
    return jnp.stack([first, mid, last])


CONV_HALO = 16
CONV_TILE = 512


def _conv_kernel(prev_ref, cur_ref, next_ref, w_ref, b_ref, o_ref, *, nt):
    i = pl.program_id(1)
    tl = cur_ref.shape[1]
    cur = cur_ref[0].astype(F32)
    prev = prev_ref[0].astype(F32) * jnp.where(i > 0, 1.0, 0.0)
    nxt = next_ref[0].astype(F32) * jnp.where(i < nt - 1, 1.0, 0.0)
    ext = jnp.concatenate([prev, cur, nxt], axis=0)
    half = CONV_WIDTH // 2
    acc = jnp.broadcast_to(b_ref[...], cur.shape)
    for t in range(CONV_WIDTH):
        off = CONV_HALO - half + t
        acc = acc + ext[off:off + tl] * w_ref[t:t + 1, :]
    o_ref[0] = (acc * jax.nn.sigmoid(acc)).astype(o_ref.dtype)


def _conv_silu(proj, conv_w, conv_b):
    b, l, _ = proj.shape
    tl = min(l, CONV_TILE)
    nt = l // tl
    cw = CONV_TILE
    nch = CONV_CH // cw
    c0 = COL_XBC // cw
    hb = tl // CONV_HALO
    nh = l // CONV_HALO
    return pl.pallas_call(
        functools.partial(_conv_kernel, nt=nt),
        out_shape=jax.ShapeDtypeStruct((b, l, CONV_CH), BF16),
        grid=(b, nt, nch),
        in_specs=[pl.BlockSpec((1, CONV_HALO, cw), lambda bi, i, c: (bi, jnp.maximum(i * hb - 1, 0), c0 + c)),
                  pl.BlockSpec((1, tl, cw), lambda bi, i, c: (bi, i, c0 + c)),
                  pl.BlockSpec((1, CONV_HALO, cw), lambda bi, i, c: (bi, jnp.minimum((i + 1) * hb, nh - 1), c0 + c)),
                  pl.BlockSpec((CONV_WIDTH, cw), lambda bi, i, c: (0, c)),
                  pl.BlockSpec((1, cw), lambda bi, i, c: (0, c))],
        out_specs=pl.BlockSpec((1, tl, cw), lambda bi, i, c: (bi, i, c)),
        compiler_params=_cparams(("parallel", "parallel", "parallel")),
        name="conv_silu",
    )(proj, proj, proj, conv_w, conv_b.reshape(1, CONV_CH))


def _ssd_kernel(xs_ref, b_ref, c_ref, dt_ref, dtb_ref, alog_ref, e_ref, y_ref, st_ref, *, rev):
    @pl.when(pl.program_id(1) == 0)
    def _():
        st_ref[...] = jnp.zeros_like(st_ref)

    q = SSD_CHUNK
    off = SSD_HEADS if rev else 0
    z = dt_ref[0].astype(F32) + dtb_ref[...]
    dt = jnp.maximum(z, 0.0) + jnp.log1p(jnp.exp(-jnp.abs(z)))
    da = dt * (-jnp.exp(alog_ref[...]))
    ri = lax.broadcasted_iota(I32, (q, q), 0)
    ci = lax.broadcasted_iota(I32, (q, q), 1)
    if rev:
        mask = ci >= ri
        mask_t = ci <= ri
    else:
        mask = ci <= ri
        mask_t = ci >= ri
    tri = jnp.where(mask, 1.0, 0.0).astype(BF16)
    tri_t = jnp.where(mask_t, 1.0, 0.0).astype(BF16)
    cum = _dot2_left(tri, da)
    cum_t = _dot2(da.T, tri_t)
    tot = cum[0:1, :] if rev else cum[q - 1:q, :]
    e = e_ref[...]
    dt_full = _dot2(dt, e)
    cum_full = _dot2(cum, e)
    tot_full = _dot2(tot, e)
    xdt_f = xs_ref[0].astype(F32) * dt_full
    xdt = xdt_f.astype(BF16)
    xw = (xdt_f * jnp.exp(tot_full - cum_full)).astype(BF16)
    expcum = jnp.exp(cum_full)
    chunk_decay = jnp.exp(tot_full)
    gw = SSD_GROUP_WIDTH
    for g in range(SSD_GROUPS):
        bg = b_ref[0][:, g * SSD_STATE:(g + 1) * SSD_STATE]
        cg = c_ref[0][:, g * SSD_STATE:(g + 1) * SSD_STATE]
        bgt = bg.astype(F32).T.astype(BF16)
        cb = jnp.dot(cg, bgt, preferred_element_type=F32)
        ys = []
        for r in range(SSD_HEADS_PER_GROUP):
            h = g * SSD_HEADS_PER_GROUP + r
            seg = cum[:, off + h:off + h + 1] - cum_t[off + h:off + h + 1, :]
            dec = jnp.exp(jnp.where(mask, seg, NEG_INF))
            m = (cb * dec).astype(BF16)
            ys.append(jnp.dot(m, xdt[:, h * SSD_HEAD_DIM:(h + 1) * SSD_HEAD_DIM],
                              preferred_element_type=F32))
        y_diag = jnp.concatenate(ys, axis=1)
        st = st_ref[g]
        y_off = jnp.dot(cg, st.astype(BF16), preferred_element_type=F32) * expcum[:, g * gw:(g + 1) * gw]
        new = jnp.dot(bgt, xw[:, g * gw:(g + 1) * gw], preferred_element_type=F32)
        st_ref[g] = st * chunk_decay[:, g * gw:(g + 1) * gw] + new
        y_ref[0, :, g * gw:(g + 1) * gw] = (y_diag + y_off).astype(y_ref.dtype)


def _ssd_scan(xbc, proj, dtb_row, alog_row, expand, rev):
    b, l, _ = xbc.shape
    nc = l // SSD_CHUNK
    gs = SSD_GROUPS * SSD_STATE
    cidx = (lambda c: nc - 1 - c) if rev else (lambda c: c)
    return pl.pallas_call(
        functools.partial(_ssd_kernel, rev=rev),
        out_shape=jax.ShapeDtypeStruct((b, l, SSD_INNER), BF16),
        grid=(b, nc),
        in_specs=[pl.BlockSpec((1, SSD_CHUNK, SSD_INNER), lambda bi, c: (bi, cidx(c), 0)),
                  pl.BlockSpec((1, SSD_CHUNK, gs), lambda bi, c: (bi, cidx(c), SSD_INNER // gs)),
                  pl.BlockSpec((1, SSD_CHUNK, gs), lambda bi, c: (bi, cidx(c), SSD_INNER // gs + 1)),
                  pl.BlockSpec((1, SSD_CHUNK, LANES), lambda bi, c: (bi, cidx(c), COL_DT // LANES)),
                  pl.BlockSpec((1, LANES), lambda bi, c: (0, 0)),
                  pl.BlockSpec((1, LANES), lambda bi, c: (0, 0)),
                  pl.BlockSpec((LANES, SSD_INNER), lambda bi, c: (0, 0))],
        out_specs=pl.BlockSpec((1, SSD_CHUNK, SSD_INNER), lambda bi, c: (bi, cidx(c), 0)),
        scratch_shapes=[pltpu.VMEM((SSD_GROUPS, SSD_STATE, SSD_GROUP_WIDTH), F32)],
        compiler_params=_cparams(("parallel", "arbitrary")),
        name="ssd_scan_bwd" if rev else "ssd_scan_fwd",
    )(xbc, xbc, xbc, proj, dtb_row, alog_row, expand)


def _merge_kernel(x_ref, yf_ref, yb_ref, xs_ref, z_ref, ao_ref, ga_ref, gs_ref, g1_ref,
                  dsk_ref, snw_ref, wssd_ref, wattn_ref, wout_ref, o_ref):
    y = yf_ref[0].astype(F32) + yb_ref[0].astype(F32) + xs_ref[0].astype(F32) * dsk_ref[...]
    z = z_ref[0].astype(F32)
    y = y * (z * jax.nn.sigmoid(z))
    parts = []
    for g in range(SSD_GROUPS):
        yg = y[:, g * SSD_GROUP_WIDTH:(g + 1) * SSD_GROUP_WIDTH]
        parts.append(yg * lax.rsqrt(jnp.mean(yg * yg, axis=-1, keepdims=True) + EPS))
    y = jnp.concatenate(parts, axis=1) * snw_ref[...]
    ssd = jnp.dot(y.astype(BF16), wssd_ref[...], preferred_element_type=F32)
    attn = jnp.dot(ao_ref[0], wattn_ref[...], preferred_element_type=F32)
    merged = (jax.nn.sigmoid(ga_ref[0].astype(F32)) * attn
              + jax.nn.sigmoid(gs_ref[0].astype(F32)) * ssd)
    out = jnp.dot(merged.astype(BF16), wout_ref[...], preferred_element_type=F32)
    o_ref[0] = x_ref[0] + g1_ref[0] * out


def _merge(x, yf, yb, xbc, proj, attn_o, modr, dsk_full, snw, w_ssd, w_attn, w_out):
    b, l, _ = x.shape
    tl = min(l, 256)
    tok = lambda w, colb: pl.BlockSpec((1, tl, w), lambda bi, i: (bi, i, colb))
    full = lambda r, c: pl.BlockSpec((r, c), lambda bi, i: (0, 0))
    return pl.pallas_call(
        _merge_kernel,
        out_shape=jax.ShapeDtypeStruct((b, l, D_MODEL), F32),
        grid=(b, l // tl),
        in_specs=[tok(D_MODEL, 0), tok(SSD_INNER, 0), tok(SSD_INNER, 0), tok(SSD_INNER, 0),
                  tok(SSD_INNER, COL_Z // SSD_INNER), tok(ATTN_WIDTH, 0),
                  tok(D_MODEL, COL_GA // D_MODEL), tok(D_MODEL, COL_GS // D_MODEL),
                  pl.BlockSpec((1, 1, D_MODEL), lambda bi, i: (bi * N_MOD + 2, 0, 0)),
                  full(1, SSD_INNER), full(1, SSD_INNER),
                  full(SSD_INNER, D_MODEL), full(ATTN_WIDTH, D_MODEL), full(D_MODEL, D_MODEL)],
        out_specs=tok(D_MODEL, 0),
        compiler_params=_cparams(("parallel", "parallel")),
        name="merge_out_proj",
    )(x, yf, yb, xbc, proj, attn_o, proj, proj, modr, dsk_full, snw, w_ssd, w_attn, w_out)


def _topk_rows(s, k):
    rowf = lax.broadcasted_iota(I32, s.shape, 0).astype(F32)
    vals, ids = [], []
    for _ in range(k):
        m = jnp.max(s, axis=0, keepdims=True)
        am = jnp.min(jnp.where(s == m, rowf, float(s.shape[0])), axis=0, keepdims=True)
        vals.append(m)
        ids.append(am)
        s = jnp.where(rowf == am, -jnp.inf, s)
    return vals, ids


def _candidate_pieces():
    pieces = []
    for a in range(PEER_TOPK):
        nb = PEER_TOPK // (a + 1)
        if nb >= SUBLANES // 2:
            for b0 in range(0, nb, SUBLANES):
                pieces.append((a, 1, b0, SUBLANES))
    a_done = max(p[0] for p in pieces) + 1
    for b in range(PEER_TOPK):
        na = PEER_TOPK // (b + 1)
        if na > a_done:
            for a0 in range(0, na, SUBLANES):
                pieces.append((a0, SUBLANES, b, 1))
    return pieces, a_done


def _route_kernel(x_ref, sh_ref, sc_ref, nw_ref, wq_ref, keys_ref, hp_ref, idx_ref, gate_ref):
    h = _norm_mod(x_ref[0], nw_ref[...], sc_ref[0], sh_ref[0])
    hp_ref[0] = h
    qall = jnp.dot(h.astype(BF16), wq_ref[...], preferred_element_type=F32)
    ntok = qall.shape[0]
    tops, topi = [], []
    for c in range(2 * PEER_HEADS):
        qc = qall[:, c * PEER_HALF:(c + 1) * PEER_HALF].astype(BF16)
        s = lax.dot_general(keys_ref[c], qc, (((1,), (1,)), ((), ())), preferred_element_type=F32)
        vals, ids = _topk_rows(s, PEER_TOPK)
        tops.append(vals)
        topi.append(ids)
    pieces, a_done = _candidate_pieces()
    sub = lax.broadcasted_iota(I32, (SUBLANES, ntok), 0)
    subf = sub.astype(F32)
    pos_parts, drop_parts = [], []
    for a0, na, b0, nb in pieces:
        if na == 1:
            pos_parts.append(subf + float(a0 * PEER_TOPK + b0))
            keep = sub + b0 < PEER_TOPK // (a0 + 1)
        else:
            pos_parts.append(subf * float(PEER_TOPK) + float(a0 * PEER_TOPK + b0))
            keep = jnp.where(sub + a0 >= a_done, sub + a0, PEER_TOPK) < PEER_TOPK // (b0 + 1)
        drop_parts.append(jnp.where(keep, 0.0, -jnp.inf))
    pos = jnp.concatenate(pos_parts, axis=0)
    drop = jnp.concatenate(drop_parts, axis=0)
    idx_rows, gate_rows = [], []
    for hd in range(PEER_HEADS):
        s0 = jnp.concatenate(tops[2 * hd], axis=0)
        i0 = jnp.concatenate(topi[2 * hd], axis=0)
        s1 = jnp.concatenate(tops[2 * hd + 1], axis=0)
        i1 = jnp.concatenate(topi[2 * hd + 1], axis=0)
        cparts, iparts = [], []
        for a0, na, b0, nb in pieces:
            if na == 1:
                cparts.append(tops[2 * hd][a0] + s1[b0:b0 + nb])
                iparts.append(topi[2 * hd][a0] * float(N_KEYS) + i1[b0:b0 + nb])
            else:
                cparts.append(s0[a0:a0 + na] + tops[2 * hd + 1][b0])
                iparts.append(i0[a0:a0 + na] * float(N_KEYS) + topi[2 * hd + 1][b0])
        cand = jnp.concatenate(cparts, axis=0) + drop
        cidx = jnp.concatenate(iparts, axis=0)
        best = []
        for _ in range(PEER_TOPK):
            m = jnp.max(cand, axis=0, keepdims=True)
            first = jnp.min(jnp.where(cand == m, pos, float(PEER_TOPK * PEER_TOPK)), axis=0, keepdims=True)
            sel = pos == first
            idx_rows.append(jnp.max(jnp.where(sel, cidx, -1.0), axis=0, keepdims=True))
            best.append(m)
            cand = jnp.where(sel, -jnp.inf, cand)
        bs = jnp.concatenate(best, axis=0)
        p = jnp.exp(bs - best[0])
        gate_rows.append(p / jnp.sum(p, axis=0, keepdims=True))
    idx_t = jnp.concatenate(idx_rows, axis=0)
    gate_t = jnp.concatenate(gate_rows, axis=0)
    idx_ref[0] = (idx_t * float(ROW_WORDS)).T.astype(I32)
    gate_ref[0] = gate_t.T


def _route(x1, modr, norm_w, wq, keys):
    b, l, _ = x1.shape
    tl = min(l, 256)
    tok = lambda w: pl.BlockSpec((1, tl, w), lambda bi, i: (bi, i, 0))
    return pl.pallas_call(
        _route_kernel,
        out_shape=(jax.ShapeDtypeStruct((b, l, D_MODEL), F32),
                   jax.ShapeDtypeStruct((b, l, N_SEL), I32),
                   jax.ShapeDtypeStruct((b, l, N_SEL), F32)),
        grid=(b, l // tl),
        in_specs=[tok(D_MODEL),
                  pl.BlockSpec((1, 1, D_MODEL), lambda bi, i: (bi * N_MOD + 3, 0, 0)),
                  pl.BlockSpec((1, 1, D_MODEL), lambda bi, i: (bi * N_MOD + 4, 0, 0)),
                  pl.BlockSpec((1, D_MODEL), lambda bi, i: (0, 0)),
                  pl.BlockSpec((D_MODEL, PEER_HEADS * PEER_KEY_DIM), lambda bi, i: (0, 0)),
                  pl.BlockSpec((2 * PEER_HEADS, N_KEYS, PEER_HALF), lambda bi, i: (0, 0, 0))],
        out_specs=(tok(D_MODEL), tok(N_SEL), tok(N_SEL)),
        compiler_params=_cparams(("parallel", "parallel")),
        name="peer_route",
    )(x1, modr, modr, norm_w, wq, keys)


def _pack_rows(tab):
    n = tab.shape[0]
    bits = lax.bitcast_convert_type(tab.astype(BF16), jnp.uint16).astype(jnp.uint32)
    lo = bits[:, :D_MODEL // 2]
    hi = bits[:, D_MODEL // 2:]
    words = lax.bitcast_convert_type(lo | (hi << 16), I32)
    return words.reshape(n, ROW_WORDS, LANES)


def _gelu_tanh(x):
    return 0.5 * x * (1.0 + jnp.tanh(math.sqrt(2.0 / math.pi) * (x + 0.044715 * (x * x * x))))


def _gather_slabs(idx_ref, base, tab_ref, tile_ref, tt):
    for k in range(N_SEL):
        r = pl.multiple_of(idx_ref[base + k], ROW_WORDS)
        tile_ref[tt, k * ROW_WORDS:(k + 1) * ROW_WORDS, :] = tab_ref[pl.ds(r, ROW_WORDS), :]


def _unpack_rows(words):
    return pltpu.bitcast(words << 16, F32), pltpu.bitcast(words & jnp.int32(-65536), F32)


def _peer_act_kernel(idx_ref, x_ref, tab_ref, act_ref, tile, psum):
    nrows = N_SEL * ROW_WORDS
    half = SUBLANES // 2
    ones = jnp.ones((SUBLANES, LANES), BF16)
    nt = (((1,), (1,)), ((), ()))

    def group(gi, carry):
        t0 = pl.multiple_of(gi * PEER_TOK_GROUP, PEER_TOK_GROUP)
        acts = []
        for tt in range(PEER_TOK_GROUP):
            _gather_slabs(idx_ref, (t0 + tt) * N_SEL, tab_ref, tile, tt)
            x8 = x_ref[pl.ds(pl.multiple_of((t0 + tt) * SUBLANES, SUBLANES), SUBLANES), :]
            x_lo = jnp.concatenate([x8[:half], x8[:half]], axis=0)
            x_hi = jnp.concatenate([x8[half:], x8[half:]], axis=0)
            u_lo, u_hi = _unpack_rows(tile[tt])
            prod = (u_lo.reshape(nrows // SUBLANES, SUBLANES, LANES) * x_lo
                    + u_hi.reshape(nrows // SUBLANES, SUBLANES, LANES) * x_hi)
            part = prod + pltpu.roll(prod, 1, axis=1)
            part = part + pltpu.roll(part, 2, axis=1)
            psum[tt] = part.reshape(nrows, LANES)
            q = psum[tt, pl.ds(ROW_WORDS - 1, N_SEL, stride=ROW_WORDS), :]
            hi, lo = _split_bf16(q)
            a = (lax.dot_general(ones, hi, nt, preferred_element_type=F32)
                 + lax.dot_general(ones, lo, nt, preferred_element_type=F32))
            acts.append(a[:1])
        act_ref[pl.ds(t0, PEER_TOK_GROUP), :] = jnp.concatenate(acts, axis=0)
        return carry

    lax.fori_loop(0, PEER_BLOCK // PEER_TOK_GROUP, group, 0)


def _peer_sum_kernel(idx_ref, act_ref, gate_ref, tab_ref, rep_ref, ones_ref, out_ref, tile):
    nrows = N_SEL * ROW_WORDS
    half = SUBLANES // 2

    def group(gi, carry):
        t0 = pl.multiple_of(gi * PEER_TOK_GROUP, PEER_TOK_GROUP)
        w = gate_ref[pl.ds(t0, PEER_TOK_GROUP), :] * _gelu_tanh(act_ref[pl.ds(t0, PEER_TOK_GROUP), :])
        for tt in range(PEER_TOK_GROUP):
            _gather_slabs(idx_ref, (t0 + tt) * N_SEL, tab_ref, tile, tt)
            v_lo, v_hi = _unpack_rows(tile[tt])
            wrows = jnp.dot((rep_ref[...] * w[tt:tt + 1, :]).astype(BF16), ones_ref[...],
                            preferred_element_type=F32)
            acc_lo = jnp.sum((v_lo * wrows).reshape(nrows // SUBLANES, SUBLANES, LANES), axis=0)
            acc_hi = jnp.sum((v_hi * wrows).reshape(nrows // SUBLANES, SUBLANES, LANES), axis=0)
            o8 = jnp.concatenate([acc_lo[:half] + acc_lo[half:], acc_hi[:half] + acc_hi[half:]], axis=0)
            out_ref[pl.ds(pl.multiple_of((t0 + tt) * SUBLANES, SUBLANES), SUBLANES), :] = o8
        return carry

    lax.fori_loop(0, PEER_BLOCK // PEER_TOK_GROUP, group, 0)


def _peer_experts(idx_flat, x8, gates, tab_u, tab_v):
    t = gates.shape[0]
    idx_spec = pl.BlockSpec((PEER_BLOCK * N_SEL,), lambda i: (i,), memory_space=pltpu.SMEM)
    tab_spec = pl.BlockSpec(memory_space=pltpu.VMEM)
    tok_spec = pl.BlockSpec((PEER_BLOCK, N_SEL), lambda i: (i, 0))
    row_spec = pl.BlockSpec((PEER_BLOCK * SUBLANES, LANES), lambda i: (i, 0))
    const = lambda a: pl.BlockSpec(a.shape, lambda i: (0, 0))
    nrows = N_SEL * ROW_WORDS
    tile = pltpu.VMEM((PEER_TOK_GROUP, nrows, LANES), I32)
    act = pl.pallas_call(
        _peer_act_kernel,
        out_shape=jax.ShapeDtypeStruct((t, N_SEL), F32),
        grid=(t // PEER_BLOCK,),
        in_specs=[idx_spec, row_spec, tab_spec],
        out_specs=tok_spec,
        scratch_shapes=[tile, pltpu.VMEM((PEER_TOK_GROUP, nrows, LANES), F32)],
        compiler_params=_cparams(("parallel",)),
        name="peer_act",
    )(idx_flat, x8, tab_u)
    rep = (jnp.arange(nrows)[:, None] // ROW_WORDS == jnp.arange(N_SEL)[None, :]).astype(F32)
    ones = jnp.ones((N_SEL, LANES), BF16)
    return pl.pallas_call(
        _peer_sum_kernel,
        out_shape=jax.ShapeDtypeStruct((t * SUBLANES, LANES), F32),
        grid=(t // PEER_BLOCK,),
        in_specs=[idx_spec, tok_spec, tok_spec, tab_spec, const(rep), const(ones)],
        out_specs=row_spec,
        scratch_shapes=[tile],
        compiler_params=_cparams(("parallel",)),
        name="peer_sum",
    )(idx_flat, act, gates, tab_v, rep, ones)


def _final_kernel(x_ref, o_ref, g2_ref, y_ref):
    y_ref[0] = x_ref[0] + g2_ref[0] * o_ref[0]


def _final_residual(x1, outp, modr):
    b, l, _ = x1.shape
    tl = min(l, 512)
    tok = pl.BlockSpec((1, tl, D_MODEL), lambda bi, i: (bi, i, 0))
    return pl.pallas_call(
        _final_kernel,
        out_shape=jax.ShapeDtypeStruct((b, l, D_MODEL), F32),
        grid=(b, l // tl),
        in_specs=[tok, tok, pl.BlockSpec((1, 1, D_MODEL), lambda bi, i: (bi * N_MOD + 5, 0, 0))],
        out_specs=tok,
        compiler_params=_cparams(("parallel", "parallel")),
        name="peer_residual",
    )(x1, outp, modr)


def _prepare(rel_bias, ada_w, ada_b, norm1_w, norm2_w, w_in, q_norm_w, k_norm_w, attn_sink, conv_w,
             conv_b, a_log, dt_bias, d_skip, ssd_norm_w, w_attn_br, w_ssd_br, w_out, peer_wq,
             peer_keys, peer_u, peer_v):
    lyr = 0
    w = w_in[lyr]
    o = 0
    parts = {}
    for name, width in (("q", ATTN_WIDTH), ("k", KV_WIDTH), ("v", KV_WIDTH), ("z", SSD_INNER),
                        ("xbc", CONV_CH), ("dt", 2 * SSD_HEADS), ("ga", D_MODEL), ("gs", D_MODEL)):
        parts[name] = w[:, o:o + width]
        o += width
    pad = jnp.zeros((D_MODEL, PROJ_W - (COL_DT + 2 * SSD_HEADS)), w.dtype)
    w_all = jnp.concatenate([parts["z"], parts["q"], parts["ga"], parts["gs"], parts["xbc"],
                             parts["k"], parts["v"], parts["dt"], pad], axis=1).astype(BF16)
    lane_pad = LANES - 2 * SSD_HEADS
    expand = (jnp.arange(SSD_INNER)[None, :] // SSD_HEAD_DIM == jnp.arange(LANES)[:, None])
    pack = lambda tab: _pack_rows(tab).reshape(N_EXPERTS * ROW_WORDS, LANES)
    return dict(
        ada_w=ada_w[lyr], ada_b=ada_b[lyr],
        norm1_w=norm1_w[lyr].reshape(1, D_MODEL), norm2_w=norm2_w[lyr].reshape(1, D_MODEL),
        w_all=w_all,
        bias_tab=_bias_table(rel_bias), sink=attn_sink[lyr].astype(F32),
        qw=(jnp.tile(q_norm_w[lyr], N_Q_HEADS) * HEAD_DIM ** -0.5).reshape(1, ATTN_WIDTH),
        kw=jnp.tile(k_norm_w[lyr], N_KV_HEADS).reshape(1, KV_WIDTH),
        conv_w=conv_w[lyr], conv_b=conv_b[lyr],
        dtb_row=jnp.pad(dt_bias[lyr].astype(F32).reshape(1, -1), ((0, 0), (0, lane_pad))),
        alog_row=jnp.pad(a_log[lyr].astype(F32).reshape(1, -1), ((0, 0), (0, lane_pad))),
        expand_fwd=expand.astype(BF16),
        expand_bwd=jnp.roll(expand, SSD_HEADS, axis=0).astype(BF16),
        dsk_full=jnp.repeat(d_skip[lyr], SSD_HEAD_DIM).reshape(1, SSD_INNER),
        snw=ssd_norm_w[lyr].reshape(1, SSD_INNER),
        w_ssd=w_ssd_br[lyr].astype(BF16), w_attn=w_attn_br[lyr].astype(BF16), w_out=w_out[lyr].astype(BF16),
        wq=peer_wq[lyr].astype(BF16),
        keys=peer_keys[lyr].reshape(2 * PEER_HEADS, N_KEYS, PEER_HALF).astype(BF16),
        tab_u=pack(peer_u[lyr]), tab_v=pack(peer_v[lyr]),
    )


def _token_mixer_stage(x, modr, p):
    proj = _in_projection(x, modr, p["norm1_w"], p["w_all"])
    attn_o = _attention(proj, p["bias_tab"], p["sink"], p["qw"], p["kw"])
    xbc = _conv_silu(proj, p["conv_w"], p["conv_b"])
    yf = _ssd_scan(xbc, proj, p["dtb_row"], p["alog_row"], p["expand_fwd"], rev=False)
    yb = _ssd_scan(xbc, proj, p["dtb_row"], p["alog_row"], p["expand_bwd"], rev=True)
    return _merge(x, yf, yb, xbc, proj, attn_o, modr, p["dsk_full"], p["snw"],
                  p["w_ssd"], p["w_attn"], p["w_out"])


def _peer_stage(x1, modr, p):
    b, l, _ = x1.shape
    t = b * l
    hp, idx, gates = _route(x1, modr, p["norm2_w"], p["wq"], p["keys"])
    out = _peer_experts(idx.reshape(t * N_SEL), hp.reshape(t * SUBLANES, LANES), gates.reshape(t, N_SEL),
                        p["tab_u"], p["tab_v"])
    return _final_residual(x1, out.reshape(b, l, D_MODEL), modr)


def _encoder(x, c, p):
    nb = c.shape[0]
    modr = _modulation(c, p["ada_w"], p["ada_b"]).reshape(nb * N_MOD, 1, D_MODEL)
    x1 = _token_mixer_stage(x, modr, p)
    return _peer_stage(x1, modr, p)


def kernel(x_prompt, x_sample, c_prompt, c_sample, rel_bias, ada_w, ada_b, norm1_w, norm2_w, w_in,
           q_norm_w, k_norm_w, attn_sink, conv_w, conv_b, a_log, dt_bias, d_skip, ssd_norm_w,
           w_attn_br, w_ssd_br, w_out, peer_wq, peer_keys, peer_u, peer_v):
    p = _prepare(rel_bias, ada_w, ada_b, norm1_w, norm2_w, w_in, q_norm_w, k_norm_w, attn_sink,
                 conv_w, conv_b, a_log, dt_bias, d_skip, ssd_norm_w, w_attn_br, w_ssd_br, w_out,
                 peer_wq, peer_keys, peer_u, peer_v)
    return (_encoder(x_prompt, c_prompt, p), _encoder(x_sample, c_sample, p))
```

```python
import functools
import math

import jax
import jax.numpy as jnp
from jax import lax
from jax.experimental import pallas as pl
from jax.experimental.pallas import tpu as pltpu

F32 = jnp.float32
BF16 = jnp.bfloat16
I32 = jnp.int32

D_MODEL = 1024
HEAD_DIM = 64
N_Q_HEADS = 16
N_KV_HEADS = 4
Q_PER_KV = N_Q_HEADS // N_KV_HEADS
ATTN_WIDTH = N_Q_HEADS * HEAD_DIM
KV_WIDTH = N_KV_HEADS * HEAD_DIM
WINDOW = 128
BAND_BLOCK = 128
NUM_BUCKETS = 32
MAX_DISTANCE = 128
NEG_INF = -1e30

SSD_INNER = 2 * D_MODEL
SSD_HEAD_DIM = 64
SSD_HEADS = SSD_INNER // SSD_HEAD_DIM
SSD_GROUPS = 4
SSD_HEADS_PER_GROUP = SSD_HEADS // SSD_GROUPS
SSD_STATE = 128
SSD_CHUNK = 128
SSD_GROUP_WIDTH = SSD_INNER // SSD_GROUPS
CONV_WIDTH = 5
CONV_CH = SSD_INNER + 2 * SSD_GROUPS * SSD_STATE

PEER_HEADS = 8
PEER_KEY_DIM = 256
PEER_HALF = PEER_KEY_DIM // 2
N_KEYS = 128
N_EXPERTS = N_KEYS * N_KEYS
PEER_TOPK = 16
N_SEL = PEER_HEADS * PEER_TOPK

N_MOD = 6
EPS = 1e-6

COL_Z = 0
COL_Q = 2048
COL_GA = 3072
COL_GS = 4096
COL_XBC = 5120
COL_K = 8192
COL_V = 8448
COL_DT = 8704
PROJ_W = 9216

LANES = 128
SUBLANES = 8
VMEM_LIMIT = 56 * 1024 * 1024
ROW_WORDS = D_MODEL // (2 * LANES)
PEER_TOK_GROUP = 8
PEER_BLOCK = 128


def _cparams(sem):
    return pltpu.CompilerParams(dimension_semantics=sem, vmem_limit_bytes=VMEM_LIMIT)


def _split_bf16(v):
    hi = v.astype(BF16)
    lo = (v - hi.astype(F32)).astype(BF16)
    return hi, lo


def _dot2(v, m_bf16):
    hi, lo = _split_bf16(v)
    return (jnp.dot(hi, m_bf16, preferred_element_type=F32)
            + jnp.dot(lo, m_bf16, preferred_element_type=F32))


def _dot2_left(m_bf16, v):
    hi, lo = _split_bf16(v)
    return (jnp.dot(m_bf16, hi, preferred_element_type=F32)
            + jnp.dot(m_bf16, lo, preferred_element_type=F32))


def _mod_kernel(c_ref, w_ref, b_ref, o_ref):
    c = c_ref[...]
    sc = c * jax.nn.sigmoid(c)
    o_ref[...] = jnp.dot(sc, w_ref[...], preferred_element_type=F32,
                         precision=lax.Precision.HIGHEST) + b_ref[...]


def _modulation(c, ada_w, ada_b):
    nb = c.shape[0]
    n = ada_w.shape[1]
    tn = 1024
    return pl.pallas_call(
        _mod_kernel,
        out_shape=jax.ShapeDtypeStruct((nb, n), F32),
        grid=(n // tn,),
        in_specs=[pl.BlockSpec((nb, D_MODEL), lambda j: (0, 0)),
                  pl.BlockSpec((D_MODEL, tn), lambda j: (0, j)),
                  pl.BlockSpec((1, tn), lambda j: (0, j))],
        out_specs=pl.BlockSpec((nb, tn), lambda j: (0, j)),
        compiler_params=_cparams(("arbitrary",)),
        name="adaln_mod",
    )(c, ada_w, ada_b.reshape(1, n))


def _norm_mod(x, nw, sc, sh):
    ms = jnp.mean(x * x, axis=-1, keepdims=True)
    h = x * lax.rsqrt(ms + EPS) * nw
    return h * (1.0 + sc) + sh


def _inproj_kernel(x_ref, sh_ref, sc_ref, nw_ref, w_ref, o_ref, h_scr):
    @pl.when(pl.program_id(2) == 0)
    def _():
        h = _norm_mod(x_ref[0], nw_ref[...], sc_ref[0], sh_ref[0])
        h_scr[...] = h.astype(BF16)

    o_ref[0] = jnp.dot(h_scr[...], w_ref[...], preferred_element_type=F32).astype(o_ref.dtype)


def _in_projection(x, modr, norm_w, w_all):
    b, l, _ = x.shape
    tl = min(l, 1024)
    tn = 1024
    return pl.pallas_call(
        _inproj_kernel,
        out_shape=jax.ShapeDtypeStruct((b, l, PROJ_W), BF16),
        grid=(b, l // tl, PROJ_W // tn),
        in_specs=[pl.BlockSpec((1, tl, D_MODEL), lambda bi, i, j: (bi, i, 0)),
                  pl.BlockSpec((1, 1, D_MODEL), lambda bi, i, j: (bi * N_MOD + 0, 0, 0)),
                  pl.BlockSpec((1, 1, D_MODEL), lambda bi, i, j: (bi * N_MOD + 1, 0, 0)),
                  pl.BlockSpec((1, D_MODEL), lambda bi, i, j: (0, 0)),
                  pl.BlockSpec((D_MODEL, tn), lambda bi, i, j: (0, j))],
        out_specs=pl.BlockSpec((1, tl, tn), lambda bi, i, j: (bi, i, j)),
        scratch_shapes=[pltpu.VMEM((tl, D_MODEL), BF16)],
        compiler_params=_cparams(("parallel", "parallel", "arbitrary")),
        name="in_projection",
    )(x, modr, modr, norm_w, w_all)


def _head_sums(n_heads):
    c = jnp.arange(n_heads * HEAD_DIM)
    hsum = (c[:, None] // HEAD_DIM == jnp.arange(LANES)[None, :]).astype(BF16)
    return hsum, hsum.T


def _qk_norm(t, hsum, hexp, w_full):
    ssq = _dot2(t * t, hsum)
    inv = lax.rsqrt(ssq * (1.0 / HEAD_DIM) + EPS)
    return t * _dot2(inv, hexp) * w_full


def _attn_kernel(sink_ref, q_ref, kp_ref, kc_ref, kn_ref, vp_ref, vc_ref, vn_ref, bias_ref,
                 qw_ref, kw_ref, qsum_ref, qexp_ref, ksum_ref, kexp_ref, o_ref, s_scr, p_scr):
    q = _qk_norm(q_ref[0].astype(F32), qsum_ref[...], qexp_ref[...], qw_ref[...]).astype(BF16)
    k = jnp.concatenate([kp_ref[0], kc_ref[0], kn_ref[0]], axis=0).astype(F32)
    k = _qk_norm(k, ksum_ref[...], kexp_ref[...], kw_ref[...])
    v = jnp.concatenate([vp_ref[0], vc_ref[0], vn_ref[0]], axis=0).astype(F32)
    low = lax.broadcasted_iota(I32, (1, LANES), 1) < HEAD_DIM
    nt = (((1,), (1,)), ((), ()))
    k_sel, v_sel = [], []
    for c in range(N_KV_HEADS // 2):
        kslab = k[:, c * LANES:(c + 1) * LANES]
        vslab = v[:, c * LANES:(c + 1) * LANES]
        kroll = pltpu.roll(kslab, HEAD_DIM, axis=1)
        vroll = pltpu.roll(vslab, HEAD_DIM, axis=1)
        for e in range(2):
            k_sel.append((jnp.where(low, kroll if e else kslab, 0.0).astype(BF16),
                          jnp.where(low, 0.0, kslab if e else kroll).astype(BF16)))
            v_sel.append((jnp.where(low, vroll if e else vslab, 0.0).astype(BF16),
                          jnp.where(low, 0.0, vslab if e else vroll).astype(BF16)))
    for hd in range(N_Q_HEADS):
        qs = q[:, (hd // 2) * LANES:(hd // 2 + 1) * LANES]
        s = lax.dot_general(qs, k_sel[hd // Q_PER_KV][hd % 2], nt, preferred_element_type=F32)
        s_scr[hd] = s + bias_ref[0, hd]
    for hd in range(N_Q_HEADS):
        s = s_scr[hd]
        sk = sink_ref[hd]
        m = jnp.maximum(jnp.max(s, axis=-1, keepdims=True), sk)
        p = jnp.exp(s - m)
        denom = jnp.sum(p, axis=-1, keepdims=True) + jnp.exp(sk - m)
        p_scr[hd] = (p / denom).astype(BF16)
    for slab in range(N_Q_HEADS // 2):
        vs = v_sel[(2 * slab) // Q_PER_KV]
        acc = (jnp.dot(p_scr[2 * slab], vs[0], preferred_element_type=F32)
               + jnp.dot(p_scr[2 * slab + 1], vs[1], preferred_element_type=F32))
        o_ref[0, :, slab * LANES:(slab + 1) * LANES] = acc.astype(o_ref.dtype)


def _attention(proj, bias_tab, sink, qw_full, kw_full):
    b, l, _ = proj.shape
    nb = l // BAND_BLOCK
    assert nb >= 2
    kcol = COL_K // KV_WIDTH
    vcol = COL_V // KV_WIDTH
    qsum, qexp = _head_sums(N_Q_HEADS)
    ksum, kexp = _head_sums(N_KV_HEADS)

    def prev(i):
        return jnp.maximum(i - 1, 0)

    def nxt(i):
        return jnp.minimum(i + 1, nb - 1)

    def edge(i):
        return jnp.where(i == 0, 0, jnp.where(i == nb - 1, 2, 1))

    kv = lambda colb, f: pl.BlockSpec((1, BAND_BLOCK, KV_WIDTH), lambda bi, i: (bi, f(i), colb))
    same = lambda i: i
    const = lambda a: pl.BlockSpec(a.shape, lambda bi, i: (0,) * a.ndim)
    return pl.pallas_call(
        _attn_kernel,
        out_shape=jax.ShapeDtypeStruct((b, l, ATTN_WIDTH), BF16),
        grid=(b, nb),
        in_specs=[pl.BlockSpec(memory_space=pltpu.SMEM),
                  pl.BlockSpec((1, BAND_BLOCK, ATTN_WIDTH), lambda bi, i: (bi, i, COL_Q // ATTN_WIDTH)),
                  kv(kcol, prev), kv(kcol, same), kv(kcol, nxt),
                  kv(vcol, prev), kv(vcol, same), kv(vcol, nxt),
                  pl.BlockSpec((1, N_Q_HEADS, BAND_BLOCK, 3 * BAND_BLOCK), lambda bi, i: (edge(i), 0, 0, 0)),
                  const(qw_full), const(kw_full), const(qsum), const(qexp), const(ksum), const(kexp)],
        out_specs=pl.BlockSpec((1, BAND_BLOCK, ATTN_WIDTH), lambda bi, i: (bi, i, 0)),
        scratch_shapes=[pltpu.VMEM((N_Q_HEADS, BAND_BLOCK, 3 * BAND_BLOCK), F32),
                        pltpu.VMEM((N_Q_HEADS, BAND_BLOCK, 3 * BAND_BLOCK), BF16)],
        compiler_params=_cparams(("parallel", "parallel")),
        name="window_attention",
    )(sink, proj, proj, proj, proj, proj, proj, proj, bias_tab, qw_full, kw_full, qsum, qexp, ksum, kexp)


def _t5_bucket(rel):
    half = NUM_BUCKETS // 2
    max_exact = half // 2
    bucket = jnp.where(rel > 0, half, 0)
    n = jnp.abs(rel)
    nf = jnp.maximum(n, 1).astype(F32)
    large = max_exact + (jnp.log(nf / max_exact) / math.log(MAX_DISTANCE / max_exact)
                         * (half - max_exact)).astype(I32)
    large = jnp.minimum(large, half - 1)
    return bucket + jnp.where(n < max_exact, n, large)


def _bias_table(rel_bias):
    kpos = jnp.arange(3 * BAND_BLOCK)[None, :]
    rel = kpos - BAND_BLOCK - jnp.arange(BAND_BLOCK)[:, None]
    in_window = jnp.abs(rel) <= WINDOW
    bias = rel_bias[_t5_bucket(rel)].astype(F32).transpose(2, 0, 1)
    mid = jnp.where(in_window[None], bias, NEG_INF)
    first = jnp.where(kpos[None] < BAND_BLOCK, NEG_INF, mid)
    last = jnp.where(kpos[None] >= 2 * BAND_BLOCK, NEG_INF, mid)
    return jnp.stack([first, mid, last])


CONV_HALO = 16
CONV_TILE = 512


def _conv_kernel(prev_ref, cur_ref, next_ref, w_ref, b_ref, o_ref, *, nt):
    i = pl.program_id(1)
    tl = cur_ref.shape[1]
    cur = cur_ref[0].astype(F32)
    prev = prev_ref[0].astype(F32) * jnp.where(i > 0, 1.0, 0.0)
    nxt = next_ref[0].astype(F32) * jnp.where(i < nt - 1, 1.0, 0.0)
    ext = jnp.concatenate([prev, cur, nxt], axis=0)
    half = CONV_WIDTH // 2
    acc = jnp.broadcast_to(b_ref[...], cur.shape)
    for t in range(CONV_WIDTH):
        off = CONV_HALO - half + t
        acc = acc + ext[off:off + tl] * w_ref[t:t + 1, :]
    o_ref[0] = (acc * jax.nn.sigmoid(acc)).astype(o_ref.dtype)


def _conv_silu(proj, conv_w, conv_b):
    b, l, _ = proj.shape
    tl = min(l, CONV_TILE)
    nt = l // tl
    cw = CONV_TILE
    nch = CONV_CH // cw
    c0 = COL_XBC // cw
    hb = tl // CONV_HALO
    nh = l // CONV_HALO
    return pl.pallas_call(
        functools.partial(_conv_kernel, nt=nt),
        out_shape=jax.ShapeDtypeStruct((b, l, CONV_CH), BF16),
        grid=(b, nt, nch),
        in_specs=[pl.BlockSpec((1, CONV_HALO, cw), lambda bi, i, c: (bi, jnp.maximum(i * hb - 1, 0), c0 + c)),
                  pl.BlockSpec((1, tl, cw), lambda bi, i, c: (bi, i, c0 + c)),
                  pl.BlockSpec((1, CONV_HALO, cw), lambda bi, i, c: (bi, jnp.minimum((i + 1) * hb, nh - 1), c0 + c)),
                  pl.BlockSpec((CONV_WIDTH, cw), lambda bi, i, c: (0, c)),
                  pl.BlockSpec((1, cw), lambda bi, i, c: (0, c))],
        out_specs=pl.BlockSpec((1, tl, cw), lambda bi, i, c: (bi, i, c)),
        compiler_params=_cparams(("parallel", "parallel", "parallel")),
        name="conv_silu",
    )(proj, proj, proj, conv_w, conv_b.reshape(1, CONV_CH))


def _ssd_kernel(xs_ref, b_ref, c_ref, dt_ref, dtb_ref, alog_ref, e_ref, y_ref, st_ref, *, rev):
    @pl.when(pl.program_id(1) == 0)
    def _():
        st_ref[...] = jnp.zeros_like(st_ref)

    q = SSD_CHUNK
    off = SSD_HEADS if rev else 0
    z = dt_ref[0].astype(F32) + dtb_ref[...]
    dt = jnp.maximum(z, 0.0) + jnp.log1p(jnp.exp(-jnp.abs(z)))
    da = dt * (-jnp.exp(alog_ref[...]))
    ri = lax.broadcasted_iota(I32, (q, q), 0)
    ci = lax.broadcasted_iota(I32, (q, q), 1)
    if rev:
        mask = ci >= ri
        mask_t = ci <= ri
    else:
        mask = ci <= ri
        mask_t = ci >= ri
    tri = jnp.where(mask, 1.0, 0.0).astype(BF16)
    tri_t = jnp.where(mask_t, 1.0, 0.0).astype(BF16)
    cum = _dot2_left(tri, da)
    cum_t = _dot2(da.T, tri_t)
    tot = cum[0:1, :] if rev else cum[q - 1:q, :]
    e = e_ref[...]
    dt_full = _dot2(dt, e)
    cum_full = _dot2(cum, e)
    tot_full = _dot2(tot, e)
    xdt_f = xs_ref[0].astype(F32) * dt_full
    xdt = xdt_f.astype(BF16)
    xw = (xdt_f * jnp.exp(tot_full - cum_full)).astype(BF16)
    expcum = jnp.exp(cum_full)
    chunk_decay = jnp.exp(tot_full)
    gw = SSD_GROUP_WIDTH
    for g in range(SSD_GROUPS):
        bg = b_ref[0][:, g * SSD_STATE:(g + 1) * SSD_STATE]
        cg = c_ref[0][:, g * SSD_STATE:(g + 1) * SSD_STATE]
        bgt = bg.astype(F32).T.astype(BF16)
        cb = jnp.dot(cg, bgt, preferred_element_type=F32)
        ys = []
        for r in range(SSD_HEADS_PER_GROUP):
            h = g * SSD_HEADS_PER_GROUP + r
            seg = cum[:, off + h:off + h + 1] - cum_t[off + h:off + h + 1, :]
            dec = jnp.exp(jnp.where(mask, seg, NEG_INF))
            m = (cb * dec).astype(BF16)
            ys.append(jnp.dot(m, xdt[:, h * SSD_HEAD_DIM:(h + 1) * SSD_HEAD_DIM],
                              preferred_element_type=F32))
        y_diag = jnp.concatenate(ys, axis=1)
        st = st_ref[g]
        y_off = jnp.dot(cg, st.astype(BF16), preferred_element_type=F32) * expcum[:, g * gw:(g + 1) * gw]
        new = jnp.dot(bgt, xw[:, g * gw:(g + 1) * gw], preferred_element_type=F32)
        st_ref[g] = st * chunk_decay[:, g * gw:(g + 1) * gw] + new
        y_ref[0, :, g * gw:(g + 1) * gw] = (y_diag + y_off).astype(y_ref.dtype)


def _ssd_scan(xbc, proj, dtb_row, alog_row, expand, rev):
    b, l, _ = xbc.shape
    nc = l // SSD_CHUNK
    gs = SSD_GROUPS * SSD_STATE
    cidx = (lambda c: nc - 1 - c) if rev else (lambda c: c)
    return pl.pallas_call(
        functools.partial(_ssd_kernel, rev=rev),
        out_shape=jax.ShapeDtypeStruct((b, l, SSD_INNER), BF16),
        grid=(b, nc),
        in_specs=[pl.BlockSpec((1, SSD_CHUNK, SSD_INNER), lambda bi, c: (bi, cidx(c), 0)),
                  pl.BlockSpec((1, SSD_CHUNK, gs), lambda bi, c: (bi, cidx(c), SSD_INNER // gs)),
                  pl.BlockSpec((1, SSD_CHUNK, gs), lambda bi, c: (bi, cidx(c), SSD_INNER // gs + 1)),
                  pl.BlockSpec((1, SSD_CHUNK, LANES), lambda bi, c: (bi, cidx(c), COL_DT // LANES)),
                  pl.BlockSpec((1, LANES), lambda bi, c: (0, 0)),
                  pl.BlockSpec((1, LANES), lambda bi, c: (0, 0)),
                  pl.BlockSpec((LANES, SSD_INNER), lambda bi, c: (0, 0))],
        out_specs=pl.BlockSpec((1, SSD_CHUNK, SSD_INNER), lambda bi, c: (bi, cidx(c), 0)),
        scratch_shapes=[pltpu.VMEM((SSD_GROUPS, SSD_STATE, SSD_GROUP_WIDTH), F32)],
        compiler_params=_cparams(("parallel", "arbitrary")),
        name="ssd_scan_bwd" if rev else "ssd_scan_fwd",
    )(xbc, xbc, xbc, proj, dtb_row, alog_row, expand)


def _merge_kernel(x_ref, yf_ref, yb_ref, xs_ref, z_ref, ao_ref, ga_ref, gs_ref, g1_ref,
                  dsk_ref, snw_ref, wssd_ref, wattn_ref, wout_ref, o_ref):
    y = yf_ref[0].astype(F32) + yb_ref[0].astype(F32) + xs_ref[0].astype(F32) * dsk_ref[...]
    z = z_ref[0].astype(F32)
    y = y * (z * jax.nn.sigmoid(z))
    parts = []
    for g in range(SSD_GROUPS):
        yg = y[:, g * SSD_GROUP_WIDTH:(g + 1) * SSD_GROUP_WIDTH]
        parts.append(yg * lax.rsqrt(jnp.mean(yg * yg, axis=-1, keepdims=True) + EPS))
    y = jnp.concatenate(parts, axis=1) * snw_ref[...]
    ssd = jnp.dot(y.astype(BF16), wssd_ref[...], preferred_element_type=F32)
    attn = jnp.dot(ao_ref[0], wattn_ref[...], preferred_element_type=F32)
    merged = (jax.nn.sigmoid(ga_ref[0].astype(F32)) * attn
              + jax.nn.sigmoid(gs_ref[0].astype(F32)) * ssd)
    out = jnp.dot(merged.astype(BF16), wout_ref[...], preferred_element_type=F32)
    o_ref[0] = x_ref[0] + g1_ref[0] * out


def _merge(x, yf, yb, xbc, proj, attn_o, modr, dsk_full, snw, w_ssd, w_attn, w_out):
    b, l, _ = x.shape
    tl = min(l, 256)
    tok = lambda w, colb: pl.BlockSpec((1, tl, w), lambda bi, i: (bi, i, colb))
    full = lambda r, c: pl.BlockSpec((r, c), lambda bi, i: (0, 0))
    return pl.pallas_call(
        _merge_kernel,
        out_shape=jax.ShapeDtypeStruct((b, l, D_MODEL), F32),
        grid=(b, l // tl),
        in_specs=[tok(D_MODEL, 0), tok(SSD_INNER, 0), tok(SSD_INNER, 0), tok(SSD_INNER, 0),
                  tok(SSD_INNER, COL_Z // SSD_INNER), tok(ATTN_WIDTH, 0),
                  tok(D_MODEL, COL_GA // D_MODEL), tok(D_MODEL, COL_GS // D_MODEL),
                  pl.BlockSpec((1, 1, D_MODEL), lambda bi, i: (bi * N_MOD + 2, 0, 0)),
                  full(1, SSD_INNER), full(1, SSD_INNER),
                  full(SSD_INNER, D_MODEL), full(ATTN_WIDTH, D_MODEL), full(D_MODEL, D_MODEL)],
        out_specs=tok(D_MODEL, 0),
        compiler_params=_cparams(("parallel", "parallel")),
        name="merge_out_proj",
    )(x, yf, yb, xbc, proj, attn_o, proj, proj, modr, dsk_full, snw, w_ssd, w_attn, w_out)


def _topk_rows(s, k):
    rowf = lax.broadcasted_iota(I32, s.shape, 0).astype(F32)
    vals, ids = [], []
    for _ in range(k):
        m = jnp.max(s, axis=0, keepdims=True)
        am = jnp.min(jnp.where(s == m, rowf, float(s.shape[0])), axis=0, keepdims=True)
        vals.append(m)
        ids.append(am)
        s = jnp.where(rowf == am, -jnp.inf, s)
    return vals, ids


def _candidate_pieces():
    pieces = []
    for a in range(PEER_TOPK):
        nb = PEER_TOPK // (a + 1)
        if nb >= SUBLANES // 2:
            for b0 in range(0, nb, SUBLANES):
                pieces.append((a, 1, b0, SUBLANES))
    a_done = max(p[0] for p in pieces) + 1
    for b in range(PEER_TOPK):
        na = PEER_TOPK // (b + 1)
        if na > a_done:
            for a0 in range(0, na, SUBLANES):
                pieces.append((a0, SUBLANES, b, 1))
    return pieces, a_done


def _route_kernel(x_ref, sh_ref, sc_ref, nw_ref, wq_ref, keys_ref, hp_ref, idx_ref, gate_ref, cnt_ref):
    h = _norm_mod(x_ref[0], nw_ref[...], sc_ref[0], sh_ref[0])
    hp_ref[0] = h
    qall = jnp.dot(h.astype(BF16), wq_ref[...], preferred_element_type=F32)
    ntok = qall.shape[0]
    tops, topi = [], []
    for c in range(2 * PEER_HEADS):
        qc = qall[:, c * PEER_HALF:(c + 1) * PEER_HALF].astype(BF16)
        s = lax.dot_general(keys_ref[c], qc, (((1,), (1,)), ((), ())), preferred_element_type=F32)
        vals, ids = _topk_rows(s, PEER_TOPK)
        tops.append(vals)
        topi.append(ids)
    pieces, a_done = _candidate_pieces()
    sub = lax.broadcasted_iota(I32, (SUBLANES, ntok), 0)
    subf = sub.astype(F32)
    pos_parts, drop_parts = [], []
    for a0, na, b0, nb in pieces:
        if na == 1:
            pos_parts.append(subf + float(a0 * PEER_TOPK + b0))
            keep = sub + b0 < PEER_TOPK // (a0 + 1)
        else:
            pos_parts.append(subf * float(PEER_TOPK) + float(a0 * PEER_TOPK + b0))
            keep = jnp.where(sub + a0 >= a_done, sub + a0, PEER_TOPK) < PEER_TOPK // (b0 + 1)
        drop_parts.append(jnp.where(keep, 0.0, -jnp.inf))
    pos = jnp.concatenate(pos_parts, axis=0)
    drop = jnp.concatenate(drop_parts, axis=0)
    idx_rows, gate_rows = [], []
    for hd in range(PEER_HEADS):
        s0 = jnp.concatenate(tops[2 * hd], axis=0)
        i0 = jnp.concatenate(topi[2 * hd], axis=0)
        s1 = jnp.concatenate(tops[2 * hd + 1], axis=0)
        i1 = jnp.concatenate(topi[2 * hd + 1], axis=0)
        cparts, iparts = [], []
        for a0, na, b0, nb in pieces:
            if na == 1:
                cparts.append(tops[2 * hd][a0] + s1[b0:b0 + nb])
                iparts.append(topi[2 * hd][a0] * float(N_KEYS) + i1[b0:b0 + nb])
            else:
                cparts.append(s0[a0:a0 + na] + tops[2 * hd + 1][b0])
                iparts.append(i0[a0:a0 + na] * float(N_KEYS) + topi[2 * hd + 1][b0])
        cand = jnp.concatenate(cparts, axis=0) + drop
        cidx = jnp.concatenate(iparts, axis=0)
        best = []
        for _ in range(PEER_TOPK):
            m = jnp.max(cand, axis=0, keepdims=True)
            first = jnp.min(jnp.where(cand == m, pos, float(PEER_TOPK * PEER_TOPK)), axis=0, keepdims=True)
            sel = pos == first
            idx_rows.append(jnp.max(jnp.where(sel, cidx, -1.0), axis=0, keepdims=True))
            best.append(m)
            cand = jnp.where(sel, -jnp.inf, cand)
        bs = jnp.concatenate(best, axis=0)
        p = jnp.exp(bs - best[0])
        gate_rows.append(p / jnp.sum(p, axis=0, keepdims=True))
    idx_t = jnp.concatenate(idx_rows, axis=0)
    gate_t = jnp.concatenate(gate_rows, axis=0)
    idx_t, gate_t, n_zero = _partition_by_class(idx_t, gate_t)
    idx_ref[0] = (idx_t * float(SLAB_ROWS)).T.astype(I32)
    gate_ref[0] = gate_t.T
    cnt_ref[0, 0] = n_zero.astype(I32)


def _route(x1, modr, norm_w, wq, keys):
    b, l, _ = x1.shape
    tl = min(l, 256)
    tok = lambda w: pl.BlockSpec((1, tl, w), lambda bi, i: (bi, i, 0))
    return pl.pallas_call(
        _route_kernel,
        out_shape=(jax.ShapeDtypeStruct((b, l, D_MODEL), F32),
                   jax.ShapeDtypeStruct((b, l, N_SEL), I32),
                   jax.ShapeDtypeStruct((b, l, N_SEL), F32),
                   jax.ShapeDtypeStruct((b, l // tl, 1, tl), I32)),
        grid=(b, l // tl),
        in_specs=[tok(D_MODEL),
                  pl.BlockSpec((1, 1, D_MODEL), lambda bi, i: (bi * N_MOD + 3, 0, 0)),
                  pl.BlockSpec((1, 1, D_MODEL), lambda bi, i: (bi * N_MOD + 4, 0, 0)),
                  pl.BlockSpec((1, D_MODEL), lambda bi, i: (0, 0)),
                  pl.BlockSpec((D_MODEL, PEER_HEADS * PEER_KEY_DIM), lambda bi, i: (0, 0)),
                  pl.BlockSpec((2 * PEER_HEADS, N_KEYS, PEER_HALF), lambda bi, i: (0, 0, 0))],
        out_specs=(tok(D_MODEL), tok(N_SEL), tok(N_SEL),
                   pl.BlockSpec((1, 1, 1, tl), lambda bi, i: (bi, i, 0, 0))),
        compiler_params=_cparams(("parallel", "parallel")),
        name="peer_route",
    )(x1, modr, modr, norm_w, wq, keys)


def _pack_rows(tab):
    n = tab.shape[0]
    bits = lax.bitcast_convert_type(tab.astype(BF16), jnp.uint16).astype(jnp.uint32)
    lo = bits[:, :D_MODEL // 2]
    hi = bits[:, D_MODEL // 2:]
    words = lax.bitcast_convert_type(lo | (hi << 16), I32)
    return words.reshape(n, ROW_WORDS, LANES)


def _gelu_tanh(x):
    return 0.5 * x * (1.0 + jnp.tanh(math.sqrt(2.0 / math.pi) * (x + 0.044715 * (x * x * x))))


def _gather_slabs(idx_ref, base, tab_ref, tile_ref, tt):
    for k in range(N_SEL):
        r = pl.multiple_of(idx_ref[base + k], ROW_WORDS)
        tile_ref[tt, k * ROW_WORDS:(k + 1) * ROW_WORDS, :] = tab_ref[pl.ds(r, ROW_WORDS), :]


def _unpack_rows(words):
    return pltpu.bitcast(words << 16, F32), pltpu.bitcast(words & jnp.int32(-65536), F32)


def _peer_act_kernel(idx_ref, x_ref, tab_ref, act_ref, tile, psum):
    nrows = N_SEL * ROW_WORDS
    half = SUBLANES // 2
    ones = jnp.ones((SUBLANES, LANES), BF16)
    nt = (((1,), (1,)), ((), ()))

    def group(gi, carry):
        t0 = pl.multiple_of(gi * PEER_TOK_GROUP, PEER_TOK_GROUP)
        acts = []
        for tt in range(PEER_TOK_GROUP):
            _gather_slabs(idx_ref, (t0 + tt) * N_SEL, tab_ref, tile, tt)
            x8 = x_ref[pl.ds(pl.multiple_of((t0 + tt) * SUBLANES, SUBLANES), SUBLANES), :]
            x_lo = jnp.concatenate([x8[:half], x8[:half]], axis=0)
            x_hi = jnp.concatenate([x8[half:], x8[half:]], axis=0)
            u_lo, u_hi = _unpack_rows(tile[tt])
            prod = (u_lo.reshape(nrows // SUBLANES, SUBLANES, LANES) * x_lo
                    + u_hi.reshape(nrows // SUBLANES, SUBLANES, LANES) * x_hi)
            part = prod + pltpu.roll(prod, 1, axis=1)
            part = part + pltpu.roll(part, 2, axis=1)
            psum[tt] = part.reshape(nrows, LANES)
            q = psum[tt, pl.ds(ROW_WORDS - 1, N_SEL, stride=ROW_WORDS), :]
            hi, lo = _split_bf16(q)
            a = (lax.dot_general(ones, hi, nt, preferred_element_type=F32)
                 + lax.dot_general(ones, lo, nt, preferred_element_type=F32))
            acts.append(a[:1])
        act_ref[pl.ds(t0, PEER_TOK_GROUP), :] = jnp.concatenate(acts, axis=0)
        return carry

    lax.fori_loop(0, PEER_BLOCK // PEER_TOK_GROUP, group, 0)


def _peer_sum_kernel(idx_ref, act_ref, gate_ref, tab_ref, rep_ref, ones_ref, out_ref, tile):
    nrows = N_SEL * ROW_WORDS
    half = SUBLANES // 2

    def group(gi, carry):
        t0 = pl.multiple_of(gi * PEER_TOK_GROUP, PEER_TOK_GROUP)
        w = gate_ref[pl.ds(t0, PEER_TOK_GROUP), :] * _gelu_tanh(act_ref[pl.ds(t0, PEER_TOK_GROUP), :])
        for tt in range(PEER_TOK_GROUP):
            _gather_slabs(idx_ref, (t0 + tt) * N_SEL, tab_ref, tile, tt)
            v_lo, v_hi = _unpack_rows(tile[tt])
            wrows = jnp.dot((rep_ref[...] * w[tt:tt + 1, :]).astype(BF16), ones_ref[...],
                            preferred_element_type=F32)
            acc_lo = jnp.sum((v_lo * wrows).reshape(nrows // SUBLANES, SUBLANES, LANES), axis=0)
            acc_hi = jnp.sum((v_hi * wrows).reshape(nrows // SUBLANES, SUBLANES, LANES), axis=0)
            o8 = jnp.concatenate([acc_lo[:half] + acc_lo[half:], acc_hi[:half] + acc_hi[half:]], axis=0)
            out_ref[pl.ds(pl.multiple_of((t0 + tt) * SUBLANES, SUBLANES), SUBLANES), :] = o8
        return carry

    lax.fori_loop(0, PEER_BLOCK // PEER_TOK_GROUP, group, 0)


def _peer_experts(idx_flat, x8, gates, tab_u, tab_v):
    t = gates.shape[0]
    idx_spec = pl.BlockSpec((PEER_BLOCK * N_SEL,), lambda i: (i,), memory_space=pltpu.SMEM)
    tab_spec = pl.BlockSpec(memory_space=pltpu.VMEM)
    tok_spec = pl.BlockSpec((PEER_BLOCK, N_SEL), lambda i: (i, 0))
    row_spec = pl.BlockSpec((PEER_BLOCK * SUBLANES, LANES), lambda i: (i, 0))
    const = lambda a: pl.BlockSpec(a.shape, lambda i: (0, 0))
    nrows = N_SEL * ROW_WORDS
    tile = pltpu.VMEM((PEER_TOK_GROUP, nrows, LANES), I32)
    act = pl.pallas_call(
        _peer_act_kernel,
        out_shape=jax.ShapeDtypeStruct((t, N_SEL), F32),
        grid=(t // PEER_BLOCK,),
        in_specs=[idx_spec, row_spec, tab_spec],
        out_specs=tok_spec,
        scratch_shapes=[tile, pltpu.VMEM((PEER_TOK_GROUP, nrows, LANES), F32)],
        compiler_params=_cparams(("parallel",)),
        name="peer_act",
    )(idx_flat, x8, tab_u)
    rep = (jnp.arange(nrows)[:, None] // ROW_WORDS == jnp.arange(N_SEL)[None, :]).astype(F32)
    ones = jnp.ones((N_SEL, LANES), BF16)
    return pl.pallas_call(
        _peer_sum_kernel,
        out_shape=jax.ShapeDtypeStruct((t * SUBLANES, LANES), F32),
        grid=(t // PEER_BLOCK,),
        in_specs=[idx_spec, tok_spec, tok_spec, tab_spec, const(rep), const(ones)],
        out_specs=row_spec,
        scratch_shapes=[tile],
        compiler_params=_cparams(("parallel",)),
        name="peer_sum",
    )(idx_flat, act, gates, tab_v, rep, ones)


SLAB_ROWS = 2 * ROW_WORDS
PEER_CHUNK = 16
PEER_COMMON_CHUNKS = 5


def _expert_class(e):
    return lax.population_count(e) & 1


def _class_tables(tab_u, tab_v):
    slabs = jnp.stack([_pack_rows(tab_u), _pack_rows(tab_v)], axis=2).reshape(N_EXPERTS, SLAB_ROWS, LANES)
    even = 2 * jnp.arange(N_EXPERTS // 2, dtype=I32)
    first = even + _expert_class(even)
    second = even + 1 - _expert_class(even)
    shape = (N_EXPERTS // 2 * SLAB_ROWS, LANES)
    return slabs[first].reshape(shape), slabs[second].reshape(shape)


def _compact_rows(vals, shift, valid, up):
    n = valid.shape[0]
    for bit in range(n.bit_length() - 1):
        step = 1 << bit
        roll = (lambda a: pltpu.roll(a, (n - step) if up else step, axis=0))
        move = valid * ((shift >> bit) & 1)
        bring = roll(move) != 0
        vals = [jnp.where(bring, roll(v), v) for v in vals]
        shift = jnp.where(bring, roll(shift), shift)
        valid = jnp.where(bring, 1, valid - move)
    return vals


def _partition_by_class(idx_t, gate_t):
    n = idx_t.shape[0]
    e = idx_t.astype(I32)
    cls = _expert_class(e)
    local = (e >> 1).astype(F32)
    is_one = cls.astype(F32).astype(BF16)
    is_zero = (1 - cls).astype(F32).astype(BF16)
    ri = lax.broadcasted_iota(I32, (n, n), 0)
    ci = lax.broadcasted_iota(I32, (n, n), 1)
    before = jnp.where(ci < ri, 1.0, 0.0).astype(BF16)
    after = jnp.where(ci > ri, 1.0, 0.0).astype(BF16)
    ones_before = jnp.dot(before, is_one, preferred_element_type=F32).astype(I32)
    zeros_after = jnp.dot(after, is_zero, preferred_element_type=F32).astype(I32)
    n_zero = jnp.sum(is_zero.astype(F32), axis=0, keepdims=True)
    z_idx, z_gate = _compact_rows([local, gate_t], ones_before, 1 - cls, up=True)
    o_idx, o_gate = _compact_rows([local, gate_t], zeros_after, cls, up=False)
    first = lax.broadcasted_iota(I32, idx_t.shape, 0).astype(F32) < n_zero
    return jnp.where(first, z_idx, o_idx), jnp.where(first, z_gate, o_gate), n_zero


def _class_pass_parts(idx_ref, cnt_ref, x_ref, gate_ref, prev_ref, tab_ref, rep_ref, ones_ref, out_ref,
                      *, cls, s_lo, s_hi):
    lane = lax.broadcasted_iota(I32, (1, N_SEL), 1)
    in_range = jnp.logical_and(lane >= s_lo * PEER_CHUNK, lane < s_hi * PEER_CHUNK)
    crows = PEER_CHUNK * SLAB_ROWS
    nrows = (s_hi - s_lo) * PEER_CHUNK * ROW_WORDS
    half = SUBLANES // 2
    nt = (((1,), (1,)), ((), ()))
    ones8 = jnp.ones((SUBLANES, LANES), BF16)

    def gather_chunk(tile, base, c):
        k0 = base + c * PEER_CHUNK
        r0 = (c - s_lo) * crows
        r0 = r0 if isinstance(c, int) else pl.multiple_of(r0, crows)
        for kk in range(PEER_CHUNK):
            r = pl.multiple_of(idx_ref[k0 + kk], SLAB_ROWS)
            tile[pl.ds(r0 + kk * SLAB_ROWS, SLAB_ROWS), :] = tab_ref[pl.ds(r, SLAB_ROWS), :]

    def valid_row(n_zero):
        mine = (lane >= n_zero) if cls else (lane < n_zero)
        return jnp.where(jnp.logical_and(mine, in_range), 1.0, 0.0)

    def needed_chunks(n_zero):
        if cls:
            return jnp.maximum(n_zero // PEER_CHUNK, s_lo), s_hi
        return s_lo, jnp.minimum((n_zero + PEER_CHUNK - 1) // PEER_CHUNK, s_hi)

    def rows8(t):
        return pl.ds(pl.multiple_of(t * SUBLANES, SUBLANES), SUBLANES)

    def activation(tile, psum, t):
        x8 = x_ref[rows8(t), :]
        x_lo = jnp.concatenate([x8[:half], x8[:half]], axis=0)
        x_hi = jnp.concatenate([x8[half:], x8[half:]], axis=0)
        u_lo, u_hi = _unpack_rows(tile[pl.ds(0, nrows, stride=2), :])
        prod = (u_lo.reshape(nrows // SUBLANES, SUBLANES, LANES) * x_lo
                + u_hi.reshape(nrows // SUBLANES, SUBLANES, LANES) * x_hi)
        part = prod + pltpu.roll(prod, 1, axis=1)
        part = part + pltpu.roll(part, 2, axis=1)
        psum[pl.ds(s_lo * PEER_CHUNK * ROW_WORDS, nrows), :] = part.reshape(nrows, LANES)
        q = psum[pl.ds(ROW_WORDS - 1, N_SEL, stride=ROW_WORDS), :]
        hi, lo = _split_bf16(q)
        return (lax.dot_general(ones8, hi, nt, preferred_element_type=F32)
                + lax.dot_general(ones8, lo, nt, preferred_element_type=F32))

    def weighted_sum(tile, w_row, t):
        v_lo, v_hi = _unpack_rows(tile[pl.ds(1, nrows, stride=2), :])
        wrows = jnp.dot((rep_ref[...] * w_row).astype(BF16), ones_ref[...], preferred_element_type=F32)
        acc_lo = jnp.sum((v_lo * wrows).reshape(nrows // SUBLANES, SUBLANES, LANES), axis=0)
        acc_hi = jnp.sum((v_hi * wrows).reshape(nrows // SUBLANES, SUBLANES, LANES), axis=0)
        o8 = jnp.concatenate([acc_lo[:half] + acc_lo[half:], acc_hi[:half] + acc_hi[half:]], axis=0)
        sl = rows8(t)
        out_ref[sl, :] = o8 if prev_ref is None else prev_ref[sl, :] + o8

    return gather_chunk, valid_row, needed_chunks, activation, weighted_sum


def _class_common_body(idx_ref, cnt_ref, x_ref, gate_ref, prev_ref, tab_ref, rep_ref, ones_ref, out_ref,
                       tiles, psums, wbufs, *, cls, s_lo, s_hi):
    gather_chunk, valid_row, _, activation, weighted_sum = _class_pass_parts(
        idx_ref, cnt_ref, x_ref, gate_ref, prev_ref, tab_ref, rep_ref, ones_ref, out_ref,
        cls=cls, s_lo=s_lo, s_hi=s_hi)

    @pl.when(pl.program_id(0) == 0)
    def _():
        for buf in tiles + psums:
            buf[...] = jnp.zeros_like(buf)

    def first_stage(g, par):
        t0 = pl.multiple_of(g * PEER_TOK_GROUP, PEER_TOK_GROUP)
        acts, valid = [], []

        def token(tt):
            valid.append(valid_row(cnt_ref[t0 + tt]))
            for c in range(s_lo, s_hi):
                gather_chunk(tiles[par].at[tt], (t0 + tt) * N_SEL, c)
            acts.append(activation(tiles[par].at[tt], psums[par].at[tt], t0 + tt)[:1])

        def finish():
            act = jnp.concatenate(acts, axis=0)
            wbufs[par][...] = (gate_ref[pl.ds(t0, PEER_TOK_GROUP), :] * _gelu_tanh(act)
                               * jnp.concatenate(valid, axis=0))

        return token, finish

    def second_stage(g, par):
        t0 = pl.multiple_of(g * PEER_TOK_GROUP, PEER_TOK_GROUP)
        w = wbufs[par][...]
        return lambda tt: weighted_sum(tiles[par].at[tt], w[tt:tt + 1, :], t0 + tt)

    def step(first, second):
        tok1, fin1 = first_stage(*first) if first else (None, None)
        tok2 = second_stage(*second) if second else None
        for tt in range(PEER_TOK_GROUP):
            if tok1:
                tok1(tt)
            if tok2:
                tok2(tt)
        if fin1:
            fin1()

    ngroups = PEER_BLOCK // PEER_TOK_GROUP
    step((0, 0), None)

    def pair(j, carry):
        step((2 * j + 1, 1), (2 * j, 0))
        step((2 * j + 2, 0), (2 * j + 1, 1))
        return carry

    lax.fori_loop(0, ngroups // 2 - 1, pair, 0)
    step((ngroups - 1, 1), (ngroups - 2, 0))
    step(None, (ngroups - 1, 1))


def _class_rare_body(idx_ref, cnt_ref, x_ref, gate_ref, prev_ref, tab_ref, rep_ref, ones_ref, out_ref,
                     tile, psum, *, cls, s_lo, s_hi):
    gather_chunk, valid_row, needed_chunks, activation, weighted_sum = _class_pass_parts(
        idx_ref, cnt_ref, x_ref, gate_ref, prev_ref, tab_ref, rep_ref, ones_ref, out_ref,
        cls=cls, s_lo=s_lo, s_hi=s_hi)

    @pl.when(pl.program_id(0) == 0)
    def _():
        tile[...] = jnp.zeros_like(tile)
        psum[...] = jnp.zeros_like(psum)

    out_ref[...] = prev_ref[...]

    def token(t, carry):
        n_zero = cnt_ref[t]
        needed = (n_zero < s_hi * PEER_CHUNK) if cls else (n_zero > s_lo * PEER_CHUNK)

        @pl.when(needed)
        def _():
            def chunk(c, carry2):
                gather_chunk(tile, t * N_SEL, c)
                return carry2
            lax.fori_loop(*needed_chunks(n_zero), chunk, 0)
            act = activation(tile, psum, t)[:1]
            w = gate_ref[pl.ds(t, 1), :] * _gelu_tanh(act) * valid_row(n_zero)
            weighted_sum(tile, w, t)
        return carry

    lax.fori_loop(0, PEER_BLOCK, token, 0)


def _class_first_kernel(idx_ref, cnt_ref, x_ref, gate_ref, tab_ref, rep_ref, ones_ref, out_ref,
                        tile0, tile1, psum0, psum1, wbuf0, wbuf1, **static):
    _class_common_body(idx_ref, cnt_ref, x_ref, gate_ref, None, tab_ref, rep_ref, ones_ref, out_ref,
                       [tile0, tile1], [psum0, psum1], [wbuf0, wbuf1], **static)


def _class_next_kernel(idx_ref, cnt_ref, x_ref, gate_ref, prev_ref, tab_ref, rep_ref, ones_ref, out_ref,
                       tile0, tile1, psum0, psum1, wbuf0, wbuf1, **static):
    _class_common_body(idx_ref, cnt_ref, x_ref, gate_ref, prev_ref, tab_ref, rep_ref, ones_ref, out_ref,
                       [tile0, tile1], [psum0, psum1], [wbuf0, wbuf1], **static)


def _class_rare_kernel(idx_ref, cnt_ref, x_ref, gate_ref, prev_ref, tab_ref, rep_ref, ones_ref, out_ref,
                       tile, psum, **static):
    _class_rare_body(idx_ref, cnt_ref, x_ref, gate_ref, prev_ref, tab_ref, rep_ref, ones_ref, out_ref,
                     tile, psum, **static)


def _class_pass(idx_flat, cnt_flat, x8, gates, prev, tab, *, cls, s_lo, s_hi, rare):
    t = gates.shape[0]
    nrows = (s_hi - s_lo) * PEER_CHUNK * ROW_WORDS
    rep = (jnp.arange(nrows)[:, None] // ROW_WORDS + s_lo * PEER_CHUNK
           == jnp.arange(N_SEL)[None, :]).astype(F32)
    ones = jnp.ones((N_SEL, LANES), BF16)
    const = lambda a: pl.BlockSpec(a.shape, lambda i: (0, 0))
    row_spec = pl.BlockSpec((PEER_BLOCK * SUBLANES, LANES), lambda i: (i, 0))
    in_specs = [pl.BlockSpec((PEER_BLOCK * N_SEL,), lambda i: (i,), memory_space=pltpu.SMEM),
                pl.BlockSpec((PEER_BLOCK,), lambda i: (i,), memory_space=pltpu.SMEM),
                row_spec, pl.BlockSpec((PEER_BLOCK, N_SEL), lambda i: (i, 0))]
    args = [idx_flat, cnt_flat, x8, gates]
    if prev is not None:
        in_specs.append(row_spec)
        args.append(prev)
    in_specs += [pl.BlockSpec(memory_space=pltpu.VMEM), const(rep), const(ones)]
    args += [tab, rep, ones]
    static = dict(cls=cls, s_lo=s_lo, s_hi=s_hi)
    tile_rows = 2 * nrows
    psum_rows = N_SEL * ROW_WORDS
    if rare:
        body = _class_rare_kernel
        scratch = [pltpu.VMEM((tile_rows, LANES), I32), pltpu.VMEM((psum_rows, LANES), F32)]
    else:
        body = _class_first_kernel if prev is None else _class_next_kernel
        tile = pltpu.VMEM((PEER_TOK_GROUP, tile_rows, LANES), I32)
        psum = pltpu.VMEM((PEER_TOK_GROUP, psum_rows, LANES), F32)
        wbuf = pltpu.VMEM((PEER_TOK_GROUP, N_SEL), F32)
        scratch = [tile, tile, psum, psum, wbuf, wbuf]
    return pl.pallas_call(
        functools.partial(body, **static),
        out_shape=jax.ShapeDtypeStruct((t * SUBLANES, LANES), F32),
        grid=(t // PEER_BLOCK,),
        in_specs=in_specs,
        out_specs=row_spec,
        scratch_shapes=scratch,
        compiler_params=_cparams(("arbitrary",)),
        name="peer_class%d_%s" % (cls, "rare" if rare else "common"),
    )(*args)


def _peer_experts_by_class(idx_flat, cnt_flat, x8, gates, tab0, tab1):
    nchunk = N_SEL // PEER_CHUNK
    split = nchunk - PEER_COMMON_CHUNKS
    args = (idx_flat, cnt_flat, x8, gates)
    out = _class_pass(*args, None, tab0, cls=0, s_lo=0, s_hi=PEER_COMMON_CHUNKS, rare=False)
    out = _class_pass(*args, out, tab1, cls=1, s_lo=split, s_hi=nchunk, rare=False)
    out = _class_pass(*args, out, tab0, cls=0, s_lo=PEER_COMMON_CHUNKS, s_hi=nchunk, rare=True)
    return _class_pass(*args, out, tab1, cls=1, s_lo=0, s_hi=split, rare=True)


def _final_kernel(x_ref, o_ref, g2_ref, y_ref):
    y_ref[0] = x_ref[0] + g2_ref[0] * o_ref[0]


def _final_residual(x1, outp, modr):
    b, l, _ = x1.shape
    tl = min(l, 512)
    tok = pl.BlockSpec((1, tl, D_MODEL), lambda bi, i: (bi, i, 0))
    return pl.pallas_call(
        _final_kernel,
        out_shape=jax.ShapeDtypeStruct((b, l, D_MODEL), F32),
        grid=(b, l // tl),
        in_specs=[tok, tok, pl.BlockSpec((1, 1, D_MODEL), lambda bi, i: (bi * N_MOD + 5, 0, 0))],
        out_specs=tok,
        compiler_params=_cparams(("parallel", "parallel")),
        name="peer_residual",
    )(x1, outp, modr)


def _prepare(rel_bias, ada_w, ada_b, norm1_w, norm2_w, w_in, q_norm_w, k_norm_w, attn_sink, conv_w,
             conv_b, a_log, dt_bias, d_skip, ssd_norm_w, w_attn_br, w_ssd_br, w_out, peer_wq,
             peer_keys, peer_u, peer_v):
    lyr = 0
    w = w_in[lyr]
    o = 0
    parts = {}
    for name, width in (("q", ATTN_WIDTH), ("k", KV_WIDTH), ("v", KV_WIDTH), ("z", SSD_INNER),
                        ("xbc", CONV_CH), ("dt", 2 * SSD_HEADS), ("ga", D_MODEL), ("gs", D_MODEL)):
        parts[name] = w[:, o:o + width]
        o += width
    pad = jnp.zeros((D_MODEL, PROJ_W - (COL_DT + 2 * SSD_HEADS)), w.dtype)
    w_all = jnp.concatenate([parts["z"], parts["q"], parts["ga"], parts["gs"], parts["xbc"],
                             parts["k"], parts["v"], parts["dt"], pad], axis=1).astype(BF16)
    lane_pad = LANES - 2 * SSD_HEADS
    expand = (jnp.arange(SSD_INNER)[None, :] // SSD_HEAD_DIM == jnp.arange(LANES)[:, None])
    tab0, tab1 = _class_tables(peer_u[lyr], peer_v[lyr])
    return dict(
        ada_w=ada_w[lyr], ada_b=ada_b[lyr],
        norm1_w=norm1_w[lyr].reshape(1, D_MODEL), norm2_w=norm2_w[lyr].reshape(1, D_MODEL),
        w_all=w_all,
        bias_tab=_bias_table(rel_bias), sink=attn_sink[lyr].astype(F32),
        qw=(jnp.tile(q_norm_w[lyr], N_Q_HEADS) * HEAD_DIM ** -0.5).reshape(1, ATTN_WIDTH),
        kw=jnp.tile(k_norm_w[lyr], N_KV_HEADS).reshape(1, KV_WIDTH),
        conv_w=conv_w[lyr], conv_b=conv_b[lyr],
        dtb_row=jnp.pad(dt_bias[lyr].astype(F32).reshape(1, -1), ((0, 0), (0, lane_pad))),
        alog_row=jnp.pad(a_log[lyr].astype(F32).reshape(1, -1), ((0, 0), (0, lane_pad))),
        expand_fwd=expand.astype(BF16),
        expand_bwd=jnp.roll(expand, SSD_HEADS, axis=0).astype(BF16),
        dsk_full=jnp.repeat(d_skip[lyr], SSD_HEAD_DIM).reshape(1, SSD_INNER),
        snw=ssd_norm_w[lyr].reshape(1, SSD_INNER),
        w_ssd=w_ssd_br[lyr].astype(BF16), w_attn=w_attn_br[lyr].astype(BF16), w_out=w_out[lyr].astype(BF16),
        wq=peer_wq[lyr].astype(BF16),
        keys=peer_keys[lyr].reshape(2 * PEER_HEADS, N_KEYS, PEER_HALF).astype(BF16),
        tab0=tab0, tab1=tab1,
    )


def _token_mixer_stage(x, modr, p):
    proj = _in_projection(x, modr, p["norm1_w"], p["w_all"])
    attn_o = _attention(proj, p["bias_tab"], p["sink"], p["qw"], p["kw"])
    xbc = _conv_silu(proj, p["conv_w"], p["conv_b"])
    yf = _ssd_scan(xbc, proj, p["dtb_row"], p["alog_row"], p["expand_fwd"], rev=False)
    yb = _ssd_scan(xbc, proj, p["dtb_row"], p["alog_row"], p["expand_bwd"], rev=True)
    return _merge(x, yf, yb, xbc, proj, attn_o, modr, p["dsk_full"], p["snw"],
                  p["w_ssd"], p["w_attn"], p["w_out"])


def _peer_stage(x1, modr, p):
    b, l, _ = x1.shape
    t = b * l
    hp, idx, gates, cnt = _route(x1, modr, p["norm2_w"], p["wq"], p["keys"])
    out = _peer_experts_by_class(idx.reshape(t * N_SEL), cnt.reshape(t), hp.reshape(t * SUBLANES, LANES),
                                 gates.reshape(t, N_SEL), p["tab0"], p["tab1"])
    return _final_residual(x1, out.reshape(b, l, D_MODEL), modr)


def _encoder(x, c, p):
    nb = c.shape[0]
    modr = _modulation(c, p["ada_w"], p["ada_b"]).reshape(nb * N_MOD, 1, D_MODEL)
    x1 = _token_mixer_stage(x, modr, p)
    return _peer_stage(x1, modr, p)


def kernel(x_prompt, x_sample, c_prompt, c_sample, rel_bias, ada_w, ada_b, norm1_w, norm2_w, w_in,
           q_norm_w, k_norm_w, attn_sink, conv_w, conv_b, a_log, dt_bias, d_skip, ssd_norm_w,
           w_attn_br, w_ssd_br, w_out, peer_wq, peer_keys, peer_u, peer_v):
    p = _prepare(rel_bias, ada_w, ada_b, norm1_w, norm2_w, w_in, q_norm_w, k_norm_w, attn_sink,
                 conv_w, conv_b, a_log, dt_bias, d_skip, ssd_norm_w, w_attn_br, w_ssd_br, w_out,
                 peer_wq, peer_keys, peer_u, peer_v)
    return (_encoder(x_prompt, c_prompt, p), _encoder(x_sample, c_sample, p))
```

```python
import functools
import math

import jax
import jax.numpy as jnp
from jax import lax
from jax.experimental import pallas as pl
from jax.experimental.pallas import tpu as pltpu

F32 = jnp.float32
BF16 = jnp.bfloat16
I32 = jnp.int32

D_MODEL = 1024
HEAD_DIM = 64
N_Q_HEADS = 16
N_KV_HEADS = 4
Q_PER_KV = N_Q_HEADS // N_KV_HEADS
ATTN_WIDTH = N_Q_HEADS * HEAD_DIM
KV_WIDTH = N_KV_HEADS * HEAD_DIM
WINDOW = 128
BAND_BLOCK = 128
NUM_BUCKETS = 32
MAX_DISTANCE = 128
NEG_INF = -1e30

SSD_INNER = 2 * D_MODEL
SSD_HEAD_DIM = 64
SSD_HEADS = SSD_INNER // SSD_HEAD_DIM
SSD_GROUPS = 4
SSD_HEADS_PER_GROUP = SSD_HEADS // SSD_GROUPS
SSD_STATE = 128
SSD_CHUNK = 128
SSD_GROUP_WIDTH = SSD_INNER // SSD_GROUPS
CONV_WIDTH = 5
CONV_CH = SSD_INNER + 2 * SSD_GROUPS * SSD_STATE

PEER_HEADS = 8
PEER_KEY_DIM = 256
PEER_HALF = PEER_KEY_DIM // 2
N_KEYS = 128
N_EXPERTS = N_KEYS * N_KEYS
PEER_TOPK = 16
N_SEL = PEER_HEADS * PEER_TOPK

N_MOD = 6
EPS = 1e-6

COL_Z = 0
COL_Q = 2048
COL_GA = 3072
COL_GS = 4096
COL_XBC = 5120
COL_K = 8192
COL_V = 8448
COL_DT = 8704
PROJ_W = 9216

LANES = 128
SUBLANES = 8
VMEM_LIMIT = 56 * 1024 * 1024
ROW_WORDS = D_MODEL // (2 * LANES)
PEER_TOK_GROUP = 8
PEER_BLOCK = 128


def _cparams(sem):
    return pltpu.CompilerParams(dimension_semantics=sem, vmem_limit_bytes=VMEM_LIMIT)


def _split_bf16(v):
    hi = v.astype(BF16)
    lo = (v - hi.astype(F32)).astype(BF16)
    return hi, lo


def _dot2(v, m_bf16):
    hi, lo = _split_bf16(v)
    return (jnp.dot(hi, m_bf16, preferred_element_type=F32)
            + jnp.dot(lo, m_bf16, preferred_element_type=F32))


def _dot2_left(m_bf16, v):
    hi, lo = _split_bf16(v)
    return (jnp.dot(m_bf16, hi, preferred_element_type=F32)
            + jnp.dot(m_bf16, lo, preferred_element_type=F32))


def _mod_kernel(c_ref, w_ref, b_ref, o_ref):
    c = c_ref[...]
    sc = c * jax.nn.sigmoid(c)
    o_ref[...] = jnp.dot(sc, w_ref[...], preferred_element_type=F32,
                         precision=lax.Precision.HIGHEST) + b_ref[...]


def _modulation(c, ada_w, ada_b):
    nb = c.shape[0]
    n = ada_w.shape[1]
    tn = 1024
    return pl.pallas_call(
        _mod_kernel,
        out_shape=jax.ShapeDtypeStruct((nb, n), F32),
        grid=(n // tn,),
        in_specs=[pl.BlockSpec((nb, D_MODEL), lambda j: (0, 0)),
                  pl.BlockSpec((D_MODEL, tn), lambda j: (0, j)),
                  pl.BlockSpec((1, tn), lambda j: (0, j))],
        out_specs=pl.BlockSpec((nb, tn), lambda j: (0, j)),
        compiler_params=_cparams(("arbitrary",)),
        name="adaln_mod",
    )(c, ada_w, ada_b.reshape(1, n))


def _norm_mod(x, nw, sc, sh):
    ms = jnp.mean(x * x, axis=-1, keepdims=True)
    h = x * lax.rsqrt(ms + EPS) * nw
    return h * (1.0 + sc) + sh


def _inproj_kernel(x_ref, sh_ref, sc_ref, nw_ref, w_ref, o_ref, h_scr):
    @pl.when(pl.program_id(2) == 0)
    def _():
        h = _norm_mod(x_ref[0], nw_ref[...], sc_ref[0], sh_ref[0])
        h_scr[...] = h.astype(BF16)

    o_ref[0] = jnp.dot(h_scr[...], w_ref[...], preferred_element_type=F32).astype(o_ref.dtype)


def _in_projection(x, modr, norm_w, w_all):
    b, l, _ = x.shape
    tl = min(l, 1024)
    tn = 1024
    return pl.pallas_call(
        _inproj_kernel,
        out_shape=jax.ShapeDtypeStruct((b, l, PROJ_W), BF16),
        grid=(b, l // tl, PROJ_W // tn),
        in_specs=[pl.BlockSpec((1, tl, D_MODEL), lambda bi, i, j: (bi, i, 0)),
                  pl.BlockSpec((1, 1, D_MODEL), lambda bi, i, j: (bi * N_MOD + 0, 0, 0)),
                  pl.BlockSpec((1, 1, D_MODEL), lambda bi, i, j: (bi * N_MOD + 1, 0, 0)),
                  pl.BlockSpec((1, D_MODEL), lambda bi, i, j: (0, 0)),
                  pl.BlockSpec((D_MODEL, tn), lambda bi, i, j: (0, j))],
        out_specs=pl.BlockSpec((1, tl, tn), lambda bi, i, j: (bi, i, j)),
        scratch_shapes=[pltpu.VMEM((tl, D_MODEL), BF16)],
        compiler_params=_cparams(("parallel", "parallel", "arbitrary")),
        name="in_projection",
    )(x, modr, modr, norm_w, w_all)


def _head_sums(n_heads):
    c = jnp.arange(n_heads * HEAD_DIM)
    hsum = (c[:, None] // HEAD_DIM == jnp.arange(LANES)[None, :]).astype(BF16)
    return hsum, hsum.T


def _qk_norm(t, hsum, hexp, w_full):
    ssq = _dot2(t * t, hsum)
    inv = lax.rsqrt(ssq * (1.0 / HEAD_DIM) + EPS)
    return t * _dot2(inv, hexp) * w_full


def _attn_kernel(sink_ref, q_ref, kp_ref, kc_ref, kn_ref, vp_ref, vc_ref, vn_ref, bias_ref,
                 qw_ref, kw_ref, qsum_ref, qexp_ref, ksum_ref, kexp_ref, o_ref, s_scr, p_scr):
    q = _qk_norm(q_ref[0].astype(F32), qsum_ref[...], qexp_ref[...], qw_ref[...]).astype(BF16)
    k = jnp.concatenate([kp_ref[0], kc_ref[0], kn_ref[0]], axis=0).astype(F32)
    k = _qk_norm(k, ksum_ref[...], kexp_ref[...], kw_ref[...])
    v = jnp.concatenate([vp_ref[0], vc_ref[0], vn_ref[0]], axis=0).astype(F32)
    low = lax.broadcasted_iota(I32, (1, LANES), 1) < HEAD_DIM
    nt = (((1,), (1,)), ((), ()))
    k_sel, v_sel = [], []
    for c in range(N_KV_HEADS // 2):
        kslab = k[:, c * LANES:(c + 1) * LANES]
        vslab = v[:, c * LANES:(c + 1) * LANES]
        kroll = pltpu.roll(kslab, HEAD_DIM, axis=1)
        vroll = pltpu.roll(vslab, HEAD_DIM, axis=1)
        for e in range(2):
            k_sel.append((jnp.where(low, kroll if e else kslab, 0.0).astype(BF16),
                          jnp.where(low, 0.0, kslab if e else kroll).astype(BF16)))
            v_sel.append((jnp.where(low, vroll if e else vslab, 0.0).astype(BF16),
                          jnp.where(low, 0.0, vslab if e else vroll).astype(BF16)))
    for hd in range(N_Q_HEADS):
        qs = q[:, (hd // 2) * LANES:(hd // 2 + 1) * LANES]
        s = lax.dot_general(qs, k_sel[hd // Q_PER_KV][hd % 2], nt, preferred_element_type=F32)
        s_scr[hd] = s + bias_ref[0, hd]
    for hd in range(N_Q_HEADS):
        s = s_scr[hd]
        sk = sink_ref[hd]
        m = jnp.maximum(jnp.max(s, axis=-1, keepdims=True), sk)
        p = jnp.exp(s - m)
        denom = jnp.sum(p, axis=-1, keepdims=True) + jnp.exp(sk - m)
        p_scr[hd] = (p / denom).astype(BF16)
    for slab in range(N_Q_HEADS // 2):
        vs = v_sel[(2 * slab) // Q_PER_KV]
        acc = (jnp.dot(p_scr[2 * slab], vs[0], preferred_element_type=F32)
               + jnp.dot(p_scr[2 * slab + 1], vs[1], preferred_element_type=F32))
        o_ref[0, :, slab * LANES:(slab + 1) * LANES] = acc.astype(o_ref.dtype)


def _attention(proj, bias_tab, sink, qw_full, kw_full):
    b, l, _ = proj.shape
    nb = l // BAND_BLOCK
    assert nb >= 2
    kcol = COL_K // KV_WIDTH
    vcol = COL_V // KV_WIDTH
    qsum, qexp = _head_sums(N_Q_HEADS)
    ksum, kexp = _head_sums(N_KV_HEADS)

    def prev(i):
        return jnp.maximum(i - 1, 0)

    def nxt(i):
        return jnp.minimum(i + 1, nb - 1)

    def edge(i):
        return jnp.where(i == 0, 0, jnp.where(i == nb - 1, 2, 1))

    kv = lambda colb, f: pl.BlockSpec((1, BAND_BLOCK, KV_WIDTH), lambda bi, i: (bi, f(i), colb))
    same = lambda i: i
    const = lambda a: pl.BlockSpec(a.shape, lambda bi, i: (0,) * a.ndim)
    return pl.pallas_call(
        _attn_kernel,
        out_shape=jax.ShapeDtypeStruct((b, l, ATTN_WIDTH), BF16),
        grid=(b, nb),
        in_specs=[pl.BlockSpec(memory_space=pltpu.SMEM),
                  pl.BlockSpec((1, BAND_BLOCK, ATTN_WIDTH), lambda bi, i: (bi, i, COL_Q // ATTN_WIDTH)),
                  kv(kcol, prev), kv(kcol, same), kv(kcol, nxt),
                  kv(vcol, prev), kv(vcol, same), kv(vcol, nxt),
                  pl.BlockSpec((1, N_Q_HEADS, BAND_BLOCK, 3 * BAND_BLOCK), lambda bi, i: (edge(i), 0, 0, 0)),
                  const(qw_full), const(kw_full), const(qsum), const(qexp), const(ksum), const(kexp)],
        out_specs=pl.BlockSpec((1, BAND_BLOCK, ATTN_WIDTH), lambda bi, i: (bi, i, 0)),
        scratch_shapes=[pltpu.VMEM((N_Q_HEADS, BAND_BLOCK, 3 * BAND_BLOCK), F32),
                        pltpu.VMEM((N_Q_HEADS, BAND_BLOCK, 3 * BAND_BLOCK), BF16)],
        compiler_params=_cparams(("parallel", "parallel")),
        name="window_attention",
    )(sink, proj, proj, proj, proj, proj, proj, proj, bias_tab, qw_full, kw_full, qsum, qexp, ksum, kexp)


def _t5_bucket(rel):
    half = NUM_BUCKETS // 2
    max_exact = half // 2
    bucket = jnp.where(rel > 0, half, 0)
    n = jnp.abs(rel)
    nf = jnp.maximum(n, 1).astype(F32)
    large = max_exact + (jnp.log(nf / max_exact) / math.log(MAX_DISTANCE / max_exact)
                         * (half - max_exact)).astype(I32)
    large = jnp.minimum(large, half - 1)
    return bucket + jnp.where(n < max_exact, n, large)


def _bias_table(rel_bias):
    kpos = jnp.arange(3 * BAND_BLOCK)[None, :]
    rel = kpos - BAND_BLOCK - jnp.arange(BAND_BLOCK)[:, None]
    in_window = jnp.abs(rel) <= WINDOW
    bias = rel_bias[_t5_bucket(rel)].astype(F32).transpose(2, 0, 1)
    mid = jnp.where(in_window[None], bias, NEG_INF)
    first = jnp.where(kpos[None] < BAND_BLOCK, NEG_INF, mid)
    last = jnp.where(kpos[None] >= 2 * BAND_BLOCK, NEG_INF, mid)
    return jnp.stack([first, mid, last])


CONV_HALO = 16
CONV_TILE = 512


def _conv_kernel(prev_ref, cur_ref, next_ref, w_ref, b_ref, o_ref, *, nt):
    i = pl.program_id(1)
    tl = cur_ref.shape[1]
    cur = cur_ref[0].astype(F32)
    prev = prev_ref[0].astype(F32) * jnp.where(i > 0, 1.0, 0.0)
    nxt = next_ref[0].astype(F32) * jnp.where(i < nt - 1, 1.0, 0.0)
    ext = jnp.concatenate([prev, cur, nxt], axis=0)
    half = CONV_WIDTH // 2
    acc = jnp.broadcast_to(b_ref[...], cur.shape)
    for t in range(CONV_WIDTH):
        off = CONV_HALO - half + t
        acc = acc + ext[off:off + tl] * w_ref[t:t + 1, :]
    o_ref[0] = (acc * jax.nn.sigmoid(acc)).astype(o_ref.dtype)


def _conv_silu(proj, conv_w, conv_b):
    b, l, _ = proj.shape
    tl = min(l, CONV_TILE)
    nt = l // tl
    cw = CONV_TILE
    nch = CONV_CH // cw
    c0 = COL_XBC // cw
    hb = tl // CONV_HALO
    nh = l // CONV_HALO
    return pl.pallas_call(
        functools.partial(_conv_kernel, nt=nt),
        out_shape=jax.ShapeDtypeStruct((b, l, CONV_CH), BF16),
        grid=(b, nt, nch),
        in_specs=[pl.BlockSpec((1, CONV_HALO, cw), lambda bi, i, c: (bi, jnp.maximum(i * hb - 1, 0), c0 + c)),
                  pl.BlockSpec((1, tl, cw), lambda bi, i, c: (bi, i, c0 + c)),
                  pl.BlockSpec((1, CONV_HALO, cw), lambda bi, i, c: (bi, jnp.minimum((i + 1) * hb, nh - 1), c0 + c)),
                  pl.BlockSpec((CONV_WIDTH, cw), lambda bi, i, c: (0, c)),
                  pl.BlockSpec((1, cw), lambda bi, i, c: (0, c))],
        out_specs=pl.BlockSpec((1, tl, cw), lambda bi, i, c: (bi, i, c)),
        compiler_params=_cparams(("parallel", "parallel", "parallel")),
        name="conv_silu",
    )(proj, proj, proj, conv_w, conv_b.reshape(1, CONV_CH))


def _ssd_kernel(xs_ref, b_ref, c_ref, dt_ref, dtb_ref, alog_ref, e_ref, y_ref, st_ref, *, rev):
    @pl.when(pl.program_id(1) == 0)
    def _():
        st_ref[...] = jnp.zeros_like(st_ref)

    q = SSD_CHUNK
    off = SSD_HEADS if rev else 0
    z = dt_ref[0].astype(F32) + dtb_ref[...]
    dt = jnp.maximum(z, 0.0) + jnp.log1p(jnp.exp(-jnp.abs(z)))
    da = dt * (-jnp.exp(alog_ref[...]))
    ri = lax.broadcasted_iota(I32, (q, q), 0)
    ci = lax.broadcasted_iota(I32, (q, q), 1)
    if rev:
        mask = ci >= ri
        mask_t = ci <= ri
    else:
        mask = ci <= ri
        mask_t = ci >= ri
    tri = jnp.where(mask, 1.0, 0.0).astype(BF16)
    tri_t = jnp.where(mask_t, 1.0, 0.0).astype(BF16)
    cum = _dot2_left(tri, da)
    cum_t = _dot2(da.T, tri_t)
    tot = cum[0:1, :] if rev else cum[q - 1:q, :]
    e = e_ref[...]
    dt_full = _dot2(dt, e)
    cum_full = _dot2(cum, e)
    tot_full = _dot2(tot, e)
    xdt_f = xs_ref[0].astype(F32) * dt_full
    xdt = xdt_f.astype(BF16)
    xw = (xdt_f * jnp.exp(tot_full - cum_full)).astype(BF16)
    expcum = jnp.exp(cum_full)
    chunk_decay = jnp.exp(tot_full)
    gw = SSD_GROUP_WIDTH
    for g in range(SSD_GROUPS):
        bg = b_ref[0][:, g * SSD_STATE:(g + 1) * SSD_STATE]
        cg = c_ref[0][:, g * SSD_STATE:(g + 1) * SSD_STATE]
        bgt = bg.astype(F32).T.astype(BF16)
        cb = jnp.dot(cg, bgt, preferred_element_type=F32)
        ys = []
        for r in range(SSD_HEADS_PER_GROUP):
            h = g * SSD_HEADS_PER_GROUP + r
            seg = cum[:, off + h:off + h + 1] - cum_t[off + h:off + h + 1, :]
            dec = jnp.exp(jnp.where(mask, seg, NEG_INF))
            m = (cb * dec).astype(BF16)
            ys.append(jnp.dot(m, xdt[:, h * SSD_HEAD_DIM:(h + 1) * SSD_HEAD_DIM],
                              preferred_element_type=F32))
        y_diag = jnp.concatenate(ys, axis=1)
        st = st_ref[g]
        y_off = jnp.dot(cg, st.astype(BF16), preferred_element_type=F32) * expcum[:, g * gw:(g + 1) * gw]
        new = jnp.dot(bgt, xw[:, g * gw:(g + 1) * gw], preferred_element_type=F32)
        st_ref[g] = st * chunk_decay[:, g * gw:(g + 1) * gw] + new
        y_ref[0, :, g * gw:(g + 1) * gw] = (y_diag + y_off).astype(y_ref.dtype)


def _ssd_scan(xbc, proj, dtb_row, alog_row, expand, rev):
    b, l, _ = xbc.shape
    nc = l // SSD_CHUNK
    gs = SSD_GROUPS * SSD_STATE
    cidx = (lambda c: nc - 1 - c) if rev else (lambda c: c)
    return pl.pallas_call(
        functools.partial(_ssd_kernel, rev=rev),
        out_shape=jax.ShapeDtypeStruct((b, l, SSD_INNER), BF16),
        grid=(b, nc),
        in_specs=[pl.BlockSpec((1, SSD_CHUNK, SSD_INNER), lambda bi, c: (bi, cidx(c), 0)),
                  pl.BlockSpec((1, SSD_CHUNK, gs), lambda bi, c: (bi, cidx(c), SSD_INNER // gs)),
                  pl.BlockSpec((1, SSD_CHUNK, gs), lambda bi, c: (bi, cidx(c), SSD_INNER // gs + 1)),
                  pl.BlockSpec((1, SSD_CHUNK, LANES), lambda bi, c: (bi, cidx(c), COL_DT // LANES)),
                  pl.BlockSpec((1, LANES), lambda bi, c: (0, 0)),
                  pl.BlockSpec((1, LANES), lambda bi, c: (0, 0)),
                  pl.BlockSpec((LANES, SSD_INNER), lambda bi, c: (0, 0))],
        out_specs=pl.BlockSpec((1, SSD_CHUNK, SSD_INNER), lambda bi, c: (bi, cidx(c), 0)),
        scratch_shapes=[pltpu.VMEM((SSD_GROUPS, SSD_STATE, SSD_GROUP_WIDTH), F32)],
        compiler_params=_cparams(("parallel", "arbitrary")),
        name="ssd_scan_bwd" if rev else "ssd_scan_fwd",
    )(xbc, xbc, xbc, proj, dtb_row, alog_row, expand)


def _merge_kernel(x_ref, yf_ref, yb_ref, xs_ref, z_ref, ao_ref, ga_ref, gs_ref, g1_ref,
                  dsk_ref, snw_ref, wssd_ref, wattn_ref, wout_ref, o_ref):
    y = yf_ref[0].astype(F32) + yb_ref[0].astype(F32) + xs_ref[0].astype(F32) * dsk_ref[...]
    z = z_ref[0].astype(F32)
    y = y * (z * jax.nn.sigmoid(z))
    parts = []
    for g in range(SSD_GROUPS):
        yg = y[:, g * SSD_GROUP_WIDTH:(g + 1) * SSD_GROUP_WIDTH]
        parts.append(yg * lax.rsqrt(jnp.mean(yg * yg, axis=-1, keepdims=True) + EPS))
    y = jnp.concatenate(parts, axis=1) * snw_ref[...]
    ssd = jnp.dot(y.astype(BF16), wssd_ref[...], preferred_element_type=F32)
    attn = jnp.dot(ao_ref[0], wattn_ref[...], preferred_element_type=F32)
    merged = (jax.nn.sigmoid(ga_ref[0].astype(F32)) * attn
              + jax.nn.sigmoid(gs_ref[0].astype(F32)) * ssd)
    out = jnp.dot(merged.astype(BF16), wout_ref[...], preferred_element_type=F32)
    o_ref[0] = x_ref[0] + g1_ref[0] * out


def _merge(x, yf, yb, xbc, proj, attn_o, modr, dsk_full, snw, w_ssd, w_attn, w_out):
    b, l, _ = x.shape
    tl = min(l, 256)
    tok = lambda w, colb: pl.BlockSpec((1, tl, w), lambda bi, i: (bi, i, colb))
    full = lambda r, c: pl.BlockSpec((r, c), lambda bi, i: (0, 0))
    return pl.pallas_call(
        _merge_kernel,
        out_shape=jax.ShapeDtypeStruct((b, l, D_MODEL), F32),
        grid=(b, l // tl),
        in_specs=[tok(D_MODEL, 0), tok(SSD_INNER, 0), tok(SSD_INNER, 0), tok(SSD_INNER, 0),
                  tok(SSD_INNER, COL_Z // SSD_INNER), tok(ATTN_WIDTH, 0),
                  tok(D_MODEL, COL_GA // D_MODEL), tok(D_MODEL, COL_GS // D_MODEL),
                  pl.BlockSpec((1, 1, D_MODEL), lambda bi, i: (bi * N_MOD + 2, 0, 0)),
                  full(1, SSD_INNER), full(1, SSD_INNER),
                  full(SSD_INNER, D_MODEL), full(ATTN_WIDTH, D_MODEL), full(D_MODEL, D_MODEL)],
        out_specs=tok(D_MODEL, 0),
        compiler_params=_cparams(("parallel", "parallel")),
        name="merge_out_proj",
    )(x, yf, yb, xbc, proj, attn_o, proj, proj, modr, dsk_full, snw, w_ssd, w_attn, w_out)


def _topk_rows(s, k):
    rowf = lax.broadcasted_iota(I32, s.shape, 0).astype(F32)
    vals, ids = [], []
    for _ in range(k):
        m = jnp.max(s, axis=0, keepdims=True)
        am = jnp.min(jnp.where(s == m, rowf, float(s.shape[0])), axis=0, keepdims=True)
        vals.append(m)
        ids.append(am)
        s = jnp.where(rowf == am, -jnp.inf, s)
    return vals, ids


def _candidate_pieces():
    pieces = []
    for a in range(PEER_TOPK):
        nb = PEER_TOPK // (a + 1)
        if nb >= SUBLANES // 2:
            for b0 in range(0, nb, SUBLANES):
                pieces.append((a, 1, b0, SUBLANES))
    a_done = max(p[0] for p in pieces) + 1
    for b in range(PEER_TOPK):
        na = PEER_TOPK // (b + 1)
        if na > a_done:
            for a0 in range(0, na, SUBLANES):
                pieces.append((a0, SUBLANES, b, 1))
    return pieces, a_done


def _route_kernel(x_ref, sh_ref, sc_ref, nw_ref, wq_ref, keys_ref, hp_ref, idx_ref, gate_ref, cnt_ref):
    h = _norm_mod(x_ref[0], nw_ref[...], sc_ref[0], sh_ref[0])
    hp_ref[0] = h
    qall = jnp.dot(h.astype(BF16), wq_ref[...], preferred_element_type=F32)
    ntok = qall.shape[0]
    tops, topi = [], []
    for c in range(2 * PEER_HEADS):
        qc = qall[:, c * PEER_HALF:(c + 1) * PEER_HALF].astype(BF16)
        s = lax.dot_general(keys_ref[c], qc, (((1,), (1,)), ((), ())), preferred_element_type=F32)
        vals, ids = _topk_rows(s, PEER_TOPK)
        tops.append(vals)
        topi.append(ids)
    pieces, a_done = _candidate_pieces()
    sub = lax.broadcasted_iota(I32, (SUBLANES, ntok), 0)
    subf = sub.astype(F32)
    pos_parts, drop_parts = [], []
    for a0, na, b0, nb in pieces:
        if na == 1:
            pos_parts.append(subf + float(a0 * PEER_TOPK + b0))
            keep = sub + b0 < PEER_TOPK // (a0 + 1)
        else:
            pos_parts.append(subf * float(PEER_TOPK) + float(a0 * PEER_TOPK + b0))
            keep = jnp.where(sub + a0 >= a_done, sub + a0, PEER_TOPK) < PEER_TOPK // (b0 + 1)
        drop_parts.append(jnp.where(keep, 0.0, -jnp.inf))
    pos = jnp.concatenate(pos_parts, axis=0)
    drop = jnp.concatenate(drop_parts, axis=0)
    idx_rows, gate_rows = [], []
    for hd in range(PEER_HEADS):
        s0 = jnp.concatenate(tops[2 * hd], axis=0)
        i0 = jnp.concatenate(topi[2 * hd], axis=0)
        s1 = jnp.concatenate(tops[2 * hd + 1], axis=0)
        i1 = jnp.concatenate(topi[2 * hd + 1], axis=0)
        cparts, iparts = [], []
        for a0, na, b0, nb in pieces:
            if na == 1:
                cparts.append(tops[2 * hd][a0] + s1[b0:b0 + nb])
                iparts.append(topi[2 * hd][a0] * float(N_KEYS) + i1[b0:b0 + nb])
            else:
                cparts.append(s0[a0:a0 + na] + tops[2 * hd + 1][b0])
                iparts.append(i0[a0:a0 + na] * float(N_KEYS) + topi[2 * hd + 1][b0])
        cand = jnp.concatenate(cparts, axis=0) + drop
        cidx = jnp.concatenate(iparts, axis=0)
        best = []
        for _ in range(PEER_TOPK):
            m = jnp.max(cand, axis=0, keepdims=True)
            first = jnp.min(jnp.where(cand == m, pos, float(PEER_TOPK * PEER_TOPK)), axis=0, keepdims=True)
            sel = pos == first
            idx_rows.append(jnp.max(jnp.where(sel, cidx, -1.0), axis=0, keepdims=True))
            best.append(m)
            cand = jnp.where(sel, -jnp.inf, cand)
        bs = jnp.concatenate(best, axis=0)
        p = jnp.exp(bs - best[0])
        gate_rows.append(p / jnp.sum(p, axis=0, keepdims=True))
    idx_t = jnp.concatenate(idx_rows, axis=0)
    gate_t = jnp.concatenate(gate_rows, axis=0)
    idx_t, gate_t, n_zero = _partition_by_class(idx_t, gate_t)
    idx_ref[0] = (idx_t * float(SLAB_ROWS)).T.astype(I32)
    gate_ref[0] = gate_t.T
    cnt_ref[0, 0] = n_zero.astype(I32)


def _route(x1, modr, norm_w, wq, keys):
    b, l, _ = x1.shape
    tl = min(l, 256)
    tok = lambda w: pl.BlockSpec((1, tl, w), lambda bi, i: (bi, i, 0))
    return pl.pallas_call(
        _route_kernel,
        out_shape=(jax.ShapeDtypeStruct((b, l, D_MODEL), F32),
                   jax.ShapeDtypeStruct((b, l, N_SEL), I32),
                   jax.ShapeDtypeStruct((b, l, N_SEL), F32),
                   jax.ShapeDtypeStruct((b, l // tl, 1, tl), I32)),
        grid=(b, l // tl),
        in_specs=[tok(D_MODEL),
                  pl.BlockSpec((1, 1, D_MODEL), lambda bi, i: (bi * N_MOD + 3, 0, 0)),
                  pl.BlockSpec((1, 1, D_MODEL), lambda bi, i: (bi * N_MOD + 4, 0, 0)),
                  pl.BlockSpec((1, D_MODEL), lambda bi, i: (0, 0)),
                  pl.BlockSpec((D_MODEL, PEER_HEADS * PEER_KEY_DIM), lambda bi, i: (0, 0)),
                  pl.BlockSpec((2 * PEER_HEADS, N_KEYS, PEER_HALF), lambda bi, i: (0, 0, 0))],
        out_specs=(tok(D_MODEL), tok(N_SEL), tok(N_SEL),
                   pl.BlockSpec((1, 1, 1, tl), lambda bi, i: (bi, i, 0, 0))),
        compiler_params=_cparams(("parallel", "parallel")),
        name="peer_route",
    )(x1, modr, modr, norm_w, wq, keys)


def _pack_rows(tab):
    n = tab.shape[0]
    bits = lax.bitcast_convert_type(tab.astype(BF16), jnp.uint16).astype(jnp.uint32)
    lo = bits[:, :D_MODEL // 2]
    hi = bits[:, D_MODEL // 2:]
    words = lax.bitcast_convert_type(lo | (hi << 16), I32)
    return words.reshape(n, ROW_WORDS, LANES)


def _gelu_tanh(x):
    return 0.5 * x * (1.0 + jnp.tanh(math.sqrt(2.0 / math.pi) * (x + 0.044715 * (x * x * x))))


def _gather_slabs(idx_ref, base, tab_ref, tile_ref, tt):
    for k in range(N_SEL):
        r = pl.multiple_of(idx_ref[base + k], ROW_WORDS)
        tile_ref[tt, k * ROW_WORDS:(k + 1) * ROW_WORDS, :] = tab_ref[pl.ds(r, ROW_WORDS), :]


def _unpack_rows(words):
    return pltpu.bitcast(words << 16, F32), pltpu.bitcast(words & jnp.int32(-65536), F32)


def _peer_act_kernel(idx_ref, x_ref, tab_ref, act_ref, tile, psum):
    nrows = N_SEL * ROW_WORDS
    half = SUBLANES // 2
    ones = jnp.ones((SUBLANES, LANES), BF16)
    nt = (((1,), (1,)), ((), ()))

    def group(gi, carry):
        t0 = pl.multiple_of(gi * PEER_TOK_GROUP, PEER_TOK_GROUP)
        acts = []
        for tt in range(PEER_TOK_GROUP):
            _gather_slabs(idx_ref, (t0 + tt) * N_SEL, tab_ref, tile, tt)
            x8 = x_ref[pl.ds(pl.multiple_of((t0 + tt) * SUBLANES, SUBLANES), SUBLANES), :]
            x_lo = jnp.concatenate([x8[:half], x8[:half]], axis=0)
            x_hi = jnp.concatenate([x8[half:], x8[half:]], axis=0)
            u_lo, u_hi = _unpack_rows(tile[tt])
            prod = (u_lo.reshape(nrows // SUBLANES, SUBLANES, LANES) * x_lo
                    + u_hi.reshape(nrows // SUBLANES, SUBLANES, LANES) * x_hi)
            part = prod + pltpu.roll(prod, 1, axis=1)
            part = part + pltpu.roll(part, 2, axis=1)
            psum[tt] = part.reshape(nrows, LANES)
            q = psum[tt, pl.ds(ROW_WORDS - 1, N_SEL, stride=ROW_WORDS), :]
            hi, lo = _split_bf16(q)
            a = (lax.dot_general(ones, hi, nt, preferred_element_type=F32)
                 + lax.dot_general(ones, lo, nt, preferred_element_type=F32))
            acts.append(a[:1])
        act_ref[pl.ds(t0, PEER_TOK_GROUP), :] = jnp.concatenate(acts, axis=0)
        return carry

    lax.fori_loop(0, PEER_BLOCK // PEER_TOK_GROUP, group, 0)


def _peer_sum_kernel(idx_ref, act_ref, gate_ref, tab_ref, rep_ref, ones_ref, out_ref, tile):
    nrows = N_SEL * ROW_WORDS
    half = SUBLANES // 2

    def group(gi, carry):
        t0 = pl.multiple_of(gi * PEER_TOK_GROUP, PEER_TOK_GROUP)
        w = gate_ref[pl.ds(t0, PEER_TOK_GROUP), :] * _gelu_tanh(act_ref[pl.ds(t0, PEER_TOK_GROUP), :])
        for tt in range(PEER_TOK_GROUP):
            _gather_slabs(idx_ref, (t0 + tt) * N_SEL, tab_ref, tile, tt)
            v_lo, v_hi = _unpack_rows(tile[tt])
            wrows = jnp.dot((rep_ref[...] * w[tt:tt + 1, :]).astype(BF16), ones_ref[...],
                            preferred_element_type=F32)
            acc_lo = jnp.sum((v_lo * wrows).reshape(nrows // SUBLANES, SUBLANES, LANES), axis=0)
            acc_hi = jnp.sum((v_hi * wrows).reshape(nrows // SUBLANES, SUBLANES, LANES), axis=0)
            o8 = jnp.concatenate([acc_lo[:half] + acc_lo[half:], acc_hi[:half] + acc_hi[half:]], axis=0)
            out_ref[pl.ds(pl.multiple_of((t0 + tt) * SUBLANES, SUBLANES), SUBLANES), :] = o8
        return carry

    lax.fori_loop(0, PEER_BLOCK // PEER_TOK_GROUP, group, 0)


def _peer_experts(idx_flat, x8, gates, tab_u, tab_v):
    t = gates.shape[0]
    idx_spec = pl.BlockSpec((PEER_BLOCK * N_SEL,), lambda i: (i,), memory_space=pltpu.SMEM)
    tab_spec = pl.BlockSpec(memory_space=pltpu.VMEM)
    tok_spec = pl.BlockSpec((PEER_BLOCK, N_SEL), lambda i: (i, 0))
    row_spec = pl.BlockSpec((PEER_BLOCK * SUBLANES, LANES), lambda i: (i, 0))
    const = lambda a: pl.BlockSpec(a.shape, lambda i: (0, 0))
    nrows = N_SEL * ROW_WORDS
    tile = pltpu.VMEM((PEER_TOK_GROUP, nrows, LANES), I32)
    act = pl.pallas_call(
        _peer_act_kernel,
        out_shape=jax.ShapeDtypeStruct((t, N_SEL), F32),
        grid=(t // PEER_BLOCK,),
        in_specs=[idx_spec, row_spec, tab_spec],
        out_specs=tok_spec,
        scratch_shapes=[tile, pltpu.VMEM((PEER_TOK_GROUP, nrows, LANES), F32)],
        compiler_params=_cparams(("parallel",)),
        name="peer_act",
    )(idx_flat, x8, tab_u)
    rep = (jnp.arange(nrows)[:, None] // ROW_WORDS == jnp.arange(N_SEL)[None, :]).astype(F32)
    ones = jnp.ones((N_SEL, LANES), BF16)
    return pl.pallas_call(
        _peer_sum_kernel,
        out_shape=jax.ShapeDtypeStruct((t * SUBLANES, LANES), F32),
        grid=(t // PEER_BLOCK,),
        in_specs=[idx_spec, tok_spec, tok_spec, tab_spec, const(rep), const(ones)],
        out_specs=row_spec,
        scratch_shapes=[tile],
        compiler_params=_cparams(("parallel",)),
        name="peer_sum",
    )(idx_flat, act, gates, tab_v, rep, ones)


SLAB_ROWS = 2 * ROW_WORDS
PEER_CHUNK = 16
PEER_COMMON_CHUNKS = 5


def _expert_class(e):
    return lax.population_count(e) & 1


def _class_tables(tab_u, tab_v):
    slabs = jnp.stack([_pack_rows(tab_u), _pack_rows(tab_v)], axis=2).reshape(N_EXPERTS, SLAB_ROWS, LANES)
    even = 2 * jnp.arange(N_EXPERTS // 2, dtype=I32)
    first = even + _expert_class(even)
    second = even + 1 - _expert_class(even)
    shape = (N_EXPERTS // 2 * SLAB_ROWS, LANES)
    return slabs[first].reshape(shape), slabs[second].reshape(shape)


def _compact_rows(vals, shift, valid, up):
    n = valid.shape[0]
    for bit in range(n.bit_length() - 1):
        step = 1 << bit
        roll = (lambda a: pltpu.roll(a, (n - step) if up else step, axis=0))
        move = valid * ((shift >> bit) & 1)
        bring = roll(move) != 0
        vals = [jnp.where(bring, roll(v), v) for v in vals]
        shift = jnp.where(bring, roll(shift), shift)
        valid = jnp.where(bring, 1, valid - move)
    return vals


def _partition_by_class(idx_t, gate_t):
    n = idx_t.shape[0]
    e = idx_t.astype(I32)
    cls = _expert_class(e)
    local = (e >> 1).astype(F32)
    is_one = cls.astype(F32).astype(BF16)
    is_zero = (1 - cls).astype(F32).astype(BF16)
    ri = lax.broadcasted_iota(I32, (n, n), 0)
    ci = lax.broadcasted_iota(I32, (n, n), 1)
    before = jnp.where(ci < ri, 1.0, 0.0).astype(BF16)
    after = jnp.where(ci > ri, 1.0, 0.0).astype(BF16)
    ones_before = jnp.dot(before, is_one, preferred_element_type=F32).astype(I32)
    zeros_after = jnp.dot(after, is_zero, preferred_element_type=F32).astype(I32)
    n_zero = jnp.sum(is_zero.astype(F32), axis=0, keepdims=True)
    z_idx, z_gate = _compact_rows([local, gate_t], ones_before, 1 - cls, up=True)
    o_idx, o_gate = _compact_rows([local, gate_t], zeros_after, cls, up=False)
    first = lax.broadcasted_iota(I32, idx_t.shape, 0).astype(F32) < n_zero
    return jnp.where(first, z_idx, o_idx), jnp.where(first, z_gate, o_gate), n_zero


def _class_pass_parts(idx_ref, x_ref, tab_ref, rep_ref, ones_ref, out_ref, *, cls, s_lo, s_hi):
    lane = lax.broadcasted_iota(I32, (1, N_SEL), 1)
    in_range = jnp.logical_and(lane >= s_lo * PEER_CHUNK, lane < s_hi * PEER_CHUNK)
    crows = PEER_CHUNK * SLAB_ROWS
    nsel = (s_hi - s_lo) * PEER_CHUNK
    nrows = nsel * ROW_WORDS
    half = SUBLANES // 2
    nt = (((1,), (1,)), ((), ()))
    ones8 = jnp.ones((SUBLANES, LANES), BF16)

    def gather_chunk(tile, base, c):
        k0 = base + c * PEER_CHUNK
        r0 = (c - s_lo) * crows
        r0 = r0 if isinstance(c, int) else pl.multiple_of(r0, crows)
        for kk in range(PEER_CHUNK):
            r = pl.multiple_of(idx_ref[k0 + kk], SLAB_ROWS)
            tile[pl.ds(r0 + kk * SLAB_ROWS, SLAB_ROWS), :] = tab_ref[pl.ds(r, SLAB_ROWS), :]

    def valid_row(n_zero):
        mine = (lane >= n_zero) if cls else (lane < n_zero)
        return jnp.where(jnp.logical_and(mine, in_range), 1.0, 0.0)

    def needed_chunks(n_zero):
        if cls:
            return jnp.maximum(n_zero // PEER_CHUNK, s_lo), s_hi
        return s_lo, jnp.minimum((n_zero + PEER_CHUNK - 1) // PEER_CHUNK, s_hi)

    def rows8(t):
        return pl.ds(pl.multiple_of(t * SUBLANES, SUBLANES), SUBLANES)

    def activation(tile, t):
        x8 = x_ref[rows8(t), :]
        acc = None
        for s in range(ROW_WORDS):
            u_lo, u_hi = _unpack_rows(tile[pl.ds(2 * s, nsel, stride=SLAB_ROWS), :])
            term = u_lo * x8[s:s + 1] + u_hi * x8[half + s:half + s + 1]
            acc = term if acc is None else acc + term
        before, after = s_lo * PEER_CHUNK, N_SEL - s_hi * PEER_CHUNK
        full = jnp.concatenate(([jnp.zeros((before, LANES), F32)] if before else []) + [acc]
                               + ([jnp.zeros((after, LANES), F32)] if after else []), axis=0)
        return jnp.sum(full.T, axis=0, keepdims=True)

    def weighted_sum(tile, w_row, t, base_ref):
        v_lo, v_hi = _unpack_rows(tile[pl.ds(1, nrows, stride=2), :])
        wrows = jnp.dot((rep_ref[...] * w_row).astype(BF16), ones_ref[...], preferred_element_type=F32)
        acc_lo = jnp.sum((v_lo * wrows).reshape(nrows // SUBLANES, SUBLANES, LANES), axis=0)
        acc_hi = jnp.sum((v_hi * wrows).reshape(nrows // SUBLANES, SUBLANES, LANES), axis=0)
        o8 = jnp.concatenate([acc_lo[:half] + acc_lo[half:], acc_hi[:half] + acc_hi[half:]], axis=0)
        sl = rows8(t)
        out_ref[sl, :] = o8 if base_ref is None else base_ref[sl, :] + o8

    return gather_chunk, valid_row, needed_chunks, activation, weighted_sum


def _class_body(idx_ref, cnt_ref, x_ref, gate_ref, prev_ref, tab_ref, rep_ref, rep_rare_ref, ones_ref,
                out_ref, tiles, wbufs, tile_rare, *, cls):
    nchunk = N_SEL // PEER_CHUNK
    if cls:
        s_lo, s_hi, r_lo, r_hi = nchunk - PEER_COMMON_CHUNKS, nchunk, 0, nchunk - PEER_COMMON_CHUNKS
    else:
        s_lo, s_hi, r_lo, r_hi = 0, PEER_COMMON_CHUNKS, PEER_COMMON_CHUNKS, nchunk
    gather_chunk, valid_row, _, activation, weighted_sum = _class_pass_parts(
        idx_ref, x_ref, tab_ref, rep_ref, ones_ref, out_ref, cls=cls, s_lo=s_lo, s_hi=s_hi)

    @pl.when(pl.program_id(0) == 0)
    def _():
        for buf in tiles + [tile_rare]:
            buf[...] = jnp.zeros_like(buf)

    def first_stage(g, par):
        t0 = pl.multiple_of(g * PEER_TOK_GROUP, PEER_TOK_GROUP)
        acts, valid = [], []

        def token(tt):
            valid.append(valid_row(cnt_ref[t0 + tt]))
            for c in range(s_lo, s_hi):
                gather_chunk(tiles[par].at[tt], (t0 + tt) * N_SEL, c)
            acts.append(activation(tiles[par].at[tt], t0 + tt)[:1])

        def finish():
            act = jnp.concatenate(acts, axis=0)
            wbufs[par][...] = (gate_ref[pl.ds(t0, PEER_TOK_GROUP), :] * _gelu_tanh(act)
                               * jnp.concatenate(valid, axis=0))

        return token, finish

    def second_stage(g, par):
        t0 = pl.multiple_of(g * PEER_TOK_GROUP, PEER_TOK_GROUP)
        w = wbufs[par][...]
        return lambda tt: weighted_sum(tiles[par].at[tt], w[tt:tt + 1, :], t0 + tt, prev_ref)

    def step(first, second):
        tok1, fin1 = first_stage(*first) if first else (None, None)
        tok2 = second_stage(*second) if second else None
        for tt in range(PEER_TOK_GROUP):
            if tok1:
                tok1(tt)
            if tok2:
                tok2(tt)
        if fin1:
            fin1()

    ngroups = PEER_BLOCK // PEER_TOK_GROUP
    step((0, 0), None)

    def pair(j, carry):
        step((2 * j + 1, 1), (2 * j, 0))
        step((2 * j + 2, 0), (2 * j + 1, 1))
        return carry

    lax.fori_loop(0, ngroups // 2 - 1, pair, 0)
    step((ngroups - 1, 1), (ngroups - 2, 0))
    step(None, (ngroups - 1, 1))

    r_gather, r_valid, r_chunks, r_activation, r_weighted_sum = _class_pass_parts(
        idx_ref, x_ref, tab_ref, rep_rare_ref, ones_ref, out_ref, cls=cls, s_lo=r_lo, s_hi=r_hi)

    def needs_rare(n_zero):
        return (n_zero < r_hi * PEER_CHUNK) if cls else (n_zero > r_lo * PEER_CHUNK)

    def rare_token(t):
        n_zero = cnt_ref[t]

        @pl.when(needs_rare(n_zero))
        def _():
            def chunk(c, carry):
                r_gather(tile_rare, t * N_SEL, c)
                return carry
            lax.fori_loop(*r_chunks(n_zero), chunk, 0)
            act = r_activation(tile_rare, t)[:1]
            w = gate_ref[pl.ds(t, 1), :] * _gelu_tanh(act) * r_valid(n_zero)
            r_weighted_sum(tile_rare, w, t, out_ref)

    def rare_group(g, carry):
        t0 = pl.multiple_of(g * PEER_TOK_GROUP, PEER_TOK_GROUP)
        counts = [cnt_ref[t0 + tt] for tt in range(PEER_TOK_GROUP)]
        extreme = functools.reduce(jnp.minimum if cls else jnp.maximum, counts)

        @pl.when(needs_rare(extreme))
        def _():
            for tt in range(PEER_TOK_GROUP):
                rare_token(t0 + tt)
        return carry

    lax.fori_loop(0, ngroups, rare_group, 0)


def _class_first_kernel(idx_ref, cnt_ref, x_ref, gate_ref, tab_ref, rep_ref, rep_rare_ref, ones_ref,
                        out_ref, tile0, tile1, wbuf0, wbuf1, tile_rare, *, cls):
    _class_body(idx_ref, cnt_ref, x_ref, gate_ref, None, tab_ref, rep_ref, rep_rare_ref, ones_ref,
                out_ref, [tile0, tile1], [wbuf0, wbuf1], tile_rare, cls=cls)


def _class_next_kernel(idx_ref, cnt_ref, x_ref, gate_ref, prev_ref, tab_ref, rep_ref, rep_rare_ref, ones_ref,
                       out_ref, tile0, tile1, wbuf0, wbuf1, tile_rare, *, cls):
    _class_body(idx_ref, cnt_ref, x_ref, gate_ref, prev_ref, tab_ref, rep_ref, rep_rare_ref, ones_ref,
                out_ref, [tile0, tile1], [wbuf0, wbuf1], tile_rare, cls=cls)


def _class_pass(idx_flat, cnt_flat, x8, gates, prev, tab, *, cls):
    t = gates.shape[0]
    ncommon = PEER_COMMON_CHUNKS * PEER_CHUNK
    nrare = N_SEL - ncommon

    def rep(first_sel, nsel):
        rows = jnp.arange(nsel * ROW_WORDS)[:, None] // ROW_WORDS + first_sel
        return (rows == jnp.arange(N_SEL)[None, :]).astype(F32)

    rep_common = rep(nrare if cls else 0, ncommon)
    rep_rare = rep(0 if cls else ncommon, nrare)
    ones = jnp.ones((N_SEL, LANES), BF16)
    const = lambda a: pl.BlockSpec(a.shape, lambda i: (0, 0))
    row_spec = pl.BlockSpec((PEER_BLOCK * SUBLANES, LANES), lambda i: (i, 0))
    in_specs = [pl.BlockSpec((PEER_BLOCK * N_SEL,), lambda i: (i,), memory_space=pltpu.SMEM),
                pl.BlockSpec((PEER_BLOCK,), lambda i: (i,), memory_space=pltpu.SMEM),
                row_spec, pl.BlockSpec((PEER_BLOCK, N_SEL), lambda i: (i, 0))]
    args = [idx_flat, cnt_flat, x8, gates]
    if prev is not None:
        in_specs.append(row_spec)
        args.append(prev)
    in_specs += [pl.BlockSpec(memory_space=pltpu.VMEM), const(rep_common), const(rep_rare), const(ones)]
    args += [tab, rep_common, rep_rare, ones]
    tile = pltpu.VMEM((PEER_TOK_GROUP, ncommon * SLAB_ROWS, LANES), I32)
    wbuf = pltpu.VMEM((PEER_TOK_GROUP, N_SEL), F32)
    return pl.pallas_call(
        functools.partial(_class_first_kernel if prev is None else _class_next_kernel, cls=cls),
        out_shape=jax.ShapeDtypeStruct((t * SUBLANES, LANES), F32),
        grid=(t // PEER_BLOCK,),
        in_specs=in_specs,
        out_specs=row_spec,
        scratch_shapes=[tile, tile, wbuf, wbuf, pltpu.VMEM((nrare * SLAB_ROWS, LANES), I32)],
        compiler_params=_cparams(("arbitrary",)),
        name="peer_class%d" % cls,
    )(*args)


def _peer_experts_by_class(idx_flat, cnt_flat, x8, gates, tab0, tab1):
    out = _class_pass(idx_flat, cnt_flat, x8, gates, None, tab0, cls=0)
    return _class_pass(idx_flat, cnt_flat, x8, gates, out, tab1, cls=1)


def _final_kernel(x_ref, o_ref, g2_ref, y_ref):
    y_ref[0] = x_ref[0] + g2_ref[0] * o_ref[0]


def _final_residual(x1, outp, modr):
    b, l, _ = x1.shape
    tl = min(l, 512)
    tok = pl.BlockSpec((1, tl, D_MODEL), lambda bi, i: (bi, i, 0))
    return pl.pallas_call(
        _final_kernel,
        out_shape=jax.ShapeDtypeStruct((b, l, D_MODEL), F32),
        grid=(b, l // tl),
        in_specs=[tok, tok, pl.BlockSpec((1, 1, D_MODEL), lambda bi, i: (bi * N_MOD + 5, 0, 0))],
        out_specs=tok,
        compiler_params=_cparams(("parallel", "parallel")),
        name="peer_residual",
    )(x1, outp, modr)


def _prepare(rel_bias, ada_w, ada_b, norm1_w, norm2_w, w_in, q_norm_w, k_norm_w, attn_sink, conv_w,
             conv_b, a_log, dt_bias, d_skip, ssd_norm_w, w_attn_br, w_ssd_br, w_out, peer_wq,
             peer_keys, peer_u, peer_v):
    lyr = 0
    w = w_in[lyr]
    o = 0
    parts = {}
    for name, width in (("q", ATTN_WIDTH), ("k", KV_WIDTH), ("v", KV_WIDTH), ("z", SSD_INNER),
                        ("xbc", CONV_CH), ("dt", 2 * SSD_HEADS), ("ga", D_MODEL), ("gs", D_MODEL)):
        parts[name] = w[:, o:o + width]
        o += width
    pad = jnp.zeros((D_MODEL, PROJ_W - (COL_DT + 2 * SSD_HEADS)), w.dtype)
    w_all = jnp.concatenate([parts["z"], parts["q"], parts["ga"], parts["gs"], parts["xbc"],
                             parts["k"], parts["v"], parts["dt"], pad], axis=1).astype(BF16)
    lane_pad = LANES - 2 * SSD_HEADS
    expand = (jnp.arange(SSD_INNER)[None, :] // SSD_HEAD_DIM == jnp.arange(LANES)[:, None])
    tab0, tab1 = _class_tables(peer_u[lyr], peer_v[lyr])
    return dict(
        ada_w=ada_w[lyr], ada_b=ada_b[lyr],
        norm1_w=norm1_w[lyr].reshape(1, D_MODEL), norm2_w=norm2_w[lyr].reshape(1, D_MODEL),
        w_all=w_all,
        bias_tab=_bias_table(rel_bias), sink=attn_sink[lyr].astype(F32),
        qw=(jnp.tile(q_norm_w[lyr], N_Q_HEADS) * HEAD_DIM ** -0.5).reshape(1, ATTN_WIDTH),
        kw=jnp.tile(k_norm_w[lyr], N_KV_HEADS).reshape(1, KV_WIDTH),
        conv_w=conv_w[lyr], conv_b=conv_b[lyr],
        dtb_row=jnp.pad(dt_bias[lyr].astype(F32).reshape(1, -1), ((0, 0), (0, lane_pad))),
        alog_row=jnp.pad(a_log[lyr].astype(F32).reshape(1, -1), ((0, 0), (0, lane_pad))),
        expand_fwd=expand.astype(BF16),
        expand_bwd=jnp.roll(expand, SSD_HEADS, axis=0).astype(BF16),
        dsk_full=jnp.repeat(d_skip[lyr], SSD_HEAD_DIM).reshape(1, SSD_INNER),
        snw=ssd_norm_w[lyr].reshape(1, SSD_INNER),
        w_ssd=w_ssd_br[lyr].astype(BF16), w_attn=w_attn_br[lyr].astype(BF16), w_out=w_out[lyr].astype(BF16),
        wq=peer_wq[lyr].astype(BF16),
        keys=peer_keys[lyr].reshape(2 * PEER_HEADS, N_KEYS, PEER_HALF).astype(BF16),
        tab0=tab0, tab1=tab1,
    )


def _token_mixer_stage(x, modr, p):
    proj = _in_projection(x, modr, p["norm1_w"], p["w_all"])
    attn_o = _attention(proj, p["bias_tab"], p["sink"], p["qw"], p["kw"])
    xbc = _conv_silu(proj, p["conv_w"], p["conv_b"])
    yf = _ssd_scan(xbc, proj, p["dtb_row"], p["alog_row"], p["expand_fwd"], rev=False)
    yb = _ssd_scan(xbc, proj, p["dtb_row"], p["alog_row"], p["expand_bwd"], rev=True)
    return _merge(x, yf, yb, xbc, proj, attn_o, modr, p["dsk_full"], p["snw"],
                  p["w_ssd"], p["w_attn"], p["w_out"])


def _peer_stage(x1, modr, p):
    b, l, _ = x1.shape
    t = b * l
    hp, idx, gates, cnt = _route(x1, modr, p["norm2_w"], p["wq"], p["keys"])
    out = _peer_experts_by_class(idx.reshape(t * N_SEL), cnt.reshape(t), hp.reshape(t * SUBLANES, LANES),
                                 gates.reshape(t, N_SEL), p["tab0"], p["tab1"])
    return _final_residual(x1, out.reshape(b, l, D_MODEL), modr)


def _encoder(x, c, p):
    nb = c.shape[0]
    modr = _modulation(c, p["ada_w"], p["ada_b"]).reshape(nb * N_MOD, 1, D_MODEL)
    x1 = _token_mixer_stage(x, modr, p)
    return _peer_stage(x1, modr, p)


def kernel(x_prompt, x_sample, c_prompt, c_sample, rel_bias, ada_w, ada_b, norm1_w, norm2_w, w_in,
           q_norm_w, k_norm_w, attn_sink, conv_w, conv_b, a_log, dt_bias, d_skip, ssd_norm_w,
           w_attn_br, w_ssd_br, w_out, peer_wq, peer_keys, peer_u, peer_v):
    p = _prepare(rel_bias, ada_w, ada_b, norm1_w, norm2_w, w_in, q_norm_w, k_norm_w, attn_sink,
                 conv_w, conv_b, a_log, dt_bias, d_skip, ssd_norm_w, w_attn_br, w_ssd_br, w_out,
                 peer_wq, peer_keys, peer_u, peer_v)
    return (_encoder(x_prompt, c_prompt, p), _encoder(x_sample, c_sample, p))
```

```python
import functools
import math

import jax
import jax.numpy as jnp
from jax import lax
from jax.experimental import pallas as pl
from jax.experimental.pallas import tpu as pltpu

F32 = jnp.float32
BF16 = jnp.bfloat16
I32 = jnp.int32

D_MODEL = 1024
HEAD_DIM = 64
N_Q_HEADS = 16
N_KV_HEADS = 4
Q_PER_KV = N_Q_HEADS // N_KV_HEADS
ATTN_WIDTH = N_Q_HEADS * HEAD_DIM
KV_WIDTH = N_KV_HEADS * HEAD_DIM
WINDOW = 128
BAND_BLOCK = 128
NUM_BUCKETS = 32
MAX_DISTANCE = 128
NEG_INF = -1e30

SSD_INNER = 2 * D_MODEL
SSD_HEAD_DIM = 64
SSD_HEADS = SSD_INNER // SSD_HEAD_DIM
SSD_GROUPS = 4
SSD_HEADS_PER_GROUP = SSD_HEADS // SSD_GROUPS
SSD_STATE = 128
SSD_CHUNK = 128
SSD_GROUP_WIDTH = SSD_INNER // SSD_GROUPS
CONV_WIDTH = 5
CONV_CH = SSD_INNER + 2 * SSD_GROUPS * SSD_STATE

PEER_HEADS = 8
PEER_KEY_DIM = 256
PEER_HALF = PEER_KEY_DIM // 2
N_KEYS = 128
N_EXPERTS = N_KEYS * N_KEYS
PEER_TOPK = 16
N_SEL = PEER_HEADS * PEER_TOPK

N_MOD = 6
EPS = 1e-6

COL_Z = 0
COL_Q = 2048
COL_GA = 3072
COL_GS = 4096
COL_XBC = 5120
COL_K = 8192
COL_V = 8448
COL_DT = 8704
PROJ_W = 9216

LANES = 128
SUBLANES = 8
VMEM_LIMIT = 56 * 1024 * 1024
ROW_WORDS = D_MODEL // (2 * LANES)
PEER_TOK_GROUP = 8
PEER_BLOCK = 128


def _cparams(sem):
    return pltpu.CompilerParams(dimension_semantics=sem, vmem_limit_bytes=VMEM_LIMIT)


def _split_bf16(v):
    hi = v.astype(BF16)
    lo = (v - hi.astype(F32)).astype(BF16)
    return hi, lo


def _dot2(v, m_bf16):
    hi, lo = _split_bf16(v)
    return (jnp.dot(hi, m_bf16, preferred_element_type=F32)
            + jnp.dot(lo, m_bf16, preferred_element_type=F32))


def _dot2_left(m_bf16, v):
    hi, lo = _split_bf16(v)
    return (jnp.dot(m_bf16, hi, preferred_element_type=F32)
            + jnp.dot(m_bf16, lo, preferred_element_type=F32))


def _mod_kernel(c_ref, w_ref, b_ref, o_ref):
    c = c_ref[...]
    sc = c * jax.nn.sigmoid(c)
    o_ref[...] = jnp.dot(sc, w_ref[...], preferred_element_type=F32,
                         precision=lax.Precision.HIGHEST) + b_ref[...]


def _modulation(c, ada_w, ada_b):
    nb = c.shape[0]
    n = ada_w.shape[1]
    tn = 1024
    return pl.pallas_call(
        _mod_kernel,
        out_shape=jax.ShapeDtypeStruct((nb, n), F32),
        grid=(n // tn,),
        in_specs=[pl.BlockSpec((nb, D_MODEL), lambda j: (0, 0)),
                  pl.BlockSpec((D_MODEL, tn), lambda j: (0, j)),
                  pl.BlockSpec((1, tn), lambda j: (0, j))],
        out_specs=pl.BlockSpec((nb, tn), lambda j: (0, j)),
        compiler_params=_cparams(("arbitrary",)),
        name="adaln_mod",
    )(c, ada_w, ada_b.reshape(1, n))


def _norm_mod(x, nw, sc, sh):
    ms = jnp.mean(x * x, axis=-1, keepdims=True)
    h = x * lax.rsqrt(ms + EPS) * nw
    return h * (1.0 + sc) + sh


def _inproj_kernel(x_ref, sh_ref, sc_ref, nw_ref, w_ref, o_ref, h_scr):
    @pl.when(pl.program_id(2) == 0)
    def _():
        h = _norm_mod(x_ref[0], nw_ref[...], sc_ref[0], sh_ref[0])
        h_scr[...] = h.astype(BF16)

    o_ref[0] = jnp.dot(h_scr[...], w_ref[...], preferred_element_type=F32).astype(o_ref.dtype)


def _in_projection(x, modr, norm_w, w_all):
    b, l, _ = x.shape
    tl = min(l, 1024)
    tn = 1024
    return pl.pallas_call(
        _inproj_kernel,
        out_shape=jax.ShapeDtypeStruct((b, l, PROJ_W), BF16),
        grid=(b, l // tl, PROJ_W // tn),
        in_specs=[pl.BlockSpec((1, tl, D_MODEL), lambda bi, i, j: (bi, i, 0)),
                  pl.BlockSpec((1, 1, D_MODEL), lambda bi, i, j: (bi * N_MOD + 0, 0, 0)),
                  pl.BlockSpec((1, 1, D_MODEL), lambda bi, i, j: (bi * N_MOD + 1, 0, 0)),
                  pl.BlockSpec((1, D_MODEL), lambda bi, i, j: (0, 0)),
                  pl.BlockSpec((D_MODEL, tn), lambda bi, i, j: (0, j))],
        out_specs=pl.BlockSpec((1, tl, tn), lambda bi, i, j: (bi, i, j)),
        scratch_shapes=[pltpu.VMEM((tl, D_MODEL), BF16)],
        compiler_params=_cparams(("parallel", "parallel", "arbitrary")),
        name="in_projection",
    )(x, modr, modr, norm_w, w_all)


def _head_sums(n_heads):
    c = jnp.arange(n_heads * HEAD_DIM)
    hsum = (c[:, None] // HEAD_DIM == jnp.arange(LANES)[None, :]).astype(BF16)
    return hsum, hsum.T


def _qk_norm(t, hsum, hexp, w_full):
    ssq = _dot2(t * t, hsum)
    inv = lax.rsqrt(ssq * (1.0 / HEAD_DIM) + EPS)
    return t * _dot2(inv, hexp) * w_full


def _attn_kernel(sink_ref, q_ref, kp_ref, kc_ref, kn_ref, vp_ref, vc_ref, vn_ref, bias_ref,
                 qw_ref, kw_ref, qsum_ref, qexp_ref, ksum_ref, kexp_ref, o_ref, s_scr, p_scr):
    q = _qk_norm(q_ref[0].astype(F32), qsum_ref[...], qexp_ref[...], qw_ref[...]).astype(BF16)
    k = jnp.concatenate([kp_ref[0], kc_ref[0], kn_ref[0]], axis=0).astype(F32)
    k = _qk_norm(k, ksum_ref[...], kexp_ref[...], kw_ref[...])
    v = jnp.concatenate([vp_ref[0], vc_ref[0], vn_ref[0]], axis=0).astype(F32)
    low = lax.broadcasted_iota(I32, (1, LANES), 1) < HEAD_DIM
    nt = (((1,), (1,)), ((), ()))
    k_sel, v_sel = [], []
    for c in range(N_KV_HEADS // 2):
        kslab = k[:, c * LANES:(c + 1) * LANES]
        vslab = v[:, c * LANES:(c + 1) * LANES]
        kroll = pltpu.roll(kslab, HEAD_DIM, axis=1)
        vroll = pltpu.roll(vslab, HEAD_DIM, axis=1)
        for e in range(2):
            k_sel.append((jnp.where(low, kroll if e else kslab, 0.0).astype(BF16),
                          jnp.where(low, 0.0, kslab if e else kroll).astype(BF16)))
            v_sel.append((jnp.where(low, vroll if e else vslab, 0.0).astype(BF16),
                          jnp.where(low, 0.0, vslab if e else vroll).astype(BF16)))
    for hd in range(N_Q_HEADS):
        qs = q[:, (hd // 2) * LANES:(hd // 2 + 1) * LANES]
        s = lax.dot_general(qs, k_sel[hd // Q_PER_KV][hd % 2], nt, preferred_element_type=F32)
        s_scr[hd] = s + bias_ref[0, hd]
    for hd in range(N_Q_HEADS):
        s = s_scr[hd]
        sk = sink_ref[hd]
        m = jnp.maximum(jnp.max(s, axis=-1, keepdims=True), sk)
        p = jnp.exp(s - m)
        denom = jnp.sum(p, axis=-1, keepdims=True) + jnp.exp(sk - m)
        p_scr[hd] = (p / denom).astype(BF16)
    for slab in range(N_Q_HEADS // 2):
        vs = v_sel[(2 * slab) // Q_PER_KV]
        acc = (jnp.dot(p_scr[2 * slab], vs[0], preferred_element_type=F32)
               + jnp.dot(p_scr[2 * slab + 1], vs[1], preferred_element_type=F32))
        o_ref[0, :, slab * LANES:(slab + 1) * LANES] = acc.astype(o_ref.dtype)


def _attention(proj, bias_tab, sink, qw_full, kw_full):
    b, l, _ = proj.shape
    nb = l // BAND_BLOCK
    assert nb >= 2
    kcol = COL_K // KV_WIDTH
    vcol = COL_V // KV_WIDTH
    qsum, qexp = _head_sums(N_Q_HEADS)
    ksum, kexp = _head_sums(N_KV_HEADS)

    def prev(i):
        return jnp.maximum(i - 1, 0)

    def nxt(i):
        return jnp.minimum(i + 1, nb - 1)

    def edge(i):
        return jnp.where(i == 0, 0, jnp.where(i == nb - 1, 2, 1))

    kv = lambda colb, f: pl.BlockSpec((1, BAND_BLOCK, KV_WIDTH), lambda bi, i: (bi, f(i), colb))
    same = lambda i: i
    const = lambda a: pl.BlockSpec(a.shape, lambda bi, i: (0,) * a.ndim)
    return pl.pallas_call(
        _attn_kernel,
        out_shape=jax.ShapeDtypeStruct((b, l, ATTN_WIDTH), BF16),
        grid=(b, nb),
        in_specs=[pl.BlockSpec(memory_space=pltpu.SMEM),
                  pl.BlockSpec((1, BAND_BLOCK, ATTN_WIDTH), lambda bi, i: (bi, i, COL_Q // ATTN_WIDTH)),
                  kv(kcol, prev), kv(kcol, same), kv(kcol, nxt),
                  kv(vcol, prev), kv(vcol, same), kv(vcol, nxt),
                  pl.BlockSpec((1, N_Q_HEADS, BAND_BLOCK, 3 * BAND_BLOCK), lambda bi, i: (edge(i), 0, 0, 0)),
                  const(qw_full), const(kw_full), const(qsum), const(qexp), const(ksum), const(kexp)],
        out_specs=pl.BlockSpec((1, BAND_BLOCK, ATTN_WIDTH), lambda bi, i: (bi, i, 0)),
        scratch_shapes=[pltpu.VMEM((N_Q_HEADS, BAND_BLOCK, 3 * BAND_BLOCK), F32),
                        pltpu.VMEM((N_Q_HEADS, BAND_BLOCK, 3 * BAND_BLOCK), BF16)],
        compiler_params=_cparams(("parallel", "parallel")),
        name="window_attention",
    )(sink, proj, proj, proj, proj, proj, proj, proj, bias_tab, qw_full, kw_full, qsum, qexp, ksum, kexp)


def _t5_bucket(rel):
    half = NUM_BUCKETS // 2
    max_exact = half // 2
    bucket = jnp.where(rel > 0, half, 0)
    n = jnp.abs(rel)
    nf = jnp.maximum(n, 1).astype(F32)
    large = max_exact + (jnp.log(nf / max_exact) / math.log(MAX_DISTANCE / max_exact)
                         * (half - max_exact)).astype(I32)
    large = jnp.minimum(large, half - 1)
    return bucket + jnp.where(n < max_exact, n, large)


def _bias_table(rel_bias):
    kpos = jnp.arange(3 * BAND_BLOCK)[None, :]
    rel = kpos - BAND_BLOCK - jnp.arange(BAND_BLOCK)[:, None]
    in_window = jnp.abs(rel) <= WINDOW
    bias = rel_bias[_t5_bucket(rel)].astype(F32).transpose(2, 0, 1)
    mid = jnp.where(in_window[None], bias, NEG_INF)
    first = jnp.where(kpos[None] < BAND_BLOCK, NEG_INF, mid)
    last = jnp.where(kpos[None] >= 2 * BAND_BLOCK, NEG_INF, mid)
    return jnp.stack([first, mid, last])


CONV_HALO = 16
CONV_TILE = 512


def _conv_kernel(prev_ref, cur_ref, next_ref, w_ref, b_ref, o_ref, *, nt):
    i = pl.program_id(1)
    tl = cur_ref.shape[1]
    cur = cur_ref[0].astype(F32)
    prev = prev_ref[0].astype(F32) * jnp.where(i > 0, 1.0, 0.0)
    nxt = next_ref[0].astype(F32) * jnp.where(i < nt - 1, 1.0, 0.0)
    ext = jnp.concatenate([prev, cur, nxt], axis=0)
    half = CONV_WIDTH // 2
    acc = jnp.broadcast_to(b_ref[...], cur.shape)
    for t in range(CONV_WIDTH):
        off = CONV_HALO - half + t
        acc = acc + ext[off:off + tl] * w_ref[t:t + 1, :]
    o_ref[0] = (acc * jax.nn.sigmoid(acc)).astype(o_ref.dtype)


def _conv_silu(proj, conv_w, conv_b):
    b, l, _ = proj.shape
    tl = min(l, CONV_TILE)
    nt = l // tl
    cw = CONV_TILE
    nch = CONV_CH // cw
    c0 = COL_XBC // cw
    hb = tl // CONV_HALO
    nh = l // CONV_HALO
    return pl.pallas_call(
        functools.partial(_conv_kernel, nt=nt),
        out_shape=jax.ShapeDtypeStruct((b, l, CONV_CH), BF16),
        grid=(b, nt, nch),
        in_specs=[pl.BlockSpec((1, CONV_HALO, cw), lambda bi, i, c: (bi, jnp.maximum(i * hb - 1, 0), c0 + c)),
                  pl.BlockSpec((1, tl, cw), lambda bi, i, c: (bi, i, c0 + c)),
                  pl.BlockSpec((1, CONV_HALO, cw), lambda bi, i, c: (bi, jnp.minimum((i + 1) * hb, nh - 1), c0 + c)),
                  pl.BlockSpec((CONV_WIDTH, cw), lambda bi, i, c: (0, c)),
                  pl.BlockSpec((1, cw), lambda bi, i, c: (0, c))],
        out_specs=pl.BlockSpec((1, tl, cw), lambda bi, i, c: (bi, i, c)),
        compiler_params=_cparams(("parallel", "parallel", "parallel")),
        name="conv_silu",
    )(proj, proj, proj, conv_w, conv_b.reshape(1, CONV_CH))


def _ssd_kernel(xs_ref, b_ref, c_ref, dt_ref, dtb_ref, alog_ref, e_ref, y_ref, st_ref, *, rev):
    @pl.when(pl.program_id(1) == 0)
    def _():
        st_ref[...] = jnp.zeros_like(st_ref)

    q = SSD_CHUNK
    off = SSD_HEADS if rev else 0
    z = dt_ref[0].astype(F32) + dtb_ref[...]
    dt = jnp.maximum(z, 0.0) + jnp.log1p(jnp.exp(-jnp.abs(z)))
    da = dt * (-jnp.exp(alog_ref[...]))
    ri = lax.broadcasted_iota(I32, (q, q), 0)
    ci = lax.broadcasted_iota(I32, (q, q), 1)
    if rev:
        mask = ci >= ri
        mask_t = ci <= ri
    else:
        mask = ci <= ri
        mask_t = ci >= ri
    tri = jnp.where(mask, 1.0, 0.0).astype(BF16)
    tri_t = jnp.where(mask_t, 1.0, 0.0).astype(BF16)
    cum = _dot2_left(tri, da)
    cum_t = _dot2(da.T, tri_t)
    tot = cum[0:1, :] if rev else cum[q - 1:q, :]
    e = e_ref[...]
    dt_full = _dot2(dt, e)
    cum_full = _dot2(cum, e)
    tot_full = _dot2(tot, e)
    xdt_f = xs_ref[0].astype(F32) * dt_full
    xdt = xdt_f.astype(BF16)
    xw = (xdt_f * jnp.exp(tot_full - cum_full)).astype(BF16)
    expcum = jnp.exp(cum_full)
    chunk_decay = jnp.exp(tot_full)
    gw = SSD_GROUP_WIDTH
    for g in range(SSD_GROUPS):
        bg = b_ref[0][:, g * SSD_STATE:(g + 1) * SSD_STATE]
        cg = c_ref[0][:, g * SSD_STATE:(g + 1) * SSD_STATE]
        bgt = bg.astype(F32).T.astype(BF16)
        cb = jnp.dot(cg, bgt, preferred_element_type=F32)
        ys = []
        for r in range(SSD_HEADS_PER_GROUP):
            h = g * SSD_HEADS_PER_GROUP + r
            seg = cum[:, off + h:off + h + 1] - cum_t[off + h:off + h + 1, :]
            dec = jnp.exp(jnp.where(mask, seg, NEG_INF))
            m = (cb * dec).astype(BF16)
            ys.append(jnp.dot(m, xdt[:, h * SSD_HEAD_DIM:(h + 1) * SSD_HEAD_DIM],
                              preferred_element_type=F32))
        y_diag = jnp.concatenate(ys, axis=1)
        st = st_ref[g]
        y_off = jnp.dot(cg, st.astype(BF16), preferred_element_type=F32) * expcum[:, g * gw:(g + 1) * gw]
        new = jnp.dot(bgt, xw[:, g * gw:(g + 1) * gw], preferred_element_type=F32)
        st_ref[g] = st * chunk_decay[:, g * gw:(g + 1) * gw] + new
        y_ref[0, :, g * gw:(g + 1) * gw] = (y_diag + y_off).astype(y_ref.dtype)


def _ssd_scan(xbc, proj, dtb_row, alog_row, expand, rev):
    b, l, _ = xbc.shape
    nc = l // SSD_CHUNK
    gs = SSD_GROUPS * SSD_STATE
    cidx = (lambda c: nc - 1 - c) if rev else (lambda c: c)
    return pl.pallas_call(
        functools.partial(_ssd_kernel, rev=rev),
        out_shape=jax.ShapeDtypeStruct((b, l, SSD_INNER), BF16),
        grid=(b, nc),
        in_specs=[pl.BlockSpec((1, SSD_CHUNK, SSD_INNER), lambda bi, c: (bi, cidx(c), 0)),
                  pl.BlockSpec((1, SSD_CHUNK, gs), lambda bi, c: (bi, cidx(c), SSD_INNER // gs)),
                  pl.BlockSpec((1, SSD_CHUNK, gs), lambda bi, c: (bi, cidx(c), SSD_INNER // gs + 1)),
                  pl.BlockSpec((1, SSD_CHUNK, LANES), lambda bi, c: (bi, cidx(c), COL_DT // LANES)),
                  pl.BlockSpec((1, LANES), lambda bi, c: (0, 0)),
                  pl.BlockSpec((1, LANES), lambda bi, c: (0, 0)),
                  pl.BlockSpec((LANES, SSD_INNER), lambda bi, c: (0, 0))],
        out_specs=pl.BlockSpec((1, SSD_CHUNK, SSD_INNER), lambda bi, c: (bi, cidx(c), 0)),
        scratch_shapes=[pltpu.VMEM((SSD_GROUPS, SSD_STATE, SSD_GROUP_WIDTH), F32)],
        compiler_params=_cparams(("parallel", "arbitrary")),
        name="ssd_scan_bwd" if rev else "ssd_scan_fwd",
    )(xbc, xbc, xbc, proj, dtb_row, alog_row, expand)


def _merge_kernel(x_ref, yf_ref, yb_ref, xs_ref, z_ref, ao_ref, ga_ref, gs_ref, g1_ref,
                  dsk_ref, snw_ref, wssd_ref, wattn_ref, wout_ref, o_ref):
    y = yf_ref[0].astype(F32) + yb_ref[0].astype(F32) + xs_ref[0].astype(F32) * dsk_ref[...]
    z = z_ref[0].astype(F32)
    y = y * (z * jax.nn.sigmoid(z))
    parts = []
    for g in range(SSD_GROUPS):
        yg = y[:, g * SSD_GROUP_WIDTH:(g + 1) * SSD_GROUP_WIDTH]
        parts.append(yg * lax.rsqrt(jnp.mean(yg * yg, axis=-1, keepdims=True) + EPS))
    y = jnp.concatenate(parts, axis=1) * snw_ref[...]
    ssd = jnp.dot(y.astype(BF16), wssd_ref[...], preferred_element_type=F32)
    attn = jnp.dot(ao_ref[0], wattn_ref[...], preferred_element_type=F32)
    merged = (jax.nn.sigmoid(ga_ref[0].astype(F32)) * attn
              + jax.nn.sigmoid(gs_ref[0].astype(F32)) * ssd)
    out = jnp.dot(merged.astype(BF16), wout_ref[...], preferred_element_type=F32)
    o_ref[0] = x_ref[0] + g1_ref[0] * out


def _merge(x, yf, yb, xbc, proj, attn_o, modr, dsk_full, snw, w_ssd, w_attn, w_out):
    b, l, _ = x.shape
    tl = min(l, 256)
    tok = lambda w, colb: pl.BlockSpec((1, tl, w), lambda bi, i: (bi, i, colb))
    full = lambda r, c: pl.BlockSpec((r, c), lambda bi, i: (0, 0))
    return pl.pallas_call(
        _merge_kernel,
        out_shape=jax.ShapeDtypeStruct((b, l, D_MODEL), F32),
        grid=(b, l // tl),
        in_specs=[tok(D_MODEL, 0), tok(SSD_INNER, 0), tok(SSD_INNER, 0), tok(SSD_INNER, 0),
                  tok(SSD_INNER, COL_Z // SSD_INNER), tok(ATTN_WIDTH, 0),
                  tok(D_MODEL, COL_GA // D_MODEL), tok(D_MODEL, COL_GS // D_MODEL),
                  pl.BlockSpec((1, 1, D_MODEL), lambda bi, i: (bi * N_MOD + 2, 0, 0)),
                  full(1, SSD_INNER), full(1, SSD_INNER),
                  full(SSD_INNER, D_MODEL), full(ATTN_WIDTH, D_MODEL), full(D_MODEL, D_MODEL)],
        out_specs=tok(D_MODEL, 0),
        compiler_params=_cparams(("parallel", "parallel")),
        name="merge_out_proj",
    )(x, yf, yb, xbc, proj, attn_o, proj, proj, modr, dsk_full, snw, w_ssd, w_attn, w_out)


def _topk_rows(s, k):
    rowf = lax.broadcasted_iota(I32, s.shape, 0).astype(F32)
    vals, ids = [], []
    for _ in range(k):
        m = jnp.max(s, axis=0, keepdims=True)
        am = jnp.min(jnp.where(s == m, rowf, float(s.shape[0])), axis=0, keepdims=True)
        vals.append(m)
        ids.append(am)
        s = jnp.where(rowf == am, -jnp.inf, s)
    return vals, ids


def _candidate_pieces():
    pieces = []
    for a in range(PEER_TOPK):
        nb = PEER_TOPK // (a + 1)
        if nb >= SUBLANES // 2:
            for b0 in range(0, nb, SUBLANES):
                pieces.append((a, 1, b0, SUBLANES))
    a_done = max(p[0] for p in pieces) + 1
    for b in range(PEER_TOPK):
        na = PEER_TOPK // (b + 1)
        if na > a_done:
            for a0 in range(0, na, SUBLANES):
                pieces.append((a0, SUBLANES, b, 1))
    return pieces, a_done


def _route_kernel(x_ref, sh_ref, sc_ref, nw_ref, wq_ref, keys_ref, hp_ref, idx_ref, gate_ref, cnt_ref):
    h = _norm_mod(x_ref[0], nw_ref[...], sc_ref[0], sh_ref[0])
    hp_ref[0] = h
    qall = jnp.dot(h.astype(BF16), wq_ref[...], preferred_element_type=F32)
    ntok = qall.shape[0]
    tops, topi = [], []
    for c in range(2 * PEER_HEADS):
        qc = qall[:, c * PEER_HALF:(c + 1) * PEER_HALF].astype(BF16)
        s = lax.dot_general(keys_ref[c], qc, (((1,), (1,)), ((), ())), preferred_element_type=F32)
        vals, ids = _topk_rows(s, PEER_TOPK)
        tops.append(vals)
        topi.append(ids)
    pieces, a_done = _candidate_pieces()
    sub = lax.broadcasted_iota(I32, (SUBLANES, ntok), 0)
    subf = sub.astype(F32)
    pos_parts, drop_parts = [], []
    for a0, na, b0, nb in pieces:
        if na == 1:
            pos_parts.append(subf + float(a0 * PEER_TOPK + b0))
            keep = sub + b0 < PEER_TOPK // (a0 + 1)
        else:
            pos_parts.append(subf * float(PEER_TOPK) + float(a0 * PEER_TOPK + b0))
            keep = jnp.where(sub + a0 >= a_done, sub + a0, PEER_TOPK) < PEER_TOPK // (b0 + 1)
        drop_parts.append(jnp.where(keep, 0.0, -jnp.inf))
    pos = jnp.concatenate(pos_parts, axis=0)
    drop = jnp.concatenate(drop_parts, axis=0)
    idx_rows, gate_rows = [], []
    for hd in range(PEER_HEADS):
        s0 = jnp.concatenate(tops[2 * hd], axis=0)
        i0 = jnp.concatenate(topi[2 * hd], axis=0)
        s1 = jnp.concatenate(tops[2 * hd + 1], axis=0)
        i1 = jnp.concatenate(topi[2 * hd + 1], axis=0)
        cparts, iparts = [], []
        for a0, na, b0, nb in pieces:
            if na == 1:
                cparts.append(tops[2 * hd][a0] + s1[b0:b0 + nb])
                iparts.append(topi[2 * hd][a0] * float(N_KEYS) + i1[b0:b0 + nb])
            else:
                cparts.append(s0[a0:a0 + na] + tops[2 * hd + 1][b0])
                iparts.append(i0[a0:a0 + na] * float(N_KEYS) + topi[2 * hd + 1][b0])
        cand = jnp.concatenate(cparts, axis=0) + drop
        cidx = jnp.concatenate(iparts, axis=0)
        best = []
        for _ in range(PEER_TOPK):
            m = jnp.max(cand, axis=0, keepdims=True)
            first = jnp.min(jnp.where(cand == m, pos, float(PEER_TOPK * PEER_TOPK)), axis=0, keepdims=True)
            sel = pos == first
            idx_rows.append(jnp.max(jnp.where(sel, cidx, -1.0), axis=0, keepdims=True))
            best.append(m)
            cand = jnp.where(sel, -jnp.inf, cand)
        bs = jnp.concatenate(best, axis=0)
        p = jnp.exp(bs - best[0])
        gate_rows.append(p / jnp.sum(p, axis=0, keepdims=True))
    idx_t = jnp.concatenate(idx_rows, axis=0)
    gate_t = jnp.concatenate(gate_rows, axis=0)
    idx_t, gate_t, n_zero = _partition_by_class(idx_t, gate_t)
    idx_ref[0] = (idx_t * float(SLAB_ROWS)).T.astype(I32)
    gate_ref[0] = gate_t.T
    cnt_ref[0, 0] = n_zero.astype(I32)


def _route(x1, modr, norm_w, wq, keys):
    b, l, _ = x1.shape
    tl = min(l, 256)
    tok = lambda w: pl.BlockSpec((1, tl, w), lambda bi, i: (bi, i, 0))
    return pl.pallas_call(
        _route_kernel,
        out_shape=(jax.ShapeDtypeStruct((b, l, D_MODEL), F32),
                   jax.ShapeDtypeStruct((b, l, N_SEL), I32),
                   jax.ShapeDtypeStruct((b, l, N_SEL), F32),
                   jax.ShapeDtypeStruct((b, l // tl, 1, tl), I32)),
        grid=(b, l // tl),
        in_specs=[tok(D_MODEL),
                  pl.BlockSpec((1, 1, D_MODEL), lambda bi, i: (bi * N_MOD + 3, 0, 0)),
                  pl.BlockSpec((1, 1, D_MODEL), lambda bi, i: (bi * N_MOD + 4, 0, 0)),
                  pl.BlockSpec((1, D_MODEL), lambda bi, i: (0, 0)),
                  pl.BlockSpec((D_MODEL, PEER_HEADS * PEER_KEY_DIM), lambda bi, i: (0, 0)),
                  pl.BlockSpec((2 * PEER_HEADS, N_KEYS, PEER_HALF), lambda bi, i: (0, 0, 0))],
        out_specs=(tok(D_MODEL), tok(N_SEL), tok(N_SEL),
                   pl.BlockSpec((1, 1, 1, tl), lambda bi, i: (bi, i, 0, 0))),
        compiler_params=_cparams(("parallel", "parallel")),
        name="peer_route",
    )(x1, modr, modr, norm_w, wq, keys)


def _pack_rows(tab):
    n = tab.shape[0]
    bits = lax.bitcast_convert_type(tab.astype(BF16), jnp.uint16).astype(jnp.uint32)
    lo = bits[:, :D_MODEL // 2]
    hi = bits[:, D_MODEL // 2:]
    words = lax.bitcast_convert_type(lo | (hi << 16), I32)
    return words.reshape(n, ROW_WORDS, LANES)


def _gelu_tanh(x):
    return 0.5 * x * (1.0 + jnp.tanh(math.sqrt(2.0 / math.pi) * (x + 0.044715 * (x * x * x))))


def _unpack_rows(words):
    return pltpu.bitcast(words << 16, F32), pltpu.bitcast(words & jnp.int32(-65536), F32)


SLAB_ROWS = 2 * ROW_WORDS
PEER_CHUNK = 16
PEER_COMMON_CHUNKS = 5


def _expert_class(e):
    return lax.population_count(e) & 1


def _class_tables(tab_u, tab_v):
    slabs = jnp.stack([_pack_rows(tab_u), _pack_rows(tab_v)], axis=2).reshape(N_EXPERTS, SLAB_ROWS, LANES)
    even = 2 * jnp.arange(N_EXPERTS // 2, dtype=I32)
    first = even + _expert_class(even)
    second = even + 1 - _expert_class(even)
    shape = (N_EXPERTS // 2 * SLAB_ROWS, LANES)
    return slabs[first].reshape(shape), slabs[second].reshape(shape)


def _compact_rows(vals, shift, valid, up):
    n = valid.shape[0]
    for bit in range(n.bit_length() - 1):
        step = 1 << bit
        roll = (lambda a: pltpu.roll(a, (n - step) if up else step, axis=0))
        move = valid * ((shift >> bit) & 1)
        bring = roll(move) != 0
        vals = [jnp.where(bring, roll(v), v) for v in vals]
        shift = jnp.where(bring, roll(shift), shift)
        valid = jnp.where(bring, 1, valid - move)
    return vals


def _partition_by_class(idx_t, gate_t):
    n = idx_t.shape[0]
    e = idx_t.astype(I32)
    cls = _expert_class(e)
    local = (e >> 1).astype(F32)
    is_one = cls.astype(F32).astype(BF16)
    is_zero = (1 - cls).astype(F32).astype(BF16)
    ri = lax.broadcasted_iota(I32, (n, n), 0)
    ci = lax.broadcasted_iota(I32, (n, n), 1)
    before = jnp.where(ci < ri, 1.0, 0.0).astype(BF16)
    after = jnp.where(ci > ri, 1.0, 0.0).astype(BF16)
    ones_before = jnp.dot(before, is_one, preferred_element_type=F32).astype(I32)
    zeros_after = jnp.dot(after, is_zero, preferred_element_type=F32).astype(I32)
    n_zero = jnp.sum(is_zero.astype(F32), axis=0, keepdims=True)
    z_idx, z_gate = _compact_rows([local, gate_t], ones_before, 1 - cls, up=True)
    o_idx, o_gate = _compact_rows([local, gate_t], zeros_after, cls, up=False)
    first = lax.broadcasted_iota(I32, idx_t.shape, 0).astype(F32) < n_zero
    return jnp.where(first, z_idx, o_idx), jnp.where(first, z_gate, o_gate), n_zero


def _token_tile(rows, tt):
    return jnp.concatenate([rows[tt:tt + 1, j * LANES:(j + 1) * LANES] for j in range(D_MODEL // LANES)],
                           axis=0)


def _rows_from_tiles(tiles8):
    return jnp.concatenate([jnp.concatenate([t8[j:j + 1] for t8 in tiles8], axis=0)
                            for j in range(D_MODEL // LANES)], axis=1)


def _class_pass_parts(idx_ref, x_ref, tab_ref, rep_ref, ones_ref, *, cls, s_lo, s_hi):
    lane = lax.broadcasted_iota(I32, (1, N_SEL), 1)
    in_range = jnp.logical_and(lane >= s_lo * PEER_CHUNK, lane < s_hi * PEER_CHUNK)
    crows = PEER_CHUNK * SLAB_ROWS
    nsel = (s_hi - s_lo) * PEER_CHUNK
    nrows = nsel * ROW_WORDS
    half = SUBLANES // 2

    def gather_chunk(tile, base, c):
        k0 = base + c * PEER_CHUNK
        r0 = (c - s_lo) * crows
        r0 = r0 if isinstance(c, int) else pl.multiple_of(r0, crows)
        for kk in range(PEER_CHUNK):
            r = pl.multiple_of(idx_ref[k0 + kk], SLAB_ROWS)
            tile[pl.ds(r0 + kk * SLAB_ROWS, SLAB_ROWS), :] = tab_ref[pl.ds(r, SLAB_ROWS), :]

    def valid_row(n_zero):
        mine = (lane >= n_zero) if cls else (lane < n_zero)
        return jnp.where(jnp.logical_and(mine, in_range), 1.0, 0.0)

    def needed_chunks(n_zero):
        if cls:
            return jnp.maximum(n_zero // PEER_CHUNK, s_lo), s_hi
        return s_lo, jnp.minimum((n_zero + PEER_CHUNK - 1) // PEER_CHUNK, s_hi)

    def activation(tile, x8):
        acc = None
        for s in range(ROW_WORDS):
            u_lo, u_hi = _unpack_rows(tile[pl.ds(2 * s, nsel, stride=SLAB_ROWS), :])
            term = u_lo * x8[s:s + 1] + u_hi * x8[half + s:half + s + 1]
            acc = term if acc is None else acc + term
        before, after = s_lo * PEER_CHUNK, N_SEL - s_hi * PEER_CHUNK
        full = jnp.concatenate(([jnp.zeros((before, LANES), F32)] if before else []) + [acc]
                               + ([jnp.zeros((after, LANES), F32)] if after else []), axis=0)
        return jnp.sum(full.T, axis=0, keepdims=True)

    def weighted_sum(tile, w_row):
        v_lo, v_hi = _unpack_rows(tile[pl.ds(1, nrows, stride=2), :])
        wrows = jnp.dot((rep_ref[...] * w_row).astype(BF16), ones_ref[...], preferred_element_type=F32)
        acc_lo = jnp.sum((v_lo * wrows).reshape(nrows // SUBLANES, SUBLANES, LANES), axis=0)
        acc_hi = jnp.sum((v_hi * wrows).reshape(nrows // SUBLANES, SUBLANES, LANES), axis=0)
        return jnp.concatenate([acc_lo[:half] + acc_lo[half:], acc_hi[:half] + acc_hi[half:]], axis=0)

    return gather_chunk, valid_row, needed_chunks, activation, weighted_sum


def _class_body(idx_ref, cnt_ref, x_ref, gate_ref, last, tab_ref, rep_ref, rep_rare_ref, ones_ref,
                out_ref, tiles, wbufs, tile_rare, *, cls):
    nchunk = N_SEL // PEER_CHUNK
    if cls:
        s_lo, s_hi, r_lo, r_hi = nchunk - PEER_COMMON_CHUNKS, nchunk, 0, nchunk - PEER_COMMON_CHUNKS
    else:
        s_lo, s_hi, r_lo, r_hi = 0, PEER_COMMON_CHUNKS, PEER_COMMON_CHUNKS, nchunk
    gather_chunk, valid_row, _, activation, weighted_sum = _class_pass_parts(
        idx_ref, x_ref, tab_ref, rep_ref, ones_ref, cls=cls, s_lo=s_lo, s_hi=s_hi)
    if last:
        prev_ref, x1_ref, g2_ref = last

    def group_rows(t0):
        return pl.ds(t0, PEER_TOK_GROUP)

    def emit(t0, tiles8):
        rows = _rows_from_tiles(tiles8)
        if last:
            rows = x1_ref[group_rows(t0), :] + g2_ref[0] * (prev_ref[group_rows(t0), :] + rows)
        out_ref[group_rows(t0), :] = rows

    def emit_more(t0, tt, o8):
        zero = jnp.zeros_like(o8)
        rows = _rows_from_tiles([o8 if i == tt else zero for i in range(PEER_TOK_GROUP)])
        out_ref[group_rows(t0), :] = out_ref[group_rows(t0), :] + (g2_ref[0] * rows if last else rows)

    @pl.when(pl.program_id(0) == 0)
    def _():
        for buf in tiles + [tile_rare]:
            buf[...] = jnp.zeros_like(buf)

    def first_stage(g, par):
        t0 = pl.multiple_of(g * PEER_TOK_GROUP, PEER_TOK_GROUP)
        x_rows = x_ref[group_rows(t0), :]
        acts, valid = [], []

        def token(tt):
            valid.append(valid_row(cnt_ref[t0 + tt]))
            for c in range(s_lo, s_hi):
                gather_chunk(tiles[par].at[tt], (t0 + tt) * N_SEL, c)
            acts.append(activation(tiles[par].at[tt], _token_tile(x_rows, tt)))

        def finish():
            act = jnp.concatenate(acts, axis=0)
            wbufs[par][...] = (gate_ref[group_rows(t0), :] * _gelu_tanh(act) * jnp.concatenate(valid, axis=0))

        return token, finish

    def second_stage(g, par):
        t0 = pl.multiple_of(g * PEER_TOK_GROUP, PEER_TOK_GROUP)
        w = wbufs[par][...]
        sums = []

        def token(tt):
            sums.append(weighted_sum(tiles[par].at[tt], w[tt:tt + 1, :]))

        return token, lambda: emit(t0, sums)

    def step(first, second):
        tok1, fin1 = first_stage(*first) if first else (None, None)
        tok2, fin2 = second_stage(*second) if second else (None, None)
        for tt in range(PEER_TOK_GROUP):
            if tok1:
                tok1(tt)
            if tok2:
                tok2(tt)
        if fin2:
            fin2()
        if fin1:
            fin1()

    ngroups = PEER_BLOCK // PEER_TOK_GROUP
    step((0, 0), None)

    def pair(j, carry):
        step((2 * j + 1, 1), (2 * j, 0))
        step((2 * j + 2, 0), (2 * j + 1, 1))
        return carry

    lax.fori_loop(0, ngroups // 2 - 1, pair, 0)
    step((ngroups - 1, 1), (ngroups - 2, 0))
    step(None, (ngroups - 1, 1))

    r_gather, r_valid, r_chunks, r_activation, r_weighted_sum = _class_pass_parts(
        idx_ref, x_ref, tab_ref, rep_rare_ref, ones_ref, cls=cls, s_lo=r_lo, s_hi=r_hi)

    def needs_rare(n_zero):
        return (n_zero < r_hi * PEER_CHUNK) if cls else (n_zero > r_lo * PEER_CHUNK)

    def rare_token(t0, tt):
        n_zero = cnt_ref[t0 + tt]

        @pl.when(needs_rare(n_zero))
        def _():
            def chunk(c, carry):
                r_gather(tile_rare, (t0 + tt) * N_SEL, c)
                return carry
            lax.fori_loop(*r_chunks(n_zero), chunk, 0)
            act = r_activation(tile_rare, _token_tile(x_ref[group_rows(t0), :], tt))
            w = gate_ref[group_rows(t0), :][tt:tt + 1] * _gelu_tanh(act) * r_valid(n_zero)
            emit_more(t0, tt, r_weighted_sum(tile_rare, w))

    def rare_group(g, carry):
        t0 = pl.multiple_of(g * PEER_TOK_GROUP, PEER_TOK_GROUP)
        counts = [cnt_ref[t0 + tt] for tt in range(PEER_TOK_GROUP)]
        extreme = functools.reduce(jnp.minimum if cls else jnp.maximum, counts)

        @pl.when(needs_rare(extreme))
        def _():
            for tt in range(PEER_TOK_GROUP):
                rare_token(t0, tt)
        return carry

    lax.fori_loop(0, ngroups, rare_group, 0)


def _class_first_kernel(idx_ref, cnt_ref, x_ref, gate_ref, tab_ref, rep_ref, rep_rare_ref, ones_ref,
                        out_ref, tile0, tile1, wbuf0, wbuf1, tile_rare, *, cls):
    _class_body(idx_ref, cnt_ref, x_ref, gate_ref, None, tab_ref, rep_ref, rep_rare_ref, ones_ref,
                out_ref, [tile0, tile1], [wbuf0, wbuf1], tile_rare, cls=cls)


def _class_last_kernel(idx_ref, cnt_ref, x_ref, gate_ref, prev_ref, x1_ref, g2_ref, tab_ref, rep_ref,
                       rep_rare_ref, ones_ref, out_ref, tile0, tile1, wbuf0, wbuf1, tile_rare, *, cls):
    _class_body(idx_ref, cnt_ref, x_ref, gate_ref, (prev_ref, x1_ref, g2_ref), tab_ref, rep_ref,
                rep_rare_ref, ones_ref, out_ref, [tile0, tile1], [wbuf0, wbuf1], tile_rare, cls=cls)


def _class_pass(idx_flat, cnt_flat, h, gates, last, tab, *, cls, seq_len):
    t = gates.shape[0]
    ncommon = PEER_COMMON_CHUNKS * PEER_CHUNK
    nrare = N_SEL - ncommon

    def rep(first_sel, nsel):
        rows = jnp.arange(nsel * ROW_WORDS)[:, None] // ROW_WORDS + first_sel
        return (rows == jnp.arange(N_SEL)[None, :]).astype(F32)

    rep_common = rep(nrare if cls else 0, ncommon)
    rep_rare = rep(0 if cls else ncommon, nrare)
    ones = jnp.ones((N_SEL, LANES), BF16)
    const = lambda a: pl.BlockSpec(a.shape, lambda i: (0, 0))
    tok_spec = pl.BlockSpec((PEER_BLOCK, D_MODEL), lambda i: (i, 0))
    in_specs = [pl.BlockSpec((PEER_BLOCK * N_SEL,), lambda i: (i,), memory_space=pltpu.SMEM),
                pl.BlockSpec((PEER_BLOCK,), lambda i: (i,), memory_space=pltpu.SMEM),
                tok_spec, pl.BlockSpec((PEER_BLOCK, N_SEL), lambda i: (i, 0))]
    args = [idx_flat, cnt_flat, h, gates]
    if last is not None:
        gate_row = lambda i: ((i * PEER_BLOCK // seq_len) * N_MOD + N_MOD - 1, 0, 0)
        in_specs += [tok_spec, tok_spec, pl.BlockSpec((1, 1, D_MODEL), gate_row)]
        args += list(last)
    in_specs += [pl.BlockSpec(memory_space=pltpu.VMEM), const(rep_common), const(rep_rare), const(ones)]
    args += [tab, rep_common, rep_rare, ones]
    tile = pltpu.VMEM((PEER_TOK_GROUP, ncommon * SLAB_ROWS, LANES), I32)
    wbuf = pltpu.VMEM((PEER_TOK_GROUP, N_SEL), F32)
    return pl.pallas_call(
        functools.partial(_class_first_kernel if last is None else _class_last_kernel, cls=cls),
        out_shape=jax.ShapeDtypeStruct((t, D_MODEL), F32),
        grid=(t // PEER_BLOCK,),
        in_specs=in_specs,
        out_specs=tok_spec,
        scratch_shapes=[tile, tile, wbuf, wbuf, pltpu.VMEM((nrare * SLAB_ROWS, LANES), I32)],
        compiler_params=_cparams(("arbitrary",)),
        name="peer_class%d" % cls,
    )(*args)


def _prepare(rel_bias, ada_w, ada_b, norm1_w, norm2_w, w_in, q_norm_w, k_norm_w, attn_sink, conv_w,
             conv_b, a_log, dt_bias, d_skip, ssd_norm_w, w_attn_br, w_ssd_br, w_out, peer_wq,
             peer_keys, peer_u, peer_v):
    lyr = 0
    w = w_in[lyr]
    o = 0
    parts = {}
    for name, width in (("q", ATTN_WIDTH), ("k", KV_WIDTH), ("v", KV_WIDTH), ("z", SSD_INNER),
                        ("xbc", CONV_CH), ("dt", 2 * SSD_HEADS), ("ga", D_MODEL), ("gs", D_MODEL)):
        parts[name] = w[:, o:o + width]
        o += width
    pad = jnp.zeros((D_MODEL, PROJ_W - (COL_DT + 2 * SSD_HEADS)), w.dtype)
    w_all = jnp.concatenate([parts["z"], parts["q"], parts["ga"], parts["gs"], parts["xbc"],
                             parts["k"], parts["v"], parts["dt"], pad], axis=1).astype(BF16)
    lane_pad = LANES - 2 * SSD_HEADS
    expand = (jnp.arange(SSD_INNER)[None, :] // SSD_HEAD_DIM == jnp.arange(LANES)[:, None])
    tab0, tab1 = _class_tables(peer_u[lyr], peer_v[lyr])
    return dict(
        ada_w=ada_w[lyr], ada_b=ada_b[lyr],
        norm1_w=norm1_w[lyr].reshape(1, D_MODEL), norm2_w=norm2_w[lyr].reshape(1, D_MODEL),
        w_all=w_all,
        bias_tab=_bias_table(rel_bias), sink=attn_sink[lyr].astype(F32),
        qw=(jnp.tile(q_norm_w[lyr], N_Q_HEADS) * HEAD_DIM ** -0.5).reshape(1, ATTN_WIDTH),
        kw=jnp.tile(k_norm_w[lyr], N_KV_HEADS).reshape(1, KV_WIDTH),
        conv_w=conv_w[lyr], conv_b=conv_b[lyr],
        dtb_row=jnp.pad(dt_bias[lyr].astype(F32).reshape(1, -1), ((0, 0), (0, lane_pad))),
        alog_row=jnp.pad(a_log[lyr].astype(F32).reshape(1, -1), ((0, 0), (0, lane_pad))),
        expand_fwd=expand.astype(BF16),
        expand_bwd=jnp.roll(expand, SSD_HEADS, axis=0).astype(BF16),
        dsk_full=jnp.repeat(d_skip[lyr], SSD_HEAD_DIM).reshape(1, SSD_INNER),
        snw=ssd_norm_w[lyr].reshape(1, SSD_INNER),
        w_ssd=w_ssd_br[lyr].astype(BF16), w_attn=w_attn_br[lyr].astype(BF16), w_out=w_out[lyr].astype(BF16),
        wq=peer_wq[lyr].astype(BF16),
        keys=peer_keys[lyr].reshape(2 * PEER_HEADS, N_KEYS, PEER_HALF).astype(BF16),
        tab0=tab0, tab1=tab1,
    )


def _token_mixer_stage(x, modr, p):
    proj = _in_projection(x, modr, p["norm1_w"], p["w_all"])
    attn_o = _attention(proj, p["bias_tab"], p["sink"], p["qw"], p["kw"])
    xbc = _conv_silu(proj, p["conv_w"], p["conv_b"])
    yf = _ssd_scan(xbc, proj, p["dtb_row"], p["alog_row"], p["expand_fwd"], rev=False)
    yb = _ssd_scan(xbc, proj, p["dtb_row"], p["alog_row"], p["expand_bwd"], rev=True)
    return _merge(x, yf, yb, xbc, proj, attn_o, modr, p["dsk_full"], p["snw"],
                  p["w_ssd"], p["w_attn"], p["w_out"])


def _peer_stage(x1, modr, p):
    b, l, _ = x1.shape
    t = b * l
    assert l % PEER_BLOCK == 0
    h, idx, gates, cnt = _route(x1, modr, p["norm2_w"], p["wq"], p["keys"])
    args = (idx.reshape(t * N_SEL), cnt.reshape(t), h.reshape(t, D_MODEL), gates.reshape(t, N_SEL))
    out = _class_pass(*args, None, p["tab0"], cls=0, seq_len=l)
    y = _class_pass(*args, (out, x1.reshape(t, D_MODEL), modr), p["tab1"], cls=1, seq_len=l)
    return y.reshape(b, l, D_MODEL)


def _encoder(x, c, p):
    nb = c.shape[0]
    modr = _modulation(c, p["ada_w"], p["ada_b"]).reshape(nb * N_MOD, 1, D_MODEL)
    x1 = _token_mixer_stage(x, modr, p)
    return _peer_stage(x1, modr, p)


def kernel(x_prompt, x_sample, c_prompt, c_sample, rel_bias, ada_w, ada_b, norm1_w, norm2_w, w_in,
           q_norm_w, k_norm_w, attn_sink, conv_w, conv_b, a_log, dt_bias, d_skip, ssd_norm_w,
           w_attn_br, w_ssd_br, w_out, peer_wq, peer_keys, peer_u, peer_v):
    p = _prepare(rel_bias, ada_w, ada_b, norm1_w, norm2_w, w_in, q_norm_w, k_norm_w, attn_sink,
                 conv_w, conv_b, a_log, dt_bias, d_skip, ssd_norm_w, w_attn_br, w_ssd_br, w_out,
                 peer_wq, peer_keys, peer_u, peer_v)
    return (_encoder(x_prompt, c_prompt, p), _encoder(x_sample, c_sample, p))
```

```python
import functools
import math

import jax
import jax.numpy as jnp
from jax import lax
from jax.experimental import pallas as pl
from jax.experimental.pallas import tpu as pltpu

F32 = jnp.float32
BF16 = jnp.bfloat16
I32 = jnp.int32

D_MODEL = 1024
HEAD_DIM = 64
N_Q_HEADS = 16
N_KV_HEADS = 4
Q_PER_KV = N_Q_HEADS // N_KV_HEADS
ATTN_WIDTH = N_Q_HEADS * HEAD_DIM
KV_WIDTH = N_KV_HEADS * HEAD_DIM
WINDOW = 128
BAND_BLOCK = 128
NUM_BUCKETS = 32
MAX_DISTANCE = 128
NEG_INF = -1e30

SSD_INNER = 2 * D_MODEL
SSD_HEAD_DIM = 64
SSD_HEADS = SSD_INNER // SSD_HEAD_DIM
SSD_GROUPS = 4
SSD_HEADS_PER_GROUP = SSD_HEADS // SSD_GROUPS
SSD_STATE = 128
SSD_CHUNK = 128
SSD_GROUP_WIDTH = SSD_INNER // SSD_GROUPS
CONV_WIDTH = 5
CONV_CH = SSD_INNER + 2 * SSD_GROUPS * SSD_STATE

PEER_HEADS = 8
PEER_KEY_DIM = 256
PEER_HALF = PEER_KEY_DIM // 2
N_KEYS = 128
N_EXPERTS = N_KEYS * N_KEYS
PEER_TOPK = 16
N_SEL = PEER_HEADS * PEER_TOPK

N_MOD = 6
EPS = 1e-6

COL_Z = 0
COL_Q = 2048
COL_GA = 3072
COL_GS = 4096
COL_XBC = 5120
COL_K = 8192
COL_V = 8448
COL_DT = 8704
PROJ_W = 9216

LANES = 128
SUBLANES = 8
VMEM_LIMIT = 56 * 1024 * 1024
ROW_WORDS = D_MODEL // (2 * LANES)
PEER_TOK_GROUP = 8
PEER_BLOCK = 128


def _cparams(sem):
    return pltpu.CompilerParams(dimension_semantics=sem, vmem_limit_bytes=VMEM_LIMIT)


def _split_bf16(v):
    hi = v.astype(BF16)
    lo = (v - hi.astype(F32)).astype(BF16)
    return hi, lo


def _dot2(v, m_bf16):
    hi, lo = _split_bf16(v)
    return (jnp.dot(hi, m_bf16, preferred_element_type=F32)
            + jnp.dot(lo, m_bf16, preferred_element_type=F32))


def _dot2_left(m_bf16, v):
    hi, lo = _split_bf16(v)
    return (jnp.dot(m_bf16, hi, preferred_element_type=F32)
            + jnp.dot(m_bf16, lo, preferred_element_type=F32))


def _mod_kernel(c_ref, w_ref, b_ref, o_ref):
    c = c_ref[...]
    sc = c * jax.nn.sigmoid(c)
    o_ref[...] = jnp.dot(sc, w_ref[...], preferred_element_type=F32,
                         precision=lax.Precision.HIGHEST) + b_ref[...]


def _modulation(c, ada_w, ada_b):
    nb = c.shape[0]
    n = ada_w.shape[1]
    tn = 1024
    return pl.pallas_call(
        _mod_kernel,
        out_shape=jax.ShapeDtypeStruct((nb, n), F32),
        grid=(n // tn,),
        in_specs=[pl.BlockSpec((nb, D_MODEL), lambda j: (0, 0)),
                  pl.BlockSpec((D_MODEL, tn), lambda j: (0, j)),
                  pl.BlockSpec((1, tn), lambda j: (0, j))],
        out_specs=pl.BlockSpec((nb, tn), lambda j: (0, j)),
        compiler_params=_cparams(("arbitrary",)),
        name="adaln_mod",
    )(c, ada_w, ada_b.reshape(1, n))


def _norm_mod(x, nw, sc, sh):
    ms = jnp.mean(x * x, axis=-1, keepdims=True)
    h = x * lax.rsqrt(ms + EPS) * nw
    return h * (1.0 + sc) + sh


def _inproj_kernel(x_ref, sh_ref, sc_ref, nw_ref, w_ref, o_ref, h_scr):
    @pl.when(pl.program_id(2) == 0)
    def _():
        h = _norm_mod(x_ref[0], nw_ref[...], sc_ref[0], sh_ref[0])
        h_scr[...] = h.astype(BF16)

    o_ref[0] = jnp.dot(h_scr[...], w_ref[...], preferred_element_type=F32).astype(o_ref.dtype)


def _in_projection(x, modr, norm_w, w_all):
    b, l, _ = x.shape
    tl = min(l, 1024)
    tn = 1024
    return pl.pallas_call(
        _inproj_kernel,
        out_shape=jax.ShapeDtypeStruct((b, l, PROJ_W), BF16),
        grid=(b, l // tl, PROJ_W // tn),
        in_specs=[pl.BlockSpec((1, tl, D_MODEL), lambda bi, i, j: (bi, i, 0)),
                  pl.BlockSpec((1, 1, D_MODEL), lambda bi, i, j: (bi * N_MOD + 0, 0, 0)),
                  pl.BlockSpec((1, 1, D_MODEL), lambda bi, i, j: (bi * N_MOD + 1, 0, 0)),
                  pl.BlockSpec((1, D_MODEL), lambda bi, i, j: (0, 0)),
                  pl.BlockSpec((D_MODEL, tn), lambda bi, i, j: (0, j))],
        out_specs=pl.BlockSpec((1, tl, tn), lambda bi, i, j: (bi, i, j)),
        scratch_shapes=[pltpu.VMEM((tl, D_MODEL), BF16)],
        compiler_params=_cparams(("parallel", "parallel", "arbitrary")),
        name="in_projection",
    )(x, modr, modr, norm_w, w_all)


def _head_sums(n_heads):
    c = jnp.arange(n_heads * HEAD_DIM)
    hsum = (c[:, None] // HEAD_DIM == jnp.arange(LANES)[None, :]).astype(BF16)
    return hsum, hsum.T


def _qk_norm(t, hsum, hexp, w_full):
    ssq = _dot2(t * t, hsum)
    inv = lax.rsqrt(ssq * (1.0 / HEAD_DIM) + EPS)
    return t * _dot2(inv, hexp) * w_full


def _attn_kernel(sink_ref, q_ref, kp_ref, kc_ref, kn_ref, vp_ref, vc_ref, vn_ref, bias_ref,
                 qw_ref, kw_ref, qsum_ref, qexp_ref, ksum_ref, kexp_ref, o_ref, s_scr, p_scr):
    q = _qk_norm(q_ref[0].astype(F32), qsum_ref[...], qexp_ref[...], qw_ref[...]).astype(BF16)
    k = jnp.concatenate([kp_ref[0], kc_ref[0], kn_ref[0]], axis=0).astype(F32)
    k = _qk_norm(k, ksum_ref[...], kexp_ref[...], kw_ref[...])
    v = jnp.concatenate([vp_ref[0], vc_ref[0], vn_ref[0]], axis=0).astype(F32)
    low = lax.broadcasted_iota(I32, (1, LANES), 1) < HEAD_DIM
    nt = (((1,), (1,)), ((), ()))
    k_sel, v_sel = [], []
    for c in range(N_KV_HEADS // 2):
        kslab = k[:, c * LANES:(c + 1) * LANES]
        vslab = v[:, c * LANES:(c + 1) * LANES]
        kroll = pltpu.roll(kslab, HEAD_DIM, axis=1)
        vroll = pltpu.roll(vslab, HEAD_DIM, axis=1)
        for e in range(2):
            k_sel.append((jnp.where(low, kroll if e else kslab, 0.0).astype(BF16),
                          jnp.where(low, 0.0, kslab if e else kroll).astype(BF16)))
            v_sel.append((jnp.where(low, vroll if e else vslab, 0.0).astype(BF16),
                          jnp.where(low, 0.0, vslab if e else vroll).astype(BF16)))
    for hd in range(N_Q_HEADS):
        qs = q[:, (hd // 2) * LANES:(hd // 2 + 1) * LANES]
        s = lax.dot_general(qs, k_sel[hd // Q_PER_KV][hd % 2], nt, preferred_element_type=F32)
        s_scr[hd] = s + bias_ref[0, hd]
    for hd in range(N_Q_HEADS):
        s = s_scr[hd]
        sk = sink_ref[hd]
        m = jnp.maximum(jnp.max(s, axis=-1, keepdims=True), sk)
        p = jnp.exp(s - m)
        denom = jnp.sum(p, axis=-1, keepdims=True) + jnp.exp(sk - m)
        p_scr[hd] = (p / denom).astype(BF16)
    for slab in range(N_Q_HEADS // 2):
        vs = v_sel[(2 * slab) // Q_PER_KV]
        acc = (jnp.dot(p_scr[2 * slab], vs[0], preferred_element_type=F32)
               + jnp.dot(p_scr[2 * slab + 1], vs[1], preferred_element_type=F32))
        o_ref[0, :, slab * LANES:(slab + 1) * LANES] = acc.astype(o_ref.dtype)


def _attention(proj, bias_tab, sink, qw_full, kw_full):
    b, l, _ = proj.shape
    nb = l // BAND_BLOCK
    assert nb >= 2
    kcol = COL_K // KV_WIDTH
    vcol = COL_V // KV_WIDTH
    qsum, qexp = _head_sums(N_Q_HEADS)
    ksum, kexp = _head_sums(N_KV_HEADS)

    def prev(i):
        return jnp.maximum(i - 1, 0)

    def nxt(i):
        return jnp.minimum(i + 1, nb - 1)

    def edge(i):
        return jnp.where(i == 0, 0, jnp.where(i == nb - 1, 2, 1))

    kv = lambda colb, f: pl.BlockSpec((1, BAND_BLOCK, KV_WIDTH), lambda bi, i: (bi, f(i), colb))
    same = lambda i: i
    const = lambda a: pl.BlockSpec(a.shape, lambda bi, i: (0,) * a.ndim)
    return pl.pallas_call(
        _attn_kernel,
        out_shape=jax.ShapeDtypeStruct((b, l, ATTN_WIDTH), BF16),
        grid=(b, nb),
        in_specs=[pl.BlockSpec(memory_space=pltpu.SMEM),
                  pl.BlockSpec((1, BAND_BLOCK, ATTN_WIDTH), lambda bi, i: (bi, i, COL_Q // ATTN_WIDTH)),
                  kv(kcol, prev), kv(kcol, same), kv(kcol, nxt),
                  kv(vcol, prev), kv(vcol, same), kv(vcol, nxt),
                  pl.BlockSpec((1, N_Q_HEADS, BAND_BLOCK, 3 * BAND_BLOCK), lambda bi, i: (edge(i), 0, 0, 0)),
                  const(qw_full), const(kw_full), const(qsum), const(qexp), const(ksum), const(kexp)],
        out_specs=pl.BlockSpec((1, BAND_BLOCK, ATTN_WIDTH), lambda bi, i: (bi, i, 0)),
        scratch_shapes=[pltpu.VMEM((N_Q_HEADS, BAND_BLOCK, 3 * BAND_BLOCK), F32),
                        pltpu.VMEM((N_Q_HEADS, BAND_BLOCK, 3 * BAND_BLOCK), BF16)],
        compiler_params=_cparams(("parallel", "parallel")),
        name="window_attention",
    )(sink, proj, proj, proj, proj, proj, proj, proj, bias_tab, qw_full, kw_full, qsum, qexp, ksum, kexp)


def _t5_bucket(rel):
    half = NUM_BUCKETS // 2
    max_exact = half // 2
    bucket = jnp.where(rel > 0, half, 0)
    n = jnp.abs(rel)
    nf = jnp.maximum(n, 1).astype(F32)
    large = max_exact + (jnp.log(nf / max_exact) / math.log(MAX_DISTANCE / max_exact)
                         * (half - max_exact)).astype(I32)
    large = jnp.minimum(large, half - 1)
    return bucket + jnp.where(n < max_exact, n, large)


def _bias_table(rel_bias):
    kpos = jnp.arange(3 * BAND_BLOCK)[None, :]
    rel = kpos - BAND_BLOCK - jnp.arange(BAND_BLOCK)[:, None]
    in_window = jnp.abs(rel) <= WINDOW
    bias = rel_bias[_t5_bucket(rel)].astype(F32).transpose(2, 0, 1)
    mid = jnp.where(in_window[None], bias, NEG_INF)
    first = jnp.where(kpos[None] < BAND_BLOCK, NEG_INF, mid)
    last = jnp.where(kpos[None] >= 2 * BAND_BLOCK, NEG_INF, mid)
    return jnp.stack([first, mid, last])


CONV_HALO = 16
CONV_TILE = 512


def _conv_kernel(prev_ref, cur_ref, next_ref, w_ref, b_ref, o_ref, *, nt):
    i = pl.program_id(1)
    tl = cur_ref.shape[1]
    cur = cur_ref[0].astype(F32)
    prev = prev_ref[0].astype(F32) * jnp.where(i > 0, 1.0, 0.0)
    nxt = next_ref[0].astype(F32) * jnp.where(i < nt - 1, 1.0, 0.0)
    ext = jnp.concatenate([prev, cur, nxt], axis=0)
    half = CONV_WIDTH // 2
    acc = jnp.broadcast_to(b_ref[...], cur.shape)
    for t in range(CONV_WIDTH):
        off = CONV_HALO - half + t
        acc = acc + ext[off:off + tl] * w_ref[t:t + 1, :]
    o_ref[0] = (acc * jax.nn.sigmoid(acc)).astype(o_ref.dtype)


def _conv_silu(proj, conv_w, conv_b):
    b, l, _ = proj.shape
    tl = min(l, CONV_TILE)
    nt = l // tl
    cw = CONV_TILE
    nch = CONV_CH // cw
    c0 = COL_XBC // cw
    hb = tl // CONV_HALO
    nh = l // CONV_HALO
    return pl.pallas_call(
        functools.partial(_conv_kernel, nt=nt),
        out_shape=jax.ShapeDtypeStruct((b, l, CONV_CH), BF16),
        grid=(b, nt, nch),
        in_specs=[pl.BlockSpec((1, CONV_HALO, cw), lambda bi, i, c: (bi, jnp.maximum(i * hb - 1, 0), c0 + c)),
                  pl.BlockSpec((1, tl, cw), lambda bi, i, c: (bi, i, c0 + c)),
                  pl.BlockSpec((1, CONV_HALO, cw), lambda bi, i, c: (bi, jnp.minimum((i + 1) * hb, nh - 1), c0 + c)),
                  pl.BlockSpec((CONV_WIDTH, cw), lambda bi, i, c: (0, c)),
                  pl.BlockSpec((1, cw), lambda bi, i, c: (0, c))],
        out_specs=pl.BlockSpec((1, tl, cw), lambda bi, i, c: (bi, i, c)),
        compiler_params=_cparams(("parallel", "parallel", "parallel")),
        name="conv_silu",
    )(proj, proj, proj, conv_w, conv_b.reshape(1, CONV_CH))


def _ssd_kernel(xs_ref, b_ref, c_ref, dt_ref, dtb_ref, alog_ref, e_ref, y_ref, st_ref, *, rev):
    @pl.when(pl.program_id(1) == 0)
    def _():
        st_ref[...] = jnp.zeros_like(st_ref)

    q = SSD_CHUNK
    off = SSD_HEADS if rev else 0
    z = dt_ref[0].astype(F32) + dtb_ref[...]
    dt = jnp.maximum(z, 0.0) + jnp.log1p(jnp.exp(-jnp.abs(z)))
    da = dt * (-jnp.exp(alog_ref[...]))
    ri = lax.broadcasted_iota(I32, (q, q), 0)
    ci = lax.broadcasted_iota(I32, (q, q), 1)
    if rev:
        mask = ci >= ri
        mask_t = ci <= ri
    else:
        mask = ci <= ri
        mask_t = ci >= ri
    tri = jnp.where(mask, 1.0, 0.0).astype(BF16)
    tri_t = jnp.where(mask_t, 1.0, 0.0).astype(BF16)
    cum = _dot2_left(tri, da)
    cum_t = _dot2(da.T, tri_t)
    tot = cum[0:1, :] if rev else cum[q - 1:q, :]
    e = e_ref[...]
    dt_full = _dot2(dt, e)
    cum_full = _dot2(cum, e)
    tot_full = _dot2(tot, e)
    xdt_f = xs_ref[0].astype(F32) * dt_full
    xdt = xdt_f.astype(BF16)
    xw = (xdt_f * jnp.exp(tot_full - cum_full)).astype(BF16)
    expcum = jnp.exp(cum_full)
    chunk_decay = jnp.exp(tot_full)
    gw = SSD_GROUP_WIDTH
    cbs, y_offs = [], []
    for g in range(SSD_GROUPS):
        bg = b_ref[0][:, g * SSD_STATE:(g + 1) * SSD_STATE]
        cg = c_ref[0][:, g * SSD_STATE:(g + 1) * SSD_STATE]
        bgt = bg.astype(F32).T.astype(BF16)
        cbs.append(jnp.dot(cg, bgt, preferred_element_type=F32))
        st = st_ref[g]
        y_offs.append(jnp.dot(cg, st.astype(BF16), preferred_element_type=F32) * expcum[:, g * gw:(g + 1) * gw])
        new = jnp.dot(bgt, xw[:, g * gw:(g + 1) * gw], preferred_element_type=F32)
        st_ref[g] = st * chunk_decay[:, g * gw:(g + 1) * gw] + new
    for g in range(SSD_GROUPS):
        ys = []
        for r in range(SSD_HEADS_PER_GROUP):
            h = g * SSD_HEADS_PER_GROUP + r
            seg = cum[:, off + h:off + h + 1] - cum_t[off + h:off + h + 1, :]
            dec = jnp.exp(jnp.where(mask, seg, NEG_INF))
            m = (cbs[g] * dec).astype(BF16)
            ys.append(jnp.dot(m, xdt[:, h * SSD_HEAD_DIM:(h + 1) * SSD_HEAD_DIM],
                              preferred_element_type=F32))
        y_diag = jnp.concatenate(ys, axis=1)
        y_ref[0, :, g * gw:(g + 1) * gw] = (y_diag + y_offs[g]).astype(y_ref.dtype)


def _ssd_scan(xbc, proj, dtb_row, alog_row, expand, rev):
    b, l, _ = xbc.shape
    nc = l // SSD_CHUNK
    gs = SSD_GROUPS * SSD_STATE
    cidx = (lambda c: nc - 1 - c) if rev else (lambda c: c)
    return pl.pallas_call(
        functools.partial(_ssd_kernel, rev=rev),
        out_shape=jax.ShapeDtypeStruct((b, l, SSD_INNER), BF16),
        grid=(b, nc),
        in_specs=[pl.BlockSpec((1, SSD_CHUNK, SSD_INNER), lambda bi, c: (bi, cidx(c), 0)),
                  pl.BlockSpec((1, SSD_CHUNK, gs), lambda bi, c: (bi, cidx(c), SSD_INNER // gs)),
                  pl.BlockSpec((1, SSD_CHUNK, gs), lambda bi, c: (bi, cidx(c), SSD_INNER // gs + 1)),
                  pl.BlockSpec((1, SSD_CHUNK, LANES), lambda bi, c: (bi, cidx(c), COL_DT // LANES)),
                  pl.BlockSpec((1, LANES), lambda bi, c: (0, 0)),
                  pl.BlockSpec((1, LANES), lambda bi, c: (0, 0)),
                  pl.BlockSpec((LANES, SSD_INNER), lambda bi, c: (0, 0))],
        out_specs=pl.BlockSpec((1, SSD_CHUNK, SSD_INNER), lambda bi, c: (bi, cidx(c), 0)),
        scratch_shapes=[pltpu.VMEM((SSD_GROUPS, SSD_STATE, SSD_GROUP_WIDTH), F32)],
        compiler_params=_cparams(("parallel", "arbitrary")),
        name="ssd_scan_bwd" if rev else "ssd_scan_fwd",
    )(xbc, xbc, xbc, proj, dtb_row, alog_row, expand)


def _merge_kernel(x_ref, yf_ref, yb_ref, xs_ref, z_ref, ao_ref, ga_ref, gs_ref, g1_ref,
                  dsk_ref, snw_ref, wssd_ref, wattn_ref, wout_ref, o_ref):
    y = yf_ref[0].astype(F32) + yb_ref[0].astype(F32) + xs_ref[0].astype(F32) * dsk_ref[...]
    z = z_ref[0].astype(F32)
    y = y * (z * jax.nn.sigmoid(z))
    parts = []
    for g in range(SSD_GROUPS):
        yg = y[:, g * SSD_GROUP_WIDTH:(g + 1) * SSD_GROUP_WIDTH]
        parts.append(yg * lax.rsqrt(jnp.mean(yg * yg, axis=-1, keepdims=True) + EPS))
    y = jnp.concatenate(parts, axis=1) * snw_ref[...]
    ssd = jnp.dot(y.astype(BF16), wssd_ref[...], preferred_element_type=F32)
    attn = jnp.dot(ao_ref[0], wattn_ref[...], preferred_element_type=F32)
    merged = (jax.nn.sigmoid(ga_ref[0].astype(F32)) * attn
              + jax.nn.sigmoid(gs_ref[0].astype(F32)) * ssd)
    out = jnp.dot(merged.astype(BF16), wout_ref[...], preferred_element_type=F32)
    o_ref[0] = x_ref[0] + g1_ref[0] * out


def _merge(x, yf, yb, xbc, proj, attn_o, modr, dsk_full, snw, w_ssd, w_attn, w_out):
    b, l, _ = x.shape
    tl = min(l, 256)
    tok = lambda w, colb: pl.BlockSpec((1, tl, w), lambda bi, i: (bi, i, colb))
    full = lambda r, c: pl.BlockSpec((r, c), lambda bi, i: (0, 0))
    return pl.pallas_call(
        _merge_kernel,
        out_shape=jax.ShapeDtypeStruct((b, l, D_MODEL), F32),
        grid=(b, l // tl),
        in_specs=[tok(D_MODEL, 0), tok(SSD_INNER, 0), tok(SSD_INNER, 0), tok(SSD_INNER, 0),
                  tok(SSD_INNER, COL_Z // SSD_INNER), tok(ATTN_WIDTH, 0),
                  tok(D_MODEL, COL_GA // D_MODEL), tok(D_MODEL, COL_GS // D_MODEL),
                  pl.BlockSpec((1, 1, D_MODEL), lambda bi, i: (bi * N_MOD + 2, 0, 0)),
                  full(1, SSD_INNER), full(1, SSD_INNER),
                  full(SSD_INNER, D_MODEL), full(ATTN_WIDTH, D_MODEL), full(D_MODEL, D_MODEL)],
        out_specs=tok(D_MODEL, 0),
        compiler_params=_cparams(("parallel", "parallel")),
        name="merge_out_proj",
    )(x, yf, yb, xbc, proj, attn_o, proj, proj, modr, dsk_full, snw, w_ssd, w_attn, w_out)


def _topk_rows(s, k):
    rowf = lax.broadcasted_iota(I32, s.shape, 0).astype(F32)
    vals, ids = [], []
    for _ in range(k):
        m = jnp.max(s, axis=0, keepdims=True)
        am = jnp.min(jnp.where(s == m, rowf, float(s.shape[0])), axis=0, keepdims=True)
        vals.append(m)
        ids.append(am)
        s = jnp.where(rowf == am, -jnp.inf, s)
    return vals, ids


def _candidate_pieces():
    pieces = []
    for a in range(PEER_TOPK):
        nb = PEER_TOPK // (a + 1)
        if nb >= SUBLANES // 2:
            for b0 in range(0, nb, SUBLANES):
                pieces.append((a, 1, b0, SUBLANES))
    a_done = max(p[0] for p in pieces) + 1
    for b in range(PEER_TOPK):
        na = PEER_TOPK // (b + 1)
        if na > a_done:
            for a0 in range(0, na, SUBLANES):
                pieces.append((a0, SUBLANES, b, 1))
    return pieces, a_done


def _route_kernel(x_ref, sh_ref, sc_ref, nw_ref, wq_ref, keys_ref, hp_ref, idx_ref, gate_ref, cnt_ref):
    h = _norm_mod(x_ref[0], nw_ref[...], sc_ref[0], sh_ref[0])
    hp_ref[0] = h
    qall = jnp.dot(h.astype(BF16), wq_ref[...], preferred_element_type=F32)
    ntok = qall.shape[0]
    tops, topi = [], []
    for c in range(2 * PEER_HEADS):
        qc = qall[:, c * PEER_HALF:(c + 1) * PEER_HALF].astype(BF16)
        s = lax.dot_general(keys_ref[c], qc, (((1,), (1,)), ((), ())), preferred_element_type=F32)
        vals, ids = _topk_rows(s, PEER_TOPK)
        tops.append(vals)
        topi.append(ids)
    pieces, a_done = _candidate_pieces()
    sub = lax.broadcasted_iota(I32, (SUBLANES, ntok), 0)
    subf = sub.astype(F32)
    pos_parts, drop_parts = [], []
    for a0, na, b0, nb in pieces:
        if na == 1:
            pos_parts.append(subf + float(a0 * PEER_TOPK + b0))
            keep = sub + b0 < PEER_TOPK // (a0 + 1)
        else:
            pos_parts.append(subf * float(PEER_TOPK) + float(a0 * PEER_TOPK + b0))
            keep = jnp.where(sub + a0 >= a_done, sub + a0, PEER_TOPK) < PEER_TOPK // (b0 + 1)
        drop_parts.append(jnp.where(keep, 0.0, -jnp.inf))
    pos = jnp.concatenate(pos_parts, axis=0)
    drop = jnp.concatenate(drop_parts, axis=0)
    idx_rows, gate_rows = [], []
    for hd in range(PEER_HEADS):
        s0 = jnp.concatenate(tops[2 * hd], axis=0)
        i0 = jnp.concatenate(topi[2 * hd], axis=0)
        s1 = jnp.concatenate(tops[2 * hd + 1], axis=0)
        i1 = jnp.concatenate(topi[2 * hd + 1], axis=0)
        cparts, iparts = [], []
        for a0, na, b0, nb in pieces:
            if na == 1:
                cparts.append(tops[2 * hd][a0] + s1[b0:b0 + nb])
                iparts.append(topi[2 * hd][a0] * float(N_KEYS) + i1[b0:b0 + nb])
            else:
                cparts.append(s0[a0:a0 + na] + tops[2 * hd + 1][b0])
                iparts.append(i0[a0:a0 + na] * float(N_KEYS) + topi[2 * hd + 1][b0])
        cand = jnp.concatenate(cparts, axis=0) + drop
        cidx = jnp.concatenate(iparts, axis=0)
        best = []
        for _ in range(PEER_TOPK):
            m = jnp.max(cand, axis=0, keepdims=True)
            first = jnp.min(jnp.where(cand == m, pos, float(PEER_TOPK * PEER_TOPK)), axis=0, keepdims=True)
            sel = pos == first
            idx_rows.append(jnp.max(jnp.where(sel, cidx, -1.0), axis=0, keepdims=True))
            best.append(m)
            cand = jnp.where(sel, -jnp.inf, cand)
        bs = jnp.concatenate(best, axis=0)
        p = jnp.exp(bs - best[0])
        gate_rows.append(p / jnp.sum(p, axis=0, keepdims=True))
    idx_t = jnp.concatenate(idx_rows, axis=0)
    gate_t = jnp.concatenate(gate_rows, axis=0)
    idx_t, gate_t, n_zero = _partition_by_class(idx_t, gate_t)
    idx_ref[0] = (idx_t * float(SLAB_ROWS)).T.astype(I32)
    gate_ref[0] = gate_t.T
    cnt_ref[0, 0] = n_zero.astype(I32)


def _route(x1, modr, norm_w, wq, keys):
    b, l, _ = x1.shape
    tl = min(l, 256)
    tok = lambda w: pl.BlockSpec((1, tl, w), lambda bi, i: (bi, i, 0))
    return pl.pallas_call(
        _route_kernel,
        out_shape=(jax.ShapeDtypeStruct((b, l, D_MODEL), F32),
                   jax.ShapeDtypeStruct((b, l, N_SEL), I32),
                   jax.ShapeDtypeStruct((b, l, N_SEL), F32),
                   jax.ShapeDtypeStruct((b, l // tl, 1, tl), I32)),
        grid=(b, l // tl),
        in_specs=[tok(D_MODEL),
                  pl.BlockSpec((1, 1, D_MODEL), lambda bi, i: (bi * N_MOD + 3, 0, 0)),
                  pl.BlockSpec((1, 1, D_MODEL), lambda bi, i: (bi * N_MOD + 4, 0, 0)),
                  pl.BlockSpec((1, D_MODEL), lambda bi, i: (0, 0)),
                  pl.BlockSpec((D_MODEL, PEER_HEADS * PEER_KEY_DIM), lambda bi, i: (0, 0)),
                  pl.BlockSpec((2 * PEER_HEADS, N_KEYS, PEER_HALF), lambda bi, i: (0, 0, 0))],
        out_specs=(tok(D_MODEL), tok(N_SEL), tok(N_SEL),
                   pl.BlockSpec((1, 1, 1, tl), lambda bi, i: (bi, i, 0, 0))),
        compiler_params=_cparams(("parallel", "parallel")),
        name="peer_route",
    )(x1, modr, modr, norm_w, wq, keys)


def _pack_rows(tab):
    n = tab.shape[0]
    bits = lax.bitcast_convert_type(tab.astype(BF16), jnp.uint16).astype(jnp.uint32)
    lo = bits[:, :D_MODEL // 2]
    hi = bits[:, D_MODEL // 2:]
    words = lax.bitcast_convert_type(lo | (hi << 16), I32)
    return words.reshape(n, ROW_WORDS, LANES)


def _gelu_tanh(x):
    return 0.5 * x * (1.0 + jnp.tanh(math.sqrt(2.0 / math.pi) * (x + 0.044715 * (x * x * x))))


def _unpack_rows(words):
    return pltpu.bitcast(words << 16, F32), pltpu.bitcast(words & jnp.int32(-65536), F32)


SLAB_ROWS = 2 * ROW_WORDS
PEER_CHUNK = 16
PEER_COMMON_CHUNKS = 5


def _expert_class(e):
    return lax.population_count(e) & 1


def _class_tables(tab_u, tab_v):
    slabs = jnp.stack([_pack_rows(tab_u), _pack_rows(tab_v)], axis=2).reshape(N_EXPERTS, SLAB_ROWS, LANES)
    even = 2 * jnp.arange(N_EXPERTS // 2, dtype=I32)
    first = even + _expert_class(even)
    second = even + 1 - _expert_class(even)
    shape = (N_EXPERTS // 2 * SLAB_ROWS, LANES)
    return slabs[first].reshape(shape), slabs[second].reshape(shape)


def _compact_rows(vals, shift, valid, up):
    n = valid.shape[0]
    for bit in range(n.bit_length() - 1):
        step = 1 << bit
        roll = (lambda a: pltpu.roll(a, (n - step) if up else step, axis=0))
        move = valid * ((shift >> bit) & 1)
        bring = roll(move) != 0
        vals = [jnp.where(bring, roll(v), v) for v in vals]
        shift = jnp.where(bring, roll(shift), shift)
        valid = jnp.where(bring, 1, valid - move)
    return vals


def _partition_by_class(idx_t, gate_t):
    n = idx_t.shape[0]
    e = idx_t.astype(I32)
    cls = _expert_class(e)
    local = (e >> 1).astype(F32)
    is_one = cls.astype(F32).astype(BF16)
    is_zero = (1 - cls).astype(F32).astype(BF16)
    ri = lax.broadcasted_iota(I32, (n, n), 0)
    ci = lax.broadcasted_iota(I32, (n, n), 1)
    before = jnp.where(ci < ri, 1.0, 0.0).astype(BF16)
    after = jnp.where(ci > ri, 1.0, 0.0).astype(BF16)
    ones_before = jnp.dot(before, is_one, preferred_element_type=F32).astype(I32)
    zeros_after = jnp.dot(after, is_zero, preferred_element_type=F32).astype(I32)
    n_zero = jnp.sum(is_zero.astype(F32), axis=0, keepdims=True)
    z_idx, z_gate = _compact_rows([local, gate_t], ones_before, 1 - cls, up=True)
    o_idx, o_gate = _compact_rows([local, gate_t], zeros_after, cls, up=False)
    first = lax.broadcasted_iota(I32, idx_t.shape, 0).astype(F32) < n_zero
    return jnp.where(first, z_idx, o_idx), jnp.where(first, z_gate, o_gate), n_zero


def _token_tile(rows, tt):
    return jnp.concatenate([rows[tt:tt + 1, j * LANES:(j + 1) * LANES] for j in range(D_MODEL // LANES)],
                           axis=0)


def _rows_from_tiles(tiles8):
    return jnp.concatenate([jnp.concatenate([t8[j:j + 1] for t8 in tiles8], axis=0)
                            for j in range(D_MODEL // LANES)], axis=1)


def _class_pass_parts(idx_ref, x_ref, tab_ref, rep_ref, ones_ref, *, cls, s_lo, s_hi):
    lane = lax.broadcasted_iota(I32, (1, N_SEL), 1)
    in_range = jnp.logical_and(lane >= s_lo * PEER_CHUNK, lane < s_hi * PEER_CHUNK)
    crows = PEER_CHUNK * SLAB_ROWS
    nsel = (s_hi - s_lo) * PEER_CHUNK
    nrows = nsel * ROW_WORDS
    half = SUBLANES // 2

    def gather_chunk(tile, base, c):
        k0 = base + c * PEER_CHUNK
        r0 = (c - s_lo) * crows
        r0 = r0 if isinstance(c, int) else pl.multiple_of(r0, crows)
        for kk in range(PEER_CHUNK):
            r = pl.multiple_of(idx_ref[k0 + kk], SLAB_ROWS)
            tile[pl.ds(r0 + kk * SLAB_ROWS, SLAB_ROWS), :] = tab_ref[pl.ds(r, SLAB_ROWS), :]

    def valid_row(n_zero):
        mine = (lane >= n_zero) if cls else (lane < n_zero)
        return jnp.where(jnp.logical_and(mine, in_range), 1.0, 0.0)

    def needed_chunks(n_zero):
        if cls:
            return jnp.maximum(n_zero // PEER_CHUNK, s_lo), s_hi
        return s_lo, jnp.minimum((n_zero + PEER_CHUNK - 1) // PEER_CHUNK, s_hi)

    def activation(tile, x8):
        acc = None
        for s in range(ROW_WORDS):
            u_lo, u_hi = _unpack_rows(tile[pl.ds(2 * s, nsel, stride=SLAB_ROWS), :])
            term = u_lo * x8[s:s + 1] + u_hi * x8[half + s:half + s + 1]
            acc = term if acc is None else acc + term
        before, after = s_lo * PEER_CHUNK, N_SEL - s_hi * PEER_CHUNK
        full = jnp.concatenate(([jnp.zeros((before, LANES), F32)] if before else []) + [acc]
                               + ([jnp.zeros((after, LANES), F32)] if after else []), axis=0)
        return jnp.sum(full.T, axis=0, keepdims=True)

    def weighted_sum(tile, w_row):
        v_lo, v_hi = _unpack_rows(tile[pl.ds(1, nrows, stride=2), :])
        wrows = jnp.dot((rep_ref[...] * w_row).astype(BF16), ones_ref[...], preferred_element_type=F32)
        acc_lo = jnp.sum((v_lo * wrows).reshape(nrows // SUBLANES, SUBLANES, LANES), axis=0)
        acc_hi = jnp.sum((v_hi * wrows).reshape(nrows // SUBLANES, SUBLANES, LANES), axis=0)
        return jnp.concatenate([acc_lo[:half] + acc_lo[half:], acc_hi[:half] + acc_hi[half:]], axis=0)

    return gather_chunk, valid_row, needed_chunks, activation, weighted_sum


def _class_body(idx_ref, cnt_ref, x_ref, gate_ref, last, tab_ref, rep_ref, rep_rare_ref, ones_ref,
                out_ref, tiles, wbufs, tile_rare, *, cls):
    nchunk = N_SEL // PEER_CHUNK
    if cls:
        s_lo, s_hi, r_lo, r_hi = nchunk - PEER_COMMON_CHUNKS, nchunk, 0, nchunk - PEER_COMMON_CHUNKS
    else:
        s_lo, s_hi, r_lo, r_hi = 0, PEER_COMMON_CHUNKS, PEER_COMMON_CHUNKS, nchunk
    gather_chunk, valid_row, _, activation, weighted_sum = _class_pass_parts(
        idx_ref, x_ref, tab_ref, rep_ref, ones_ref, cls=cls, s_lo=s_lo, s_hi=s_hi)
    if last:
        prev_ref, x1_ref, g2_ref = last

    def group_rows(t0):
        return pl.ds(t0, PEER_TOK_GROUP)

    def emit(t0, tiles8):
        rows = _rows_from_tiles(tiles8)
        if last:
            rows = x1_ref[group_rows(t0), :] + g2_ref[0] * (prev_ref[group_rows(t0), :] + rows)
        out_ref[group_rows(t0), :] = rows

    def emit_more(t0, tt, o8):
        zero = jnp.zeros_like(o8)
        rows = _rows_from_tiles([o8 if i == tt else zero for i in range(PEER_TOK_GROUP)])
        out_ref[group_rows(t0), :] = out_ref[group_rows(t0), :] + (g2_ref[0] * rows if last else rows)

    @pl.when(pl.program_id(0) == 0)
    def _():
        for buf in tiles + [tile_rare]:
            buf[...] = jnp.zeros_like(buf)

    def first_stage(g, par):
        t0 = pl.multiple_of(g * PEER_TOK_GROUP, PEER_TOK_GROUP)
        x_rows = x_ref[group_rows(t0), :]
        acts, valid = [], []

        def token(tt):
            valid.append(valid_row(cnt_ref[t0 + tt]))
            for c in range(s_lo, s_hi):
                gather_chunk(tiles[par].at[tt], (t0 + tt) * N_SEL, c)
            acts.append(activation(tiles[par].at[tt], _token_tile(x_rows, tt)))

        def finish():
            act = jnp.concatenate(acts, axis=0)
            wbufs[par][...] = (gate_ref[group_rows(t0), :] * _gelu_tanh(act) * jnp.concatenate(valid, axis=0))

        return token, finish

    def second_stage(g, par):
        t0 = pl.multiple_of(g * PEER_TOK_GROUP, PEER_TOK_GROUP)
        w = wbufs[par][...]
        sums = []

        def token(tt):
            sums.append(weighted_sum(tiles[par].at[tt], w[tt:tt + 1, :]))

        return token, lambda: emit(t0, sums)

    def step(first, second):
        tok1, fin1 = first_stage(*first) if first else (None, None)
        tok2, fin2 = second_stage(*second) if second else (None, None)
        for tt in range(PEER_TOK_GROUP):
            if tok1:
                tok1(tt)
            if tok2:
                tok2(tt)
        if fin2:
            fin2()
        if fin1:
            fin1()

    ngroups = PEER_BLOCK // PEER_TOK_GROUP
    unroll = 2
    assert (ngroups - unroll) % unroll == 0

    def steps(first_group, parity0):
        for k in range(unroll):
            par = (parity0 + k) % 2
            step((first_group + k, par), (first_group + k - 1, 1 - par))

    step((0, 0), None)

    def body(j, carry):
        steps(1 + unroll * j, 1)
        return carry

    lax.fori_loop(0, (ngroups - unroll) // unroll, body, 0)
    for g in range(ngroups - unroll + 1, ngroups):
        step((g, g % 2), (g - 1, (g - 1) % 2))
    step(None, (ngroups - 1, (ngroups - 1) % 2))

    r_gather, r_valid, r_chunks, r_activation, r_weighted_sum = _class_pass_parts(
        idx_ref, x_ref, tab_ref, rep_rare_ref, ones_ref, cls=cls, s_lo=r_lo, s_hi=r_hi)

    def needs_rare(n_zero):
        return (n_zero < r_hi * PEER_CHUNK) if cls else (n_zero > r_lo * PEER_CHUNK)

    def rare_token(t0, tt):
        n_zero = cnt_ref[t0 + tt]

        @pl.when(needs_rare(n_zero))
        def _():
            def chunk(c, carry):
                r_gather(tile_rare, (t0 + tt) * N_SEL, c)
                return carry
            lax.fori_loop(*r_chunks(n_zero), chunk, 0)
            act = r_activation(tile_rare, _token_tile(x_ref[group_rows(t0), :], tt))
            w = gate_ref[group_rows(t0), :][tt:tt + 1] * _gelu_tanh(act) * r_valid(n_zero)
            emit_more(t0, tt, r_weighted_sum(tile_rare, w))

    def rare_group(g, carry):
        t0 = pl.multiple_of(g * PEER_TOK_GROUP, PEER_TOK_GROUP)
        counts = [cnt_ref[t0 + tt] for tt in range(PEER_TOK_GROUP)]
        extreme = functools.reduce(jnp.minimum if cls else jnp.maximum, counts)

        @pl.when(needs_rare(extreme))
        def _():
            for tt in range(PEER_TOK_GROUP):
                rare_token(t0, tt)
        return carry

    lax.fori_loop(0, ngroups, rare_group, 0)


def _class_first_kernel(idx_ref, cnt_ref, x_ref, gate_ref, tab_ref, rep_ref, rep_rare_ref, ones_ref,
                        out_ref, tile0, tile1, wbuf0, wbuf1, tile_rare, *, cls):
    _class_body(idx_ref, cnt_ref, x_ref, gate_ref, None, tab_ref, rep_ref, rep_rare_ref, ones_ref,
                out_ref, [tile0, tile1], [wbuf0, wbuf1], tile_rare, cls=cls)


def _class_last_kernel(idx_ref, cnt_ref, x_ref, gate_ref, prev_ref, x1_ref, g2_ref, tab_ref, rep_ref,
                       rep_rare_ref, ones_ref, out_ref, tile0, tile1, wbuf0, wbuf1, tile_rare, *, cls):
    _class_body(idx_ref, cnt_ref, x_ref, gate_ref, (prev_ref, x1_ref, g2_ref), tab_ref, rep_ref,
                rep_rare_ref, ones_ref, out_ref, [tile0, tile1], [wbuf0, wbuf1], tile_rare, cls=cls)


def _class_pass(idx_flat, cnt_flat, h, gates, last, tab, *, cls, seq_len):
    t = gates.shape[0]
    ncommon = PEER_COMMON_CHUNKS * PEER_CHUNK
    nrare = N_SEL - ncommon

    def rep(first_sel, nsel):
        rows = jnp.arange(nsel * ROW_WORDS)[:, None] // ROW_WORDS + first_sel
        return (rows == jnp.arange(N_SEL)[None, :]).astype(F32)

    rep_common = rep(nrare if cls else 0, ncommon)
    rep_rare = rep(0 if cls else ncommon, nrare)
    ones = jnp.ones((N_SEL, LANES), BF16)
    const = lambda a: pl.BlockSpec(a.shape, lambda i: (0, 0))
    tok_spec = pl.BlockSpec((PEER_BLOCK, D_MODEL), lambda i: (i, 0))
    in_specs = [pl.BlockSpec((PEER_BLOCK * N_SEL,), lambda i: (i,), memory_space=pltpu.SMEM),
                pl.BlockSpec((PEER_BLOCK,), lambda i: (i,), memory_space=pltpu.SMEM),
                tok_spec, pl.BlockSpec((PEER_BLOCK, N_SEL), lambda i: (i, 0))]
    args = [idx_flat, cnt_flat, h, gates]
    if last is not None:
        gate_row = lambda i: ((i * PEER_BLOCK // seq_len) * N_MOD + N_MOD - 1, 0, 0)
        in_specs += [tok_spec, tok_spec, pl.BlockSpec((1, 1, D_MODEL), gate_row)]
        args += list(last)
    in_specs += [pl.BlockSpec(memory_space=pltpu.VMEM), const(rep_common), const(rep_rare), const(ones)]
    args += [tab, rep_common, rep_rare, ones]
    tile = pltpu.VMEM((PEER_TOK_GROUP, ncommon * SLAB_ROWS, LANES), I32)
    wbuf = pltpu.VMEM((PEER_TOK_GROUP, N_SEL), F32)
    return pl.pallas_call(
        functools.partial(_class_first_kernel if last is None else _class_last_kernel, cls=cls),
        out_shape=jax.ShapeDtypeStruct((t, D_MODEL), F32),
        grid=(t // PEER_BLOCK,),
        in_specs=in_specs,
        out_specs=tok_spec,
        scratch_shapes=[tile, tile, wbuf, wbuf, pltpu.VMEM((nrare * SLAB_ROWS, LANES), I32)],
        compiler_params=_cparams(("arbitrary",)),
        name="peer_class%d" % cls,
    )(*args)


def _prepare(rel_bias, ada_w, ada_b, norm1_w, norm2_w, w_in, q_norm_w, k_norm_w, attn_sink, conv_w,
             conv_b, a_log, dt_bias, d_skip, ssd_norm_w, w_attn_br, w_ssd_br, w_out, peer_wq,
             peer_keys, peer_u, peer_v):
    lyr = 0
    w = w_in[lyr]
    o = 0
    parts = {}
    for name, width in (("q", ATTN_WIDTH), ("k", KV_WIDTH), ("v", KV_WIDTH), ("z", SSD_INNER),
                        ("xbc", CONV_CH), ("dt", 2 * SSD_HEADS), ("ga", D_MODEL), ("gs", D_MODEL)):
        parts[name] = w[:, o:o + width]
        o += width
    pad = jnp.zeros((D_MODEL, PROJ_W - (COL_DT + 2 * SSD_HEADS)), w.dtype)
    w_all = jnp.concatenate([parts["z"], parts["q"], parts["ga"], parts["gs"], parts["xbc"],
                             parts["k"], parts["v"], parts["dt"], pad], axis=1).astype(BF16)
    lane_pad = LANES - 2 * SSD_HEADS
    expand = (jnp.arange(SSD_INNER)[None, :] // SSD_HEAD_DIM == jnp.arange(LANES)[:, None])
    tab0, tab1 = _class_tables(peer_u[lyr], peer_v[lyr])
    return dict(
        ada_w=ada_w[lyr], ada_b=ada_b[lyr],
        norm1_w=norm1_w[lyr].reshape(1, D_MODEL), norm2_w=norm2_w[lyr].reshape(1, D_MODEL),
        w_all=w_all,
        bias_tab=_bias_table(rel_bias), sink=attn_sink[lyr].astype(F32),
        qw=(jnp.tile(q_norm_w[lyr], N_Q_HEADS) * HEAD_DIM ** -0.5).reshape(1, ATTN_WIDTH),
        kw=jnp.tile(k_norm_w[lyr], N_KV_HEADS).reshape(1, KV_WIDTH),
        conv_w=conv_w[lyr], conv_b=conv_b[lyr],
        dtb_row=jnp.pad(dt_bias[lyr].astype(F32).reshape(1, -1), ((0, 0), (0, lane_pad))),
        alog_row=jnp.pad(a_log[lyr].astype(F32).reshape(1, -1), ((0, 0), (0, lane_pad))),
        expand_fwd=expand.astype(BF16),
        expand_bwd=jnp.roll(expand, SSD_HEADS, axis=0).astype(BF16),
        dsk_full=jnp.repeat(d_skip[lyr], SSD_HEAD_DIM).reshape(1, SSD_INNER),
        snw=ssd_norm_w[lyr].reshape(1, SSD_INNER),
        w_ssd=w_ssd_br[lyr].astype(BF16), w_attn=w_attn_br[lyr].astype(BF16), w_out=w_out[lyr].astype(BF16),
        wq=peer_wq[lyr].astype(BF16),
        keys=peer_keys[lyr].reshape(2 * PEER_HEADS, N_KEYS, PEER_HALF).astype(BF16),
        tab0=tab0, tab1=tab1,
    )


def _token_mixer_stage(x, modr, p):
    proj = _in_projection(x, modr, p["norm1_w"], p["w_all"])
    attn_o = _attention(proj, p["bias_tab"], p["sink"], p["qw"], p["kw"])
    xbc = _conv_silu(proj, p["conv_w"], p["conv_b"])
    yf = _ssd_scan(xbc, proj, p["dtb_row"], p["alog_row"], p["expand_fwd"], rev=False)
    yb = _ssd_scan(xbc, proj, p["dtb_row"], p["alog_row"], p["expand_bwd"], rev=True)
    return _merge(x, yf, yb, xbc, proj, attn_o, modr, p["dsk_full"], p["snw"],
                  p["w_ssd"], p["w_attn"], p["w_out"])


def _peer_stage(x1, modr, p):
    b, l, _ = x1.shape
    t = b * l
    assert l % PEER_BLOCK == 0
    h, idx, gates, cnt = _route(x1, modr, p["norm2_w"], p["wq"], p["keys"])
    args = (idx.reshape(t * N_SEL), cnt.reshape(t), h.reshape(t, D_MODEL), gates.reshape(t, N_SEL))
    out = _class_pass(*args, None, p["tab0"], cls=0, seq_len=l)
    y = _class_pass(*args, (out, x1.reshape(t, D_MODEL), modr), p["tab1"], cls=1, seq_len=l)
    return y.reshape(b, l, D_MODEL)


def _encoder(x, c, p):
    nb = c.shape[0]
    modr = _modulation(c, p["ada_w"], p["ada_b"]).reshape(nb * N_MOD, 1, D_MODEL)
    x1 = _token_mixer_stage(x, modr, p)
    return _peer_stage(x1, modr, p)


def kernel(x_prompt, x_sample, c_prompt, c_sample, rel_bias, ada_w, ada_b, norm1_w, norm2_w, w_in,
           q_norm_w, k_norm_w, attn_sink, conv_w, conv_b, a_log, dt_bias, d_skip, ssd_norm_w,
           w_attn_br, w_ssd_br, w_out, peer_wq, peer_keys, peer_u, peer_v):
    p = _prepare(rel_bias, ada_w, ada_b, norm1_w, norm2_w, w_in, q_norm_w, k_norm_w, attn_sink,
                 conv_w, conv_b, a_log, dt_bias, d_skip, ssd_norm_w, w_attn_br, w_ssd_br, w_out,
                 peer_wq, peer_keys, peer_u, peer_v)
    return (_encoder(x_prompt, c_prompt, p), _encoder(x_sample, c_sample, p))
```

```python
import functools
import math

import jax
import jax.numpy as jnp
from jax import lax
from jax.experimental import pallas as pl
from jax.experimental.pallas import tpu as pltpu

F32 = jnp.float32
BF16 = jnp.bfloat16
I32 = jnp.int32

D_MODEL = 1024
HEAD_DIM = 64
N_Q_HEADS = 16
N_KV_HEADS = 4
Q_PER_KV = N_Q_HEADS // N_KV_HEADS
ATTN_WIDTH = N_Q_HEADS * HEAD_DIM
KV_WIDTH = N_KV_HEADS * HEAD_DIM
WINDOW = 128
BAND_BLOCK = 128
NUM_BUCKETS = 32
MAX_DISTANCE = 128
NEG_INF = -1e30

SSD_INNER = 2 * D_MODEL
SSD_HEAD_DIM = 64
SSD_HEADS = SSD_INNER // SSD_HEAD_DIM
SSD_GROUPS = 4
SSD_HEADS_PER_GROUP = SSD_HEADS // SSD_GROUPS
SSD_STATE = 128
SSD_CHUNK = 128
SSD_GROUP_WIDTH = SSD_INNER // SSD_GROUPS
CONV_WIDTH = 5
CONV_CH = SSD_INNER + 2 * SSD_GROUPS * SSD_STATE

PEER_HEADS = 8
PEER_KEY_DIM = 256
PEER_HALF = PEER_KEY_DIM // 2
N_KEYS = 128
N_EXPERTS = N_KEYS * N_KEYS
PEER_TOPK = 16
N_SEL = PEER_HEADS * PEER_TOPK

N_MOD = 6
EPS = 1e-6

COL_Z = 0
COL_Q = 2048
COL_GA = 3072
COL_GS = 4096
COL_XBC = 5120
COL_K = 8192
COL_V = 8448
COL_DT = 8704
PROJ_W = 9216

LANES = 128
SUBLANES = 8
VMEM_LIMIT = 56 * 1024 * 1024
ROW_WORDS = D_MODEL // (2 * LANES)
PEER_TOK_GROUP = 8
PEER_BLOCK = 128


def _cparams(sem):
    return pltpu.CompilerParams(dimension_semantics=sem, vmem_limit_bytes=VMEM_LIMIT)


def _split_bf16(v):
    hi = v.astype(BF16)
    lo = (v - hi.astype(F32)).astype(BF16)
    return hi, lo


def _dot2(v, m_bf16):
    hi, lo = _split_bf16(v)
    return (jnp.dot(hi, m_bf16, preferred_element_type=F32)
            + jnp.dot(lo, m_bf16, preferred_element_type=F32))


def _dot2_left(m_bf16, v):
    hi, lo = _split_bf16(v)
    return (jnp.dot(m_bf16, hi, preferred_element_type=F32)
            + jnp.dot(m_bf16, lo, preferred_element_type=F32))


def _mod_kernel(c_ref, w_ref, b_ref, o_ref):
    c = c_ref[...]
    sc = c * jax.nn.sigmoid(c)
    o_ref[...] = jnp.dot(sc, w_ref[...], preferred_element_type=F32,
                         precision=lax.Precision.HIGHEST) + b_ref[...]


def _modulation(c, ada_w, ada_b):
    nb = c.shape[0]
    n = ada_w.shape[1]
    tn = 1024
    return pl.pallas_call(
        _mod_kernel,
        out_shape=jax.ShapeDtypeStruct((nb, n), F32),
        grid=(n // tn,),
        in_specs=[pl.BlockSpec((nb, D_MODEL), lambda j: (0, 0)),
                  pl.BlockSpec((D_MODEL, tn), lambda j: (0, j)),
                  pl.BlockSpec((1, tn), lambda j: (0, j))],
        out_specs=pl.BlockSpec((nb, tn), lambda j: (0, j)),
        compiler_params=_cparams(("arbitrary",)),
        name="adaln_mod",
    )(c, ada_w, ada_b.reshape(1, n))


def _norm_mod(x, nw, sc, sh):
    ms = jnp.mean(x * x, axis=-1, keepdims=True)
    h = x * lax.rsqrt(ms + EPS) * nw
    return h * (1.0 + sc) + sh


def _inproj_kernel(x_ref, sh_ref, sc_ref, nw_ref, w_ref, o_ref, h_scr):
    @pl.when(pl.program_id(2) == 0)
    def _():
        h = _norm_mod(x_ref[0], nw_ref[...], sc_ref[0], sh_ref[0])
        h_scr[...] = h.astype(BF16)

    o_ref[0] = jnp.dot(h_scr[...], w_ref[...], preferred_element_type=F32).astype(o_ref.dtype)


def _in_projection(x, modr, norm_w, w_all):
    b, l, _ = x.shape
    tl = min(l, 1024)
    tn = 1024
    return pl.pallas_call(
        _inproj_kernel,
        out_shape=jax.ShapeDtypeStruct((b, l, PROJ_W), BF16),
        grid=(b, l // tl, PROJ_W // tn),
        in_specs=[pl.BlockSpec((1, tl, D_MODEL), lambda bi, i, j: (bi, i, 0)),
                  pl.BlockSpec((1, 1, D_MODEL), lambda bi, i, j: (bi * N_MOD + 0, 0, 0)),
                  pl.BlockSpec((1, 1, D_MODEL), lambda bi, i, j: (bi * N_MOD + 1, 0, 0)),
                  pl.BlockSpec((1, D_MODEL), lambda bi, i, j: (0, 0)),
                  pl.BlockSpec((D_MODEL, tn), lambda bi, i, j: (0, j))],
        out_specs=pl.BlockSpec((1, tl, tn), lambda bi, i, j: (bi, i, j)),
        scratch_shapes=[pltpu.VMEM((tl, D_MODEL), BF16)],
        compiler_params=_cparams(("parallel", "parallel", "arbitrary")),
        name="in_projection",
    )(x, modr, modr, norm_w, w_all)


def _head_sums(n_heads):
    c = jnp.arange(n_heads * HEAD_DIM)
    hsum = (c[:, None] // HEAD_DIM == jnp.arange(LANES)[None, :]).astype(BF16)
    return hsum, hsum.T


def _qk_norm(t, hsum, hexp, w_full):
    ssq = _dot2(t * t, hsum)
    inv = lax.rsqrt(ssq * (1.0 / HEAD_DIM) + EPS)
    return t * _dot2(inv, hexp) * w_full


def _attn_kernel(sink_ref, q_ref, kp_ref, kc_ref, kn_ref, vp_ref, vc_ref, vn_ref, bias_ref,
                 qw_ref, kw_ref, qsum_ref, qexp_ref, ksum_ref, kexp_ref, o_ref, s_scr, p_scr):
    q = _qk_norm(q_ref[0].astype(F32), qsum_ref[...], qexp_ref[...], qw_ref[...]).astype(BF16)
    k = jnp.concatenate([kp_ref[0], kc_ref[0], kn_ref[0]], axis=0).astype(F32)
    k = _qk_norm(k, ksum_ref[...], kexp_ref[...], kw_ref[...])
    v = jnp.concatenate([vp_ref[0], vc_ref[0], vn_ref[0]], axis=0).astype(F32)
    low = lax.broadcasted_iota(I32, (1, LANES), 1) < HEAD_DIM
    nt = (((1,), (1,)), ((), ()))
    k_sel, v_sel = [], []
    for c in range(N_KV_HEADS // 2):
        kslab = k[:, c * LANES:(c + 1) * LANES]
        vslab = v[:, c * LANES:(c + 1) * LANES]
        kroll = pltpu.roll(kslab, HEAD_DIM, axis=1)
        vroll = pltpu.roll(vslab, HEAD_DIM, axis=1)
        for e in range(2):
            k_sel.append((jnp.where(low, kroll if e else kslab, 0.0).astype(BF16),
                          jnp.where(low, 0.0, kslab if e else kroll).astype(BF16)))
            v_sel.append((jnp.where(low, vroll if e else vslab, 0.0).astype(BF16),
                          jnp.where(low, 0.0, vslab if e else vroll).astype(BF16)))
    for hd in range(N_Q_HEADS):
        qs = q[:, (hd // 2) * LANES:(hd // 2 + 1) * LANES]
        s = lax.dot_general(qs, k_sel[hd // Q_PER_KV][hd % 2], nt, preferred_element_type=F32)
        s_scr[hd] = s + bias_ref[0, hd]
    for hd in range(N_Q_HEADS):
        s = s_scr[hd]
        sk = sink_ref[hd]
        m = jnp.maximum(jnp.max(s, axis=-1, keepdims=True), sk)
        p = jnp.exp(s - m)
        denom = jnp.sum(p, axis=-1, keepdims=True) + jnp.exp(sk - m)
        p_scr[hd] = (p * (1.0 / denom)).astype(BF16)
    for slab in range(N_Q_HEADS // 2):
        vs = v_sel[(2 * slab) // Q_PER_KV]
        acc = (jnp.dot(p_scr[2 * slab], vs[0], preferred_element_type=F32)
               + jnp.dot(p_scr[2 * slab + 1], vs[1], preferred_element_type=F32))
        o_ref[0, :, slab * LANES:(slab + 1) * LANES] = acc.astype(o_ref.dtype)


def _attention(proj, bias_tab, sink, qw_full, kw_full):
    b, l, _ = proj.shape
    nb = l // BAND_BLOCK
    assert nb >= 2
    kcol = COL_K // KV_WIDTH
    vcol = COL_V // KV_WIDTH
    qsum, qexp = _head_sums(N_Q_HEADS)
    ksum, kexp = _head_sums(N_KV_HEADS)

    def prev(i):
        return jnp.maximum(i - 1, 0)

    def nxt(i):
        return jnp.minimum(i + 1, nb - 1)

    def edge(i):
        return jnp.where(i == 0, 0, jnp.where(i == nb - 1, 2, 1))

    kv = lambda colb, f: pl.BlockSpec((1, BAND_BLOCK, KV_WIDTH), lambda bi, i: (bi, f(i), colb))
    same = lambda i: i
    const = lambda a: pl.BlockSpec(a.shape, lambda bi, i: (0,) * a.ndim)
    return pl.pallas_call(
        _attn_kernel,
        out_shape=jax.ShapeDtypeStruct((b, l, ATTN_WIDTH), BF16),
        grid=(b, nb),
        in_specs=[pl.BlockSpec(memory_space=pltpu.SMEM),
                  pl.BlockSpec((1, BAND_BLOCK, ATTN_WIDTH), lambda bi, i: (bi, i, COL_Q // ATTN_WIDTH)),
                  kv(kcol, prev), kv(kcol, same), kv(kcol, nxt),
                  kv(vcol, prev), kv(vcol, same), kv(vcol, nxt),
                  pl.BlockSpec((1, N_Q_HEADS, BAND_BLOCK, 3 * BAND_BLOCK), lambda bi, i: (edge(i), 0, 0, 0)),
                  const(qw_full), const(kw_full), const(qsum), const(qexp), const(ksum), const(kexp)],
        out_specs=pl.BlockSpec((1, BAND_BLOCK, ATTN_WIDTH), lambda bi, i: (bi, i, 0)),
        scratch_shapes=[pltpu.VMEM((N_Q_HEADS, BAND_BLOCK, 3 * BAND_BLOCK), F32),
                        pltpu.VMEM((N_Q_HEADS, BAND_BLOCK, 3 * BAND_BLOCK), BF16)],
        compiler_params=_cparams(("parallel", "parallel")),
        name="window_attention",
    )(sink, proj, proj, proj, proj, proj, proj, proj, bias_tab, qw_full, kw_full, qsum, qexp, ksum, kexp)


def _t5_bucket(rel):
    half = NUM_BUCKETS // 2
    max_exact = half // 2
    bucket = jnp.where(rel > 0, half, 0)
    n = jnp.abs(rel)
    nf = jnp.maximum(n, 1).astype(F32)
    large = max_exact + (jnp.log(nf / max_exact) / math.log(MAX_DISTANCE / max_exact)
                         * (half - max_exact)).astype(I32)
    large = jnp.minimum(large, half - 1)
    return bucket + jnp.where(n < max_exact, n, large)


def _bias_table(rel_bias):
    kpos = jnp.arange(3 * BAND_BLOCK)[None, :]
    rel = kpos - BAND_BLOCK - jnp.arange(BAND_BLOCK)[:, None]
    in_window = jnp.abs(rel) <= WINDOW
    bias = rel_bias[_t5_bucket(rel)].astype(F32).transpose(2, 0, 1)
    mid = jnp.where(in_window[None], bias, NEG_INF)
    first = jnp.where(kpos[None] < BAND_BLOCK, NEG_INF, mid)
    last = jnp.where(kpos[None] >= 2 * BAND_BLOCK, NEG_INF, mid)
    return jnp.stack([first, mid, last])


CONV_HALO = 16
CONV_TILE = 512


def _conv_kernel(prev_ref, cur_ref, next_ref, w_ref, b_ref, o_ref, *, nt):
    i = pl.program_id(1)
    tl = cur_ref.shape[1]
    cur = cur_ref[0].astype(F32)
    prev = prev_ref[0].astype(F32) * jnp.where(i > 0, 1.0, 0.0)
    nxt = next_ref[0].astype(F32) * jnp.where(i < nt - 1, 1.0, 0.0)
    ext = jnp.concatenate([prev, cur, nxt], axis=0)
    half = CONV_WIDTH // 2
    acc = jnp.broadcast_to(b_ref[...], cur.shape)
    for t in range(CONV_WIDTH):
        off = CONV_HALO - half + t
        acc = acc + ext[off:off + tl] * w_ref[t:t + 1, :]
    o_ref[0] = (acc * jax.nn.sigmoid(acc)).astype(o_ref.dtype)


def _conv_silu(proj, conv_w, conv_b):
    b, l, _ = proj.shape
    tl = min(l, CONV_TILE)
    nt = l // tl
    cw = CONV_TILE
    nch = CONV_CH // cw
    c0 = COL_XBC // cw
    hb = tl // CONV_HALO
    nh = l // CONV_HALO
    return pl.pallas_call(
        functools.partial(_conv_kernel, nt=nt),
        out_shape=jax.ShapeDtypeStruct((b, l, CONV_CH), BF16),
        grid=(b, nt, nch),
        in_specs=[pl.BlockSpec((1, CONV_HALO, cw), lambda bi, i, c: (bi, jnp.maximum(i * hb - 1, 0), c0 + c)),
                  pl.BlockSpec((1, tl, cw), lambda bi, i, c: (bi, i, c0 + c)),
                  pl.BlockSpec((1, CONV_HALO, cw), lambda bi, i, c: (bi, jnp.minimum((i + 1) * hb, nh - 1), c0 + c)),
                  pl.BlockSpec((CONV_WIDTH, cw), lambda bi, i, c: (0, c)),
                  pl.BlockSpec((1, cw), lambda bi, i, c: (0, c))],
        out_specs=pl.BlockSpec((1, tl, cw), lambda bi, i, c: (bi, i, c)),
        compiler_params=_cparams(("parallel", "parallel", "parallel")),
        name="conv_silu",
    )(proj, proj, proj, conv_w, conv_b.reshape(1, CONV_CH))


def _ssd_kernel(xs_ref, b_ref, c_ref, dt_ref, dtb_ref, alog_ref, e_ref, y_ref, st_ref, *, rev):
    @pl.when(pl.program_id(1) == 0)
    def _():
        st_ref[...] = jnp.zeros_like(st_ref)

    q = SSD_CHUNK
    off = SSD_HEADS if rev else 0
    z = dt_ref[0].astype(F32) + dtb_ref[...]
    dt = jnp.maximum(z, 0.0) + jnp.log1p(jnp.exp(-jnp.abs(z)))
    da = dt * (-jnp.exp(alog_ref[...]))
    ri = lax.broadcasted_iota(I32, (q, q), 0)
    ci = lax.broadcasted_iota(I32, (q, q), 1)
    if rev:
        mask = ci >= ri
        mask_t = ci <= ri
    else:
        mask = ci <= ri
        mask_t = ci >= ri
    tri = jnp.where(mask, 1.0, 0.0).astype(BF16)
    tri_t = jnp.where(mask_t, 1.0, 0.0).astype(BF16)
    cum = _dot2_left(tri, da)
    cum_t = _dot2(da.T, tri_t)
    tot = cum[0:1, :] if rev else cum[q - 1:q, :]
    e = e_ref[...]
    dt_full = _dot2(dt, e)
    cum_full = _dot2(cum, e)
    tot_full = _dot2(tot, e)
    xdt_f = xs_ref[0].astype(F32) * dt_full
    xdt = xdt_f.astype(BF16)
    xw = (xdt_f * jnp.exp(tot_full - cum_full)).astype(BF16)
    expcum = jnp.exp(cum_full)
    chunk_decay = jnp.exp(tot_full)
    gw = SSD_GROUP_WIDTH
    cbs, y_offs = [], []
    for g in range(SSD_GROUPS):
        bg = b_ref[0][:, g * SSD_STATE:(g + 1) * SSD_STATE]
        cg = c_ref[0][:, g * SSD_STATE:(g + 1) * SSD_STATE]
        bgt = bg.astype(F32).T.astype(BF16)
        cbs.append(jnp.dot(cg, bgt, preferred_element_type=F32))
        st = st_ref[g]
        y_offs.append(jnp.dot(cg, st.astype(BF16), preferred_element_type=F32) * expcum[:, g * gw:(g + 1) * gw])
        new = jnp.dot(bgt, xw[:, g * gw:(g + 1) * gw], preferred_element_type=F32)
        st_ref[g] = st * chunk_decay[:, g * gw:(g + 1) * gw] + new
    for g in range(SSD_GROUPS):
        ys = []
        for r in range(SSD_HEADS_PER_GROUP):
            h = g * SSD_HEADS_PER_GROUP + r
            seg = cum[:, off + h:off + h + 1] - cum_t[off + h:off + h + 1, :]
            dec = jnp.exp(jnp.where(mask, seg, NEG_INF))
            m = (cbs[g] * dec).astype(BF16)
            ys.append(jnp.dot(m, xdt[:, h * SSD_HEAD_DIM:(h + 1) * SSD_HEAD_DIM],
                              preferred_element_type=F32))
        y_diag = jnp.concatenate(ys, axis=1)
        y_ref[0, :, g * gw:(g + 1) * gw] = (y_diag + y_offs[g]).astype(y_ref.dtype)


def _ssd_scan(xbc, proj, dtb_row, alog_row, expand, rev):
    b, l, _ = xbc.shape
    nc = l // SSD_CHUNK
    gs = SSD_GROUPS * SSD_STATE
    cidx = (lambda c: nc - 1 - c) if rev else (lambda c: c)
    return pl.pallas_call(
        functools.partial(_ssd_kernel, rev=rev),
        out_shape=jax.ShapeDtypeStruct((b, l, SSD_INNER), BF16),
        grid=(b, nc),
        in_specs=[pl.BlockSpec((1, SSD_CHUNK, SSD_INNER), lambda bi, c: (bi, cidx(c), 0)),
                  pl.BlockSpec((1, SSD_CHUNK, gs), lambda bi, c: (bi, cidx(c), SSD_INNER // gs)),
                  pl.BlockSpec((1, SSD_CHUNK, gs), lambda bi, c: (bi, cidx(c), SSD_INNER // gs + 1)),
                  pl.BlockSpec((1, SSD_CHUNK, LANES), lambda bi, c: (bi, cidx(c), COL_DT // LANES)),
                  pl.BlockSpec((1, LANES), lambda bi, c: (0, 0)),
                  pl.BlockSpec((1, LANES), lambda bi, c: (0, 0)),
                  pl.BlockSpec((LANES, SSD_INNER), lambda bi, c: (0, 0))],
        out_specs=pl.BlockSpec((1, SSD_CHUNK, SSD_INNER), lambda bi, c: (bi, cidx(c), 0)),
        scratch_shapes=[pltpu.VMEM((SSD_GROUPS, SSD_STATE, SSD_GROUP_WIDTH), F32)],
        compiler_params=_cparams(("parallel", "arbitrary")),
        name="ssd_scan_bwd" if rev else "ssd_scan_fwd",
    )(xbc, xbc, xbc, proj, dtb_row, alog_row, expand)


def _merge_kernel(x_ref, yf_ref, yb_ref, xs_ref, z_ref, ao_ref, ga_ref, gs_ref, g1_ref,
                  dsk_ref, snw_ref, wssd_ref, wattn_ref, wout_ref, o_ref):
    y = yf_ref[0].astype(F32) + yb_ref[0].astype(F32) + xs_ref[0].astype(F32) * dsk_ref[...]
    z = z_ref[0].astype(F32)
    y = y * (z * jax.nn.sigmoid(z))
    parts = []
    for g in range(SSD_GROUPS):
        yg = y[:, g * SSD_GROUP_WIDTH:(g + 1) * SSD_GROUP_WIDTH]
        parts.append(yg * lax.rsqrt(jnp.mean(yg * yg, axis=-1, keepdims=True) + EPS))
    y = jnp.concatenate(parts, axis=1) * snw_ref[...]
    ssd = jnp.dot(y.astype(BF16), wssd_ref[...], preferred_element_type=F32)
    attn = jnp.dot(ao_ref[0], wattn_ref[...], preferred_element_type=F32)
    merged = (jax.nn.sigmoid(ga_ref[0].astype(F32)) * attn
              + jax.nn.sigmoid(gs_ref[0].astype(F32)) * ssd)
    out = jnp.dot(merged.astype(BF16), wout_ref[...], preferred_element_type=F32)
    o_ref[0] = x_ref[0] + g1_ref[0] * out


def _merge(x, yf, yb, xbc, proj, attn_o, modr, dsk_full, snw, w_ssd, w_attn, w_out):
    b, l, _ = x.shape
    tl = min(l, 256)
    tok = lambda w, colb: pl.BlockSpec((1, tl, w), lambda bi, i: (bi, i, colb))
    full = lambda r, c: pl.BlockSpec((r, c), lambda bi, i: (0, 0))
    return pl.pallas_call(
        _merge_kernel,
        out_shape=jax.ShapeDtypeStruct((b, l, D_MODEL), F32),
        grid=(b, l // tl),
        in_specs=[tok(D_MODEL, 0), tok(SSD_INNER, 0), tok(SSD_INNER, 0), tok(SSD_INNER, 0),
                  tok(SSD_INNER, COL_Z // SSD_INNER), tok(ATTN_WIDTH, 0),
                  tok(D_MODEL, COL_GA // D_MODEL), tok(D_MODEL, COL_GS // D_MODEL),
                  pl.BlockSpec((1, 1, D_MODEL), lambda bi, i: (bi * N_MOD + 2, 0, 0)),
                  full(1, SSD_INNER), full(1, SSD_INNER),
                  full(SSD_INNER, D_MODEL), full(ATTN_WIDTH, D_MODEL), full(D_MODEL, D_MODEL)],
        out_specs=tok(D_MODEL, 0),
        compiler_params=_cparams(("parallel", "parallel")),
        name="merge_out_proj",
    )(x, yf, yb, xbc, proj, attn_o, proj, proj, modr, dsk_full, snw, w_ssd, w_attn, w_out)


def _topk_rows(s, k):
    rowf = lax.broadcasted_iota(I32, s.shape, 0).astype(F32)
    vals, ids = [], []
    for _ in range(k):
        m = jnp.max(s, axis=0, keepdims=True)
        am = jnp.min(jnp.where(s == m, rowf, float(s.shape[0])), axis=0, keepdims=True)
        vals.append(m)
        ids.append(am)
        s = jnp.where(rowf == am, -jnp.inf, s)
    return vals, ids


def _candidate_pieces():
    pieces = []
    for a in range(PEER_TOPK):
        nb = PEER_TOPK // (a + 1)
        if nb >= SUBLANES // 2:
            for b0 in range(0, nb, SUBLANES):
                pieces.append((a, 1, b0, SUBLANES))
    a_done = max(p[0] for p in pieces) + 1
    for b in range(PEER_TOPK):
        na = PEER_TOPK // (b + 1)
        if na > a_done:
            for a0 in range(0, na, SUBLANES):
                pieces.append((a0, SUBLANES, b, 1))
    return pieces, a_done


def _route_kernel(x_ref, sh_ref, sc_ref, nw_ref, wq_ref, keys_ref, hp_ref, idx_ref, gate_ref, cnt_ref):
    h = _norm_mod(x_ref[0], nw_ref[...], sc_ref[0], sh_ref[0])
    hp_ref[0] = h
    qall = jnp.dot(h.astype(BF16), wq_ref[...], preferred_element_type=F32)
    ntok = qall.shape[0]
    tops, topi = [], []
    for c in range(2 * PEER_HEADS):
        qc = qall[:, c * PEER_HALF:(c + 1) * PEER_HALF].astype(BF16)
        s = lax.dot_general(keys_ref[c], qc, (((1,), (1,)), ((), ())), preferred_element_type=F32)
        vals, ids = _topk_rows(s, PEER_TOPK)
        tops.append(vals)
        topi.append(ids)
    pieces, a_done = _candidate_pieces()
    sub = lax.broadcasted_iota(I32, (SUBLANES, ntok), 0)
    subf = sub.astype(F32)
    pos_parts, drop_parts = [], []
    for a0, na, b0, nb in pieces:
        if na == 1:
            pos_parts.append(subf + float(a0 * PEER_TOPK + b0))
            keep = sub + b0 < PEER_TOPK // (a0 + 1)
        else:
            pos_parts.append(subf * float(PEER_TOPK) + float(a0 * PEER_TOPK + b0))
            keep = jnp.where(sub + a0 >= a_done, sub + a0, PEER_TOPK) < PEER_TOPK // (b0 + 1)
        drop_parts.append(jnp.where(keep, 0.0, -jnp.inf))
    pos = jnp.concatenate(pos_parts, axis=0)
    drop = jnp.concatenate(drop_parts, axis=0)
    idx_rows, gate_rows = [], []
    for hd in range(PEER_HEADS):
        s0 = jnp.concatenate(tops[2 * hd], axis=0)
        i0 = jnp.concatenate(topi[2 * hd], axis=0)
        s1 = jnp.concatenate(tops[2 * hd + 1], axis=0)
        i1 = jnp.concatenate(topi[2 * hd + 1], axis=0)
        cparts, iparts = [], []
        for a0, na, b0, nb in pieces:
            if na == 1:
                cparts.append(tops[2 * hd][a0] + s1[b0:b0 + nb])
                iparts.append(topi[2 * hd][a0] * float(N_KEYS) + i1[b0:b0 + nb])
            else:
                cparts.append(s0[a0:a0 + na] + tops[2 * hd + 1][b0])
                iparts.append(i0[a0:a0 + na] * float(N_KEYS) + topi[2 * hd + 1][b0])
        cand = jnp.concatenate(cparts, axis=0) + drop
        cidx = jnp.concatenate(iparts, axis=0)
        best = []
        for _ in range(PEER_TOPK):
            m = jnp.max(cand, axis=0, keepdims=True)
            first = jnp.min(jnp.where(cand == m, pos, float(PEER_TOPK * PEER_TOPK)), axis=0, keepdims=True)
            sel = pos == first
            idx_rows.append(jnp.max(jnp.where(sel, cidx, -1.0), axis=0, keepdims=True))
            best.append(m)
            cand = jnp.where(sel, -jnp.inf, cand)
        bs = jnp.concatenate(best, axis=0)
        p = jnp.exp(bs - best[0])
        gate_rows.append(p / jnp.sum(p, axis=0, keepdims=True))
    idx_t = jnp.concatenate(idx_rows, axis=0)
    gate_t = jnp.concatenate(gate_rows, axis=0)
    idx_t, gate_t, n_zero = _partition_by_class(idx_t, gate_t)
    idx_ref[0] = (idx_t * float(SLAB_ROWS)).T.astype(I32)
    gate_ref[0] = gate_t.T
    cnt_ref[0, 0] = n_zero.astype(I32)


def _route(x1, modr, norm_w, wq, keys):
    b, l, _ = x1.shape
    tl = min(l, 256)
    tok = lambda w: pl.BlockSpec((1, tl, w), lambda bi, i: (bi, i, 0))
    return pl.pallas_call(
        _route_kernel,
        out_shape=(jax.ShapeDtypeStruct((b, l, D_MODEL), F32),
                   jax.ShapeDtypeStruct((b, l, N_SEL), I32),
                   jax.ShapeDtypeStruct((b, l, N_SEL), F32),
                   jax.ShapeDtypeStruct((b, l // tl, 1, tl), I32)),
        grid=(b, l // tl),
        in_specs=[tok(D_MODEL),
                  pl.BlockSpec((1, 1, D_MODEL), lambda bi, i: (bi * N_MOD + 3, 0, 0)),
                  pl.BlockSpec((1, 1, D_MODEL), lambda bi, i: (bi * N_MOD + 4, 0, 0)),
                  pl.BlockSpec((1, D_MODEL), lambda bi, i: (0, 0)),
                  pl.BlockSpec((D_MODEL, PEER_HEADS * PEER_KEY_DIM), lambda bi, i: (0, 0)),
                  pl.BlockSpec((2 * PEER_HEADS, N_KEYS, PEER_HALF), lambda bi, i: (0, 0, 0))],
        out_specs=(tok(D_MODEL), tok(N_SEL), tok(N_SEL),
                   pl.BlockSpec((1, 1, 1, tl), lambda bi, i: (bi, i, 0, 0))),
        compiler_params=_cparams(("parallel", "parallel")),
        name="peer_route",
    )(x1, modr, modr, norm_w, wq, keys)


def _pack_rows(tab):
    n = tab.shape[0]
    bits = lax.bitcast_convert_type(tab.astype(BF16), jnp.uint16).astype(jnp.uint32)
    lo = bits[:, :D_MODEL // 2]
    hi = bits[:, D_MODEL // 2:]
    words = lax.bitcast_convert_type(lo | (hi << 16), I32)
    return words.reshape(n, ROW_WORDS, LANES)


def _gelu_tanh(x):
    return 0.5 * x * (1.0 + jnp.tanh(math.sqrt(2.0 / math.pi) * (x + 0.044715 * (x * x * x))))


def _unpack_rows(words):
    return pltpu.bitcast(words << 16, F32), pltpu.bitcast(words & jnp.int32(-65536), F32)


SLAB_ROWS = 2 * ROW_WORDS
PEER_CHUNK = 16
PEER_COMMON_CHUNKS = 5


def _expert_class(e):
    return lax.population_count(e) & 1


def _class_tables(tab_u, tab_v):
    slabs = jnp.stack([_pack_rows(tab_u), _pack_rows(tab_v)], axis=2).reshape(N_EXPERTS, SLAB_ROWS, LANES)
    even = 2 * jnp.arange(N_EXPERTS // 2, dtype=I32)
    first = even + _expert_class(even)
    second = even + 1 - _expert_class(even)
    shape = (N_EXPERTS // 2 * SLAB_ROWS, LANES)
    return slabs[first].reshape(shape), slabs[second].reshape(shape)


def _compact_rows(vals, shift, valid, up):
    n = valid.shape[0]
    for bit in range(n.bit_length() - 1):
        step = 1 << bit
        roll = (lambda a: pltpu.roll(a, (n - step) if up else step, axis=0))
        move = valid * ((shift >> bit) & 1)
        bring = roll(move) != 0
        vals = [jnp.where(bring, roll(v), v) for v in vals]
        shift = jnp.where(bring, roll(shift), shift)
        valid = jnp.where(bring, 1, valid - move)
    return vals


def _partition_by_class(idx_t, gate_t):
    n = idx_t.shape[0]
    e = idx_t.astype(I32)
    cls = _expert_class(e)
    local = (e >> 1).astype(F32)
    is_one = cls.astype(F32).astype(BF16)
    is_zero = (1 - cls).astype(F32).astype(BF16)
    ri = lax.broadcasted_iota(I32, (n, n), 0)
    ci = lax.broadcasted_iota(I32, (n, n), 1)
    before = jnp.where(ci < ri, 1.0, 0.0).astype(BF16)
    after = jnp.where(ci > ri, 1.0, 0.0).astype(BF16)
    ones_before = jnp.dot(before, is_one, preferred_element_type=F32).astype(I32)
    zeros_after = jnp.dot(after, is_zero, preferred_element_type=F32).astype(I32)
    n_zero = jnp.sum(is_zero.astype(F32), axis=0, keepdims=True)
    z_idx, z_gate = _compact_rows([local, gate_t], ones_before, 1 - cls, up=True)
    o_idx, o_gate = _compact_rows([local, gate_t], zeros_after, cls, up=False)
    first = lax.broadcasted_iota(I32, idx_t.shape, 0).astype(F32) < n_zero
    return jnp.where(first, z_idx, o_idx), jnp.where(first, z_gate, o_gate), n_zero


def _token_tile(rows, tt):
    return jnp.concatenate([rows[tt:tt + 1, j * LANES:(j + 1) * LANES] for j in range(D_MODEL // LANES)],
                           axis=0)


def _rows_from_tiles(tiles8):
    return jnp.concatenate([jnp.concatenate([t8[j:j + 1] for t8 in tiles8], axis=0)
                            for j in range(D_MODEL // LANES)], axis=1)


def _class_pass_parts(idx_ref, x_ref, tab_ref, rep_ref, ones_ref, *, cls, s_lo, s_hi):
    lane = lax.broadcasted_iota(I32, (1, N_SEL), 1)
    in_range = jnp.logical_and(lane >= s_lo * PEER_CHUNK, lane < s_hi * PEER_CHUNK)
    crows = PEER_CHUNK * SLAB_ROWS
    nsel = (s_hi - s_lo) * PEER_CHUNK
    nrows = nsel * ROW_WORDS
    half = SUBLANES // 2

    def gather_chunk(tile, base, c):
        k0 = base + c * PEER_CHUNK
        r0 = (c - s_lo) * crows
        r0 = r0 if isinstance(c, int) else pl.multiple_of(r0, crows)
        for kk in range(PEER_CHUNK):
            r = pl.multiple_of(idx_ref[k0 + kk], SLAB_ROWS)
            tile[pl.ds(r0 + kk * SLAB_ROWS, SLAB_ROWS), :] = tab_ref[pl.ds(r, SLAB_ROWS), :]

    def valid_row(n_zero):
        mine = (lane >= n_zero) if cls else (lane < n_zero)
        return jnp.where(jnp.logical_and(mine, in_range), 1.0, 0.0)

    def needed_chunks(n_zero):
        if cls:
            return jnp.maximum(n_zero // PEER_CHUNK, s_lo), s_hi
        return s_lo, jnp.minimum((n_zero + PEER_CHUNK - 1) // PEER_CHUNK, s_hi)

    def activation(tile, x8):
        acc = None
        for s in range(ROW_WORDS):
            u_lo, u_hi = _unpack_rows(tile[pl.ds(2 * s, nsel, stride=SLAB_ROWS), :])
            term = u_lo * x8[s:s + 1] + u_hi * x8[half + s:half + s + 1]
            acc = term if acc is None else acc + term
        before, after = s_lo * PEER_CHUNK, N_SEL - s_hi * PEER_CHUNK
        full = jnp.concatenate(([jnp.zeros((before, LANES), F32)] if before else []) + [acc]
                               + ([jnp.zeros((after, LANES), F32)] if after else []), axis=0)
        return jnp.sum(full.T, axis=0, keepdims=True)

    def weighted_sum(tile, w_row):
        v_lo, v_hi = _unpack_rows(tile[pl.ds(1, nrows, stride=2), :])
        wrows = jnp.dot((rep_ref[...] * w_row).astype(BF16), ones_ref[...], preferred_element_type=F32)
        acc_lo = jnp.sum((v_lo * wrows).reshape(nrows // SUBLANES, SUBLANES, LANES), axis=0)
        acc_hi = jnp.sum((v_hi * wrows).reshape(nrows // SUBLANES, SUBLANES, LANES), axis=0)
        return jnp.concatenate([acc_lo[:half] + acc_lo[half:], acc_hi[:half] + acc_hi[half:]], axis=0)

    return gather_chunk, valid_row, needed_chunks, activation, weighted_sum


def _class_body(idx_ref, cnt_ref, x_ref, gate_ref, last, tab_ref, rep_ref, rep_rare_ref, ones_ref,
                out_ref, tiles, wbufs, tile_rare, *, cls):
    nchunk = N_SEL // PEER_CHUNK
    if cls:
        s_lo, s_hi, r_lo, r_hi = nchunk - PEER_COMMON_CHUNKS, nchunk, 0, nchunk - PEER_COMMON_CHUNKS
    else:
        s_lo, s_hi, r_lo, r_hi = 0, PEER_COMMON_CHUNKS, PEER_COMMON_CHUNKS, nchunk
    gather_chunk, valid_row, _, activation, weighted_sum = _class_pass_parts(
        idx_ref, x_ref, tab_ref, rep_ref, ones_ref, cls=cls, s_lo=s_lo, s_hi=s_hi)
    if last:
        prev_ref, x1_ref, g2_ref = last

    def group_rows(t0):
        return pl.ds(t0, PEER_TOK_GROUP)

    def emit(t0, tiles8):
        rows = _rows_from_tiles(tiles8)
        if last:
            rows = x1_ref[group_rows(t0), :] + g2_ref[0] * (prev_ref[group_rows(t0), :] + rows)
        out_ref[group_rows(t0), :] = rows

    def emit_more(t0, tt, o8):
        zero = jnp.zeros_like(o8)
        rows = _rows_from_tiles([o8 if i == tt else zero for i in range(PEER_TOK_GROUP)])
        out_ref[group_rows(t0), :] = out_ref[group_rows(t0), :] + (g2_ref[0] * rows if last else rows)

    @pl.when(pl.program_id(0) == 0)
    def _():
        for buf in tiles + [tile_rare]:
            buf[...] = jnp.zeros_like(buf)

    def first_stage(g, par):
        t0 = pl.multiple_of(g * PEER_TOK_GROUP, PEER_TOK_GROUP)
        x_rows = x_ref[group_rows(t0), :]
        acts, valid = [], []

        def token(tt):
            valid.append(valid_row(cnt_ref[t0 + tt]))
            for c in range(s_lo, s_hi):
                gather_chunk(tiles[par].at[tt], (t0 + tt) * N_SEL, c)
            acts.append(activation(tiles[par].at[tt], _token_tile(x_rows, tt)))

        def finish():
            act = jnp.concatenate(acts, axis=0)
            wbufs[par][...] = (gate_ref[group_rows(t0), :] * _gelu_tanh(act) * jnp.concatenate(valid, axis=0))

        return token, finish

    def second_stage(g, par):
        t0 = pl.multiple_of(g * PEER_TOK_GROUP, PEER_TOK_GROUP)
        w = wbufs[par][...]
        sums = []

        def token(tt):
            sums.append(weighted_sum(tiles[par].at[tt], w[tt:tt + 1, :]))

        return token, lambda: emit(t0, sums)

    def step(first, second):
        tok1, fin1 = first_stage(*first) if first else (None, None)
        tok2, fin2 = second_stage(*second) if second else (None, None)
        for tt in range(PEER_TOK_GROUP):
            if tok1:
                tok1(tt)
            if tok2:
                tok2(tt)
        if fin2:
            fin2()
        if fin1:
            fin1()

    ngroups = PEER_BLOCK // PEER_TOK_GROUP
    unroll = 2
    assert (ngroups - unroll) % unroll == 0

    def steps(first_group, parity0):
        for k in range(unroll):
            par = (parity0 + k) % 2
            step((first_group + k, par), (first_group + k - 1, 1 - par))

    step((0, 0), None)

    def body(j, carry):
        steps(1 + unroll * j, 1)
        return carry

    lax.fori_loop(0, (ngroups - unroll) // unroll, body, 0)
    for g in range(ngroups - unroll + 1, ngroups):
        step((g, g % 2), (g - 1, (g - 1) % 2))
    step(None, (ngroups - 1, (ngroups - 1) % 2))

    r_gather, r_valid, r_chunks, r_activation, r_weighted_sum = _class_pass_parts(
        idx_ref, x_ref, tab_ref, rep_rare_ref, ones_ref, cls=cls, s_lo=r_lo, s_hi=r_hi)

    def needs_rare(n_zero):
        return (n_zero < r_hi * PEER_CHUNK) if cls else (n_zero > r_lo * PEER_CHUNK)

    def rare_token(t0, tt):
        n_zero = cnt_ref[t0 + tt]

        @pl.when(needs_rare(n_zero))
        def _():
            def chunk(c, carry):
                r_gather(tile_rare, (t0 + tt) * N_SEL, c)
                return carry
            lax.fori_loop(*r_chunks(n_zero), chunk, 0)
            act = r_activation(tile_rare, _token_tile(x_ref[group_rows(t0), :], tt))
            w = gate_ref[group_rows(t0), :][tt:tt + 1] * _gelu_tanh(act) * r_valid(n_zero)
            emit_more(t0, tt, r_weighted_sum(tile_rare, w))

    def rare_group(g, carry):
        t0 = pl.multiple_of(g * PEER_TOK_GROUP, PEER_TOK_GROUP)
        counts = [cnt_ref[t0 + tt] for tt in range(PEER_TOK_GROUP)]
        extreme = functools.reduce(jnp.minimum if cls else jnp.maximum, counts)

        @pl.when(needs_rare(extreme))
        def _():
            for tt in range(PEER_TOK_GROUP):
                rare_token(t0, tt)
        return carry

    lax.fori_loop(0, ngroups, rare_group, 0)


def _class_first_kernel(idx_ref, cnt_ref, x_ref, gate_ref, tab_ref, rep_ref, rep_rare_ref, ones_ref,
                        out_ref, tile0, tile1, wbuf0, wbuf1, tile_rare, *, cls):
    _class_body(idx_ref, cnt_ref, x_ref, gate_ref, None, tab_ref, rep_ref, rep_rare_ref, ones_ref,
                out_ref, [tile0, tile1], [wbuf0, wbuf1], tile_rare, cls=cls)


def _class_last_kernel(idx_ref, cnt_ref, x_ref, gate_ref, prev_ref, x1_ref, g2_ref, tab_ref, rep_ref,
                       rep_rare_ref, ones_ref, out_ref, tile0, tile1, wbuf0, wbuf1, tile_rare, *, cls):
    _class_body(idx_ref, cnt_ref, x_ref, gate_ref, (prev_ref, x1_ref, g2_ref), tab_ref, rep_ref,
                rep_rare_ref, ones_ref, out_ref, [tile0, tile1], [wbuf0, wbuf1], tile_rare, cls=cls)


def _class_pass(idx_flat, cnt_flat, h, gates, last, tab, *, cls, seq_len):
    t = gates.shape[0]
    ncommon = PEER_COMMON_CHUNKS * PEER_CHUNK
    nrare = N_SEL - ncommon

    def rep(first_sel, nsel):
        rows = jnp.arange(nsel * ROW_WORDS)[:, None] // ROW_WORDS + first_sel
        return (rows == jnp.arange(N_SEL)[None, :]).astype(F32)

    rep_common = rep(nrare if cls else 0, ncommon)
    rep_rare = rep(0 if cls else ncommon, nrare)
    ones = jnp.ones((N_SEL, LANES), BF16)
    const = lambda a: pl.BlockSpec(a.shape, lambda i: (0, 0))
    tok_spec = pl.BlockSpec((PEER_BLOCK, D_MODEL), lambda i: (i, 0))
    in_specs = [pl.BlockSpec((PEER_BLOCK * N_SEL,), lambda i: (i,), memory_space=pltpu.SMEM),
                pl.BlockSpec((PEER_BLOCK,), lambda i: (i,), memory_space=pltpu.SMEM),
                tok_spec, pl.BlockSpec((PEER_BLOCK, N_SEL), lambda i: (i, 0))]
    args = [idx_flat, cnt_flat, h, gates]
    if last is not None:
        gate_row = lambda i: ((i * PEER_BLOCK // seq_len) * N_MOD + N_MOD - 1, 0, 0)
        in_specs += [tok_spec, tok_spec, pl.BlockSpec((1, 1, D_MODEL), gate_row)]
        args += list(last)
    in_specs += [pl.BlockSpec(memory_space=pltpu.VMEM), const(rep_common), const(rep_rare), const(ones)]
    args += [tab, rep_common, rep_rare, ones]
    tile = pltpu.VMEM((PEER_TOK_GROUP, ncommon * SLAB_ROWS, LANES), I32)
    wbuf = pltpu.VMEM((PEER_TOK_GROUP, N_SEL), F32)
    return pl.pallas_call(
        functools.partial(_class_first_kernel if last is None else _class_last_kernel, cls=cls),
        out_shape=jax.ShapeDtypeStruct((t, D_MODEL), F32),
        grid=(t // PEER_BLOCK,),
        in_specs=in_specs,
        out_specs=tok_spec,
        scratch_shapes=[tile, tile, wbuf, wbuf, pltpu.VMEM((nrare * SLAB_ROWS, LANES), I32)],
        compiler_params=_cparams(("arbitrary",)),
        name="peer_class%d" % cls,
    )(*args)


def _prepare(rel_bias, ada_w, ada_b, norm1_w, norm2_w, w_in, q_norm_w, k_norm_w, attn_sink, conv_w,
             conv_b, a_log, dt_bias, d_skip, ssd_norm_w, w_attn_br, w_ssd_br, w_out, peer_wq,
             peer_keys, peer_u, peer_v):
    lyr = 0
    w = w_in[lyr]
    o = 0
    parts = {}
    for name, width in (("q", ATTN_WIDTH), ("k", KV_WIDTH), ("v", KV_WIDTH), ("z", SSD_INNER),
                        ("xbc", CONV_CH), ("dt", 2 * SSD_HEADS), ("ga", D_MODEL), ("gs", D_MODEL)):
        parts[name] = w[:, o:o + width]
        o += width
    pad = jnp.zeros((D_MODEL, PROJ_W - (COL_DT + 2 * SSD_HEADS)), w.dtype)
    w_all = jnp.concatenate([parts["z"], parts["q"], parts["ga"], parts["gs"], parts["xbc"],
                             parts["k"], parts["v"], parts["dt"], pad], axis=1).astype(BF16)
    lane_pad = LANES - 2 * SSD_HEADS
    expand = (jnp.arange(SSD_INNER)[None, :] // SSD_HEAD_DIM == jnp.arange(LANES)[:, None])
    tab0, tab1 = _class_tables(peer_u[lyr], peer_v[lyr])
    return dict(
        ada_w=ada_w[lyr], ada_b=ada_b[lyr],
        norm1_w=norm1_w[lyr].reshape(1, D_MODEL), norm2_w=norm2_w[lyr].reshape(1, D_MODEL),
        w_all=w_all,
        bias_tab=_bias_table(rel_bias), sink=attn_sink[lyr].astype(F32),
        qw=(jnp.tile(q_norm_w[lyr], N_Q_HEADS) * HEAD_DIM ** -0.5).reshape(1, ATTN_WIDTH),
        kw=jnp.tile(k_norm_w[lyr], N_KV_HEADS).reshape(1, KV_WIDTH),
        conv_w=conv_w[lyr], conv_b=conv_b[lyr],
        dtb_row=jnp.pad(dt_bias[lyr].astype(F32).reshape(1, -1), ((0, 0), (0, lane_pad))),
        alog_row=jnp.pad(a_log[lyr].astype(F32).reshape(1, -1), ((0, 0), (0, lane_pad))),
        expand_fwd=expand.astype(BF16),
        expand_bwd=jnp.roll(expand, SSD_HEADS, axis=0).astype(BF16),
        dsk_full=jnp.repeat(d_skip[lyr], SSD_HEAD_DIM).reshape(1, SSD_INNER),
        snw=ssd_norm_w[lyr].reshape(1, SSD_INNER),
        w_ssd=w_ssd_br[lyr].astype(BF16), w_attn=w_attn_br[lyr].astype(BF16), w_out=w_out[lyr].astype(BF16),
        wq=peer_wq[lyr].astype(BF16),
        keys=peer_keys[lyr].reshape(2 * PEER_HEADS, N_KEYS, PEER_HALF).astype(BF16),
        tab0=tab0, tab1=tab1,
    )


def _token_mixer_stage(x, modr, p):
    proj = _in_projection(x, modr, p["norm1_w"], p["w_all"])
    attn_o = _attention(proj, p["bias_tab"], p["sink"], p["qw"], p["kw"])
    xbc = _conv_silu(proj, p["conv_w"], p["conv_b"])
    yf = _ssd_scan(xbc, proj, p["dtb_row"], p["alog_row"], p["expand_fwd"], rev=False)
    yb = _ssd_scan(xbc, proj, p["dtb_row"], p["alog_row"], p["expand_bwd"], rev=True)
    return _merge(x, yf, yb, xbc, proj, attn_o, modr, p["dsk_full"], p["snw"],
                  p["w_ssd"], p["w_attn"], p["w_out"])


def _peer_stage(x1, modr, p):
    b, l, _ = x1.shape
    t = b * l
    assert l % PEER_BLOCK == 0
    h, idx, gates, cnt = _route(x1, modr, p["norm2_w"], p["wq"], p["keys"])
    args = (idx.reshape(t * N_SEL), cnt.reshape(t), h.reshape(t, D_MODEL), gates.reshape(t, N_SEL))
    out = _class_pass(*args, None, p["tab0"], cls=0, seq_len=l)
    y = _class_pass(*args, (out, x1.reshape(t, D_MODEL), modr), p["tab1"], cls=1, seq_len=l)
    return y.reshape(b, l, D_MODEL)


def _encoder(x, c, p):
    nb = c.shape[0]
    modr = _modulation(c, p["ada_w"], p["ada_b"]).reshape(nb * N_MOD, 1, D_MODEL)
    x1 = _token_mixer_stage(x, modr, p)
    return _peer_stage(x1, modr, p)


def kernel(x_prompt, x_sample, c_prompt, c_sample, rel_bias, ada_w, ada_b, norm1_w, norm2_w, w_in,
           q_norm_w, k_norm_w, attn_sink, conv_w, conv_b, a_log, dt_bias, d_skip, ssd_norm_w,
           w_attn_br, w_ssd_br, w_out, peer_wq, peer_keys, peer_u, peer_v):
    p = _prepare(rel_bias, ada_w, ada_b, norm1_w, norm2_w, w_in, q_norm_w, k_norm_w, attn_sink,
                 conv_w, conv_b, a_log, dt_bias, d_skip, ssd_norm_w, w_attn_br, w_ssd_br, w_out,
                 peer_wq, peer_keys, peer_u, peer_v)
    return (_encoder(x_prompt, c_prompt, p), _encoder(x_sample, c_sample, p))
```
